```python
import jax, jax.numpy as jnp
from jax import lax
import numpy as np

D_MODEL = 2048
BATCH = 4
SEQ = 4096
DEPTH = 1

ATTN_HEADS = 16
ATTN_HEAD_DIM = 128
ATTN_WIDTH = ATTN_HEADS * ATTN_HEAD_DIM
DILATED_PATTERNS = ((128, 1), (512, 4), (2048, 16))
BAND_BLOCK = 128
MLSTM_HEADS = 8
MLSTM_QK_DIM = 128
MLSTM_V_DIM = 256
MLSTM_QK_WIDTH = MLSTM_HEADS * MLSTM_QK_DIM
MLSTM_V_WIDTH = MLSTM_HEADS * MLSTM_V_DIM
MLSTM_CHUNK = 64
CONV_WIDTH = 4
NORM_EPS = 1e-6

IN_SPLIT_SIZES = (
    ATTN_WIDTH, ATTN_WIDTH, ATTN_WIDTH,
    ATTN_WIDTH,
    2 * MLSTM_QK_WIDTH,
    MLSTM_V_WIDTH,
    MLSTM_HEADS, MLSTM_HEADS,
    MLSTM_V_WIDTH,
    MLSTM_V_WIDTH,
    D_MODEL, D_MODEL,
)
IN_WIDTH = int(sum(IN_SPLIT_SIZES))
IN_SPLIT_POINTS = [int(v) for v in np.cumsum(IN_SPLIT_SIZES)[:-1]]

kernel_name = "hybrid_dilated_attn_mlstm_gated_block"


def _rmsnorm(x, g):
    xf = x.astype(jnp.float32)
    y = xf * lax.rsqrt(jnp.mean(xf * xf, axis=-1, keepdims=True) + NORM_EPS)
    return (y * g.astype(jnp.float32)).astype(x.dtype)


def _causal_conv_silu(u, w, b):
    k_taps = w.shape[0]
    s = u.shape[1]
    up = jnp.pad(u, ((0, 0), (k_taps - 1, 0), (0, 0)))
    y = b
    for j in range(k_taps):
        y = y + up[:, j:j + s] * w[j]
    return jax.nn.silu(y)


def _dilated_band(q, k, v, slopes, window, dilation):
    bsz, s, h, dh = q.shape
    n_back = window // dilation
    span = dilation * BAND_BLOCK
    s_pad = -(-s // span) * span
    n_blk = s_pad // span
    length = s_pad // dilation

    def phase(a):
        a = jnp.pad(a, ((0, 0), (0, s_pad - s), (0, 0), (0, 0)))
        a = a.reshape(bsz, length, dilation, h, dh).transpose(0, 2, 1, 3, 4)
        return a.reshape(bsz, dilation, n_blk, BAND_BLOCK, h, dh)

    def with_prev(a):
        prev = jnp.pad(a, ((0, 0), (0, 0), (1, 0), (0, 0), (0, 0), (0, 0)))[:, :, :-1]
        return jnp.concatenate([prev, a], axis=3)

    qb = phase(q)
    kk = with_prev(phase(k))
    vv = with_prev(phase(v))
    scores = jnp.einsum('brnqhd,brnkhd->brnhqk', qb, kk,
                        preferred_element_type=jnp.float32)
    qi = jnp.arange(BAND_BLOCK)
    ki = jnp.arange(2 * BAND_BLOCK)
    dist = BAND_BLOCK + qi[:, None] - ki[None, :]
    key_pos = (jnp.arange(n_blk)[:, None] - 1) * BAND_BLOCK + ki[None, :]
    valid = (dist >= 0)[None] & (dist <= n_back)[None] & (key_pos >= 0)[:, None, :]
    alibi = -slopes[:, None, None] * (dist * dilation).astype(jnp.float32)[None]
    scores = jnp.where(valid[None, None, :, None], scores + alibi, -jnp.inf)
    mx = jnp.max(scores, axis=-1)
    p = jnp.exp(scores - mx[..., None])
    den = jnp.sum(p, axis=-1)
    num = jnp.einsum('brnhqk,brnkhd->brnqhd', p, vv.astype(jnp.float32))

    num = num.reshape(bsz, dilation, length, h, dh).transpose(0, 2, 1, 3, 4)
    num = num.reshape(bsz, s_pad, h, dh)[:, :s]

    def unphase_stat(a):
        a = a.transpose(0, 1, 2, 4, 3).reshape(bsz, dilation, length, h)
        return a.transpose(0, 2, 1, 3).reshape(bsz, s_pad, h)[:, :s]

    return num, unphase_stat(den), unphase_stat(mx)


def _dilated_attention(q, k, v):
    q = q * (ATTN_HEAD_DIM ** -0.5)
    slopes = jnp.exp2(-8.0 * jnp.arange(1, ATTN_HEADS + 1, dtype=jnp.float32) / ATTN_HEADS)
    nums, dens, mxs = [], [], []
    for window, dilation in DILATED_PATTERNS:
        n_, d_, m_ = _dilated_band(q, k, v, slopes, window, dilation)
        nums.append(n_); dens.append(d_); mxs.append(m_)
    m_all = jnp.max(jnp.stack(mxs, 0), axis=0)
    wts = [jnp.exp(m_ - m_all) for m_ in mxs]
    num = sum(w_[..., None] * n_ for w_, n_ in zip(wts, nums))
    den = sum(w_ * d_ for w_, d_ in zip(wts, dens))
    return num / den[..., None]


def _mlstm_chunkwise(q, k, v, ig, lf):
    bsz, s, h, dk = q.shape
    dv = v.shape[-1]
    nc = s // MLSTM_CHUNK
    L = MLSTM_CHUNK

    def chunks(a):
        a = a.reshape((bsz, nc, L, h) + a.shape[3:])
        perm = (1, 0, 3, 2) + tuple(range(4, a.ndim))
        return a.transpose(perm)

    xs = (chunks(q), chunks(k * (dk ** -0.5)), chunks(v), chunks(ig), chunks(lf))
    causal = jnp.tril(jnp.ones((L, L), dtype=bool))

    def step(carry, inp):
        c_st, n_st, m_st = carry
        qc, kc, vc, igc, lfc = inp
        b = jnp.cumsum(lfc, axis=-1)
        a = b + m_st[..., None]
        dmat = b[..., :, None] - b[..., None, :] + igc[..., None, :]
        dmat = jnp.where(causal, dmat, -jnp.inf)
        m_t = jnp.maximum(a, jnp.max(dmat, axis=-1))
        w_intra = jnp.exp(dmat - m_t[..., None])
        w_inter = jnp.exp(a - m_t)
        sc = jnp.einsum('bhtd,bhsd->bhts', qc, kc) * w_intra
        num = (w_inter[..., None] * jnp.einsum('bhtd,bhde->bhte', qc, c_st)
               + jnp.einsum('bhts,bhse->bhte', sc, vc))
        den = w_inter * jnp.einsum('bhtd,bhd->bht', qc, n_st) + jnp.sum(sc, axis=-1)
        h_out = num / jnp.maximum(jnp.abs(den), jnp.exp(-m_t))[..., None]
        b_last = b[..., -1]
        g = b_last[..., None] - b + igc
        m_new = jnp.maximum(b_last + m_st, jnp.max(g, axis=-1))
        w_old = jnp.exp(b_last + m_st - m_new)
        w_s = jnp.exp(g - m_new[..., None])
        c_new = w_old[..., None, None] * c_st + jnp.einsum('bhs,bhsd,bhse->bhde', w_s, kc, vc)
        n_new = w_old[..., None] * n_st + jnp.einsum('bhs,bhsd->bhd', w_s, kc)
        return (c_new, n_new, m_new), h_out

    init = (jnp.zeros((bsz, h, dk, dv), jnp.float32),
            jnp.zeros((bsz, h, dk), jnp.float32),
            jnp.zeros((bsz, h), jnp.float32))
    _, hs = lax.scan(step, init, xs)
    return hs.transpose(1, 0, 3, 2, 4).reshape(bsz, s, h, dv)


def setup_inputs(seed: int = 0) -> dict:
    key = jax.random.key(seed)
    ks = jax.random.split(key, 14)
    f32 = jnp.float32
    x = jax.random.normal(ks[0], (BATCH, SEQ, D_MODEL), f32)
    norm_g = 1.0 + 0.02 * jax.random.normal(ks[1], (D_MODEL,), f32)
    w_in = jax.random.normal(ks[2], (D_MODEL, IN_WIDTH), f32) * D_MODEL ** -0.5
    b_i = 0.1 * jax.random.normal(ks[3], (MLSTM_HEADS,), f32)
    b_f = jnp.linspace(3.0, 6.0, MLSTM_HEADS, dtype=f32) + 0.1 * jax.random.normal(ks[4], (MLSTM_HEADS,), f32)
    b_if = jnp.concatenate([b_i, b_f])
    conv_w = jax.random.normal(ks[5], (CONV_WIDTH, 2 * MLSTM_QK_WIDTH), f32) * CONV_WIDTH ** -0.5
    conv_b = 0.02 * jax.random.normal(ks[6], (2 * MLSTM_QK_WIDTH,), f32)
    mlstm_norm_g = 1.0 + 0.02 * jax.random.normal(ks[7], (MLSTM_V_WIDTH,), f32)
    w_attn_branch = jax.random.normal(ks[8], (ATTN_WIDTH, D_MODEL), f32) * ATTN_WIDTH ** -0.5
    w_mlstm_branch = jax.random.normal(ks[9], (MLSTM_V_WIDTH, D_MODEL), f32) * MLSTM_V_WIDTH ** -0.5
    w_out = jax.random.normal(ks[10], (D_MODEL, D_MODEL), f32) * D_MODEL ** -0.5
    final_norm_g = 1.0 + 0.02 * jax.random.normal(ks[11], (D_MODEL,), f32)
    return {"x": x, "norm_g": norm_g, "w_in": w_in, "b_if": b_if, "conv_w": conv_w,
            "conv_b": conv_b, "mlstm_norm_g": mlstm_norm_g, "w_attn_branch": w_attn_branch,
            "w_mlstm_branch": w_mlstm_branch, "w_out": w_out, "final_norm_g": final_norm_g}


def reference(x, norm_g, w_in, b_if, conv_w, conv_b, mlstm_norm_g, w_attn_branch,
              w_mlstm_branch, w_out, final_norm_g):
    bsz, s, _ = x.shape
    f32 = jnp.float32
    for _layer in range(DEPTH):
        hn = _rmsnorm(x, norm_g)
        proj = jnp.einsum('bsd,de->bse', hn, w_in)
        (aq, ak, av, az, mqk, mv, mi, mf, mo, mz, gate_a, gate_m) = jnp.split(
            proj, IN_SPLIT_POINTS, axis=-1)

        heads = lambda t: t.reshape(bsz, s, ATTN_HEADS, ATTN_HEAD_DIM)
        attn = _dilated_attention(heads(aq), heads(ak), heads(av)).reshape(bsz, s, ATTN_WIDTH)
        attn = attn.astype(x.dtype) * jax.nn.silu(az)
        y_a = jnp.einsum('bse,ed->bsd', attn, w_attn_branch)

        qk = _causal_conv_silu(mqk, conv_w, conv_b)
        mq, mk = jnp.split(qk, 2, axis=-1)
        mq = mq.reshape(bsz, s, MLSTM_HEADS, MLSTM_QK_DIM).astype(f32)
        mk = mk.reshape(bsz, s, MLSTM_HEADS, MLSTM_QK_DIM).astype(f32)
        mvh = mv.reshape(bsz, s, MLSTM_HEADS, MLSTM_V_DIM).astype(f32)
        ig = mi.astype(f32) + b_if[:MLSTM_HEADS].astype(f32)
        lf = jax.nn.log_sigmoid(mf.astype(f32) + b_if[MLSTM_HEADS:].astype(f32))
        cell = _mlstm_chunkwise(mq, mk, mvh, ig, lf)
        cell = jax.nn.sigmoid(mo.astype(f32)).reshape(bsz, s, MLSTM_HEADS, MLSTM_V_DIM) * cell
        cell = cell * lax.rsqrt(jnp.mean(cell * cell, axis=-1, keepdims=True) + NORM_EPS)
        cell = cell.reshape(bsz, s, MLSTM_V_WIDTH) * mlstm_norm_g.astype(f32)
        mem = cell.astype(x.dtype) * jax.nn.silu(mz)
        y_m = jnp.einsum('bse,ed->bsd', mem, w_mlstm_branch)

        merged = jax.nn.sigmoid(gate_a) * y_a + jax.nn.sigmoid(gate_m) * y_m
        x = x + jnp.einsum('bsd,de->bse', merged, w_out)
    return _rmsnorm(x, final_norm_g)
```

```python
import jax
import jax.numpy as jnp
from jax import lax
from jax.experimental import pallas as pl
from jax.experimental.pallas import tpu as pltpu

F32 = jnp.float32
BF16 = jnp.bfloat16

D_MODEL = 2048
ATTN_HEADS = 16
ATTN_HEAD_DIM = 128
MLSTM_HEADS = 8
MLSTM_QK_DIM = 128
MLSTM_V_DIM = 256
CONV_WIDTH = 4
NORM_EPS = 1e-6
BAND = 128
SLAB = 256
GROUP_SLABS = D_MODEL // SLAB
QKV_COLS = 3 * D_MODEL
S_AZ, S_MQK, S_MV, S_MO, S_MZ, S_GA, S_GM = range(7)
N_SLAB_GROUPS = 7
GATE_LANES = 128
MASKED = -1e30
CHUNK = 128
VMEM_LIMIT = 56 * 1024 * 1024


def _cparams(sem):
    return pltpu.CompilerParams(dimension_semantics=sem, vmem_limit_bytes=VMEM_LIMIT)


def _sigmoid(x):
    return 1.0 / (1.0 + jnp.exp(-x))


def _silu(x):
    return x * _sigmoid(x)


def _log_sigmoid(x):
    return jnp.minimum(x, 0.0) - jnp.log(1.0 + jnp.exp(-jnp.abs(x)))


IN_TM = 1024
IN_TN = 1024
QKV_STEPS = QKV_COLS // IN_TN


def _in_proj_kernel(x_ref, g_ref, w_ref, wg_ref, qkv_ref, p_ref, gt_ref, hn_ref):
    n = pl.program_id(1)

    @pl.when(n == 0)
    def _():
        x = x_ref[...]
        ms = jnp.mean(x * x, axis=-1, keepdims=True)
        hn = (x * lax.rsqrt(ms + NORM_EPS) * g_ref[...]).astype(BF16)
        hn_ref[...] = hn
        gates = jnp.dot(hn, wg_ref[...], preferred_element_type=F32)
        gates_t = gates.T
        for c in range(IN_TM // CHUNK):
            gt_ref[c] = gates_t[:2 * MLSTM_HEADS, c * CHUNK:(c + 1) * CHUNK]

    acc = jnp.dot(hn_ref[...], w_ref[...], preferred_element_type=F32)

    @pl.when(n < QKV_STEPS)
    def _():
        scale = jnp.where(n < D_MODEL // IN_TN, ATTN_HEAD_DIM ** -0.5, 1.0).astype(F32)
        for j in range(IN_TN // ATTN_HEAD_DIM):
            qkv_ref[j] = acc[:, j * ATTN_HEAD_DIM:(j + 1) * ATTN_HEAD_DIM] * scale

    @pl.when(n >= QKV_STEPS)
    def _():
        for j in range(IN_TN // SLAB):
            p_ref[j] = acc[:, j * SLAB:(j + 1) * SLAB].astype(BF16)


def _in_proj(x2, norm_g, w_big, w_gate):
    t = x2.shape[0]
    n_cols = w_big.shape[1]
    grid = (t // IN_TM, n_cols // IN_TN)
    return pl.pallas_call(
        _in_proj_kernel,
        grid=grid,
        in_specs=[
            pl.BlockSpec((IN_TM, D_MODEL), lambda i, n: (i, 0)),
            pl.BlockSpec((1, D_MODEL), lambda i, n: (0, 0)),
            pl.BlockSpec((D_MODEL, IN_TN), lambda i, n: (0, n)),
            pl.BlockSpec((D_MODEL, GATE_LANES), lambda i, n: (0, 0)),
        ],
        out_specs=[
            pl.BlockSpec((IN_TN // ATTN_HEAD_DIM, IN_TM, ATTN_HEAD_DIM),
                         lambda i, n: (jnp.minimum(n, QKV_STEPS - 1), i, 0)),
            pl.BlockSpec((IN_TN // SLAB, IN_TM, SLAB),
                         lambda i, n: (jnp.maximum(n - QKV_STEPS, 0), i, 0)),
            pl.BlockSpec((IN_TM // CHUNK, 2 * MLSTM_HEADS, CHUNK), lambda i, n: (i, 0, 0)),
        ],
        out_shape=[
            jax.ShapeDtypeStruct((QKV_COLS // ATTN_HEAD_DIM, t, ATTN_HEAD_DIM), F32),
            jax.ShapeDtypeStruct(((n_cols - QKV_COLS) // SLAB, t, SLAB), BF16),
            jax.ShapeDtypeStruct((t // CHUNK, 2 * MLSTM_HEADS, CHUNK), F32),
        ],
        scratch_shapes=[pltpu.VMEM((IN_TM, D_MODEL), BF16)],
        compiler_params=_cparams(("parallel", "arbitrary")),
        name="in_proj",
    )(x2, norm_g, w_big, w_gate)


def _band_bias(slope, dilation, first):
    qi = lax.broadcasted_iota(jnp.int32, (BAND, 2 * BAND), 0)
    ki = lax.broadcasted_iota(jnp.int32, (BAND, 2 * BAND), 1)
    dist = BAND + qi - ki
    valid = (dist >= 0) & (dist <= BAND)
    if first:
        valid = valid & (ki >= BAND)
    return jnp.where(valid, -slope * (dist * dilation).astype(F32), MASKED)


def _dense_bias(slope, first):
    qi = lax.broadcasted_iota(jnp.int32, (BAND, 2 * BAND), 0)
    ki = lax.broadcasted_iota(jnp.int32, (BAND, 2 * BAND), 1)
    pq, jq = qi // 32, qi % 32
    half, pk, jk = ki // BAND, (ki % BAND) // 32, ki % 32
    dist = 4 * (jq - jk + 32 * (1 - half)) + pq - pk
    valid = (dist >= 0) & (dist <= BAND)
    if first:
        valid = valid & (half == 1)
    return jnp.where(valid, -slope * dist.astype(F32), MASKED)


def _wide_bias(slope):
    qi = lax.broadcasted_iota(jnp.int32, (2 * BAND, 2 * BAND), 0)
    ki = lax.broadcasted_iota(jnp.int32, (2 * BAND, 2 * BAND), 1)
    dist = qi - ki
    valid = (dist >= 0) & (dist <= BAND)
    return jnp.where(valid, -slope * (dist * 16).astype(F32), MASKED)


def _softmax_block(q, k, v, bias):
    s = lax.dot_general(q, k, (((1,), (1,)), ((), ())), preferred_element_type=F32) + bias
    m = jnp.max(s, axis=-1, keepdims=True)
    p = jnp.exp(s - m).astype(BF16)
    v_ext = jnp.concatenate([v, jnp.ones_like(v)], axis=1)
    r = jnp.dot(p, v_ext, preferred_element_type=F32)
    hd = ATTN_HEAD_DIM
    return r[:, :hd], r[:, hd:], jnp.broadcast_to(m, (q.shape[0], hd))


def _attn_kernel(q_ref, k_ref, v_ref, o_ref,
                 x4_ref, x16_ref, tmp_ref, num_ref, den_ref, mx_ref, bias_ref, bias16_ref, onat_ref):
    seq = q_ref.shape[0]
    n_u4, n_u16 = seq // 4, seq // 16
    head = (pl.program_id(1) + 1).astype(F32)
    slope = jnp.exp2(jnp.full((1, 1), -8.0 / ATTN_HEADS, F32) * head)
    bias_ref[0, 0] = _dense_bias(slope, False)
    bias_ref[0, 1] = _dense_bias(slope, True)
    bias_ref[1, 0] = _band_bias(slope, 4, False)
    bias_ref[1, 1] = _band_bias(slope, 4, True)
    bias16_ref[...] = _wide_bias(slope)

    for ti, src in enumerate((q_ref, k_ref, v_ref)):
        for p4 in range(4):
            rows = src[pl.ds(p4, n_u4, stride=4), :]
            tmp_ref[p4] = rows
            x4_ref[ti, p4] = rows.astype(BF16)
        for p16 in range(16):
            x16_ref[ti, p16] = tmp_ref[p16 % 4, pl.ds(p16 // 4, n_u16, stride=4), :].astype(BF16)

    def dense_body(n, carry):
        cur = pl.multiple_of(n * 32, 32)
        prev = pl.multiple_of(jnp.maximum(n - 1, 0) * 32, 32)
        first = (n == 0).astype(jnp.int32)

        def rows(ti, start):
            return [x4_ref[ti, p, pl.ds(start, 32), :] for p in range(4)]

        q = jnp.concatenate(rows(0, cur), axis=0)
        k = jnp.concatenate(rows(1, prev) + rows(1, cur), axis=0)
        v = jnp.concatenate(rows(2, prev) + rows(2, cur), axis=0)
        num, den, mx = _softmax_block(q, k, v, bias_ref[0, first])
        for p in range(4):
            part = slice(p * 32, (p + 1) * 32)
            num_ref[0, p, pl.ds(cur, 32), :] = num[part]
            den_ref[0, p, pl.ds(cur, 32), :] = den[part]
            mx_ref[0, p, pl.ds(cur, 32), :] = mx[part]
        return carry

    lax.fori_loop(0, n_u4 // 32, dense_body, 0)

    blocks4 = n_u4 // BAND

    def band_body(it, carry):
        p, n = it // blocks4, it % blocks4
        cur = pl.multiple_of(n * BAND, BAND)
        prev = pl.multiple_of(jnp.maximum(n - 1, 0) * BAND, BAND)
        first = (n == 0).astype(jnp.int32)
        q = x4_ref[0, p, pl.ds(cur, BAND), :]
        k = jnp.concatenate([x4_ref[1, p, pl.ds(prev, BAND), :], x4_ref[1, p, pl.ds(cur, BAND), :]], axis=0)
        v = jnp.concatenate([x4_ref[2, p, pl.ds(prev, BAND), :], x4_ref[2, p, pl.ds(cur, BAND), :]], axis=0)
        num, den, mx = _softmax_block(q, k, v, bias_ref[1, first])
        num_ref[1, p, pl.ds(cur, BAND), :] = num
        den_ref[1, p, pl.ds(cur, BAND), :] = den
        mx_ref[1, p, pl.ds(cur, BAND), :] = mx
        return carry

    lax.fori_loop(0, 4 * blocks4, band_body, 0)

    def wide_body(p16, carry):
        num, den, mx = _softmax_block(x16_ref[0, p16], x16_ref[1, p16], x16_ref[2, p16], bias16_ref[...])
        p4, a = p16 % 4, p16 // 4
        rows = pl.ds(a, n_u16, stride=4)
        num_ref[2, p4, rows, :] = num
        den_ref[2, p4, rows, :] = den
        mx_ref[2, p4, rows, :] = mx
        return carry

    lax.fori_loop(0, 16, wide_body, 0)

    def merge_body(it, carry):
        p, n = it // blocks4, it % blocks4
        rows = pl.ds(pl.multiple_of(n * BAND, BAND), BAND)
        m0, m1, m2 = mx_ref[0, p, rows, :], mx_ref[1, p, rows, :], mx_ref[2, p, rows, :]
        m_all = jnp.maximum(jnp.maximum(m0, m1), m2)
        w0, w1, w2 = jnp.exp(m0 - m_all), jnp.exp(m1 - m_all), jnp.exp(m2 - m_all)
        num = w0 * num_ref[0, p, rows, :] + w1 * num_ref[1, p, rows, :] + w2 * num_ref[2, p, rows, :]
        den = w0 * den_ref[0, p, rows, :] + w1 * den_ref[1, p, rows, :] + w2 * den_ref[2, p, rows, :]
        onat_ref[pl.ds(p + 4 * BAND * n, BAND, stride=4), :] = num / den
        return carry

    lax.fori_loop(0, 4 * blocks4, merge_body, 0)
    o_ref[...] = onat_ref[...].astype(o_ref.dtype)


def _attention(qkv, batch, seq):
    assert seq // 16 == 2 * BAND
    hd = ATTN_HEAD_DIM
    qkv4 = qkv.reshape(3 * ATTN_HEADS, batch, seq, hd)

    def spec(which):
        return pl.BlockSpec((None, None, seq, hd), lambda b, h: (which * ATTN_HEADS + h, b, 0, 0))

    out = pl.pallas_call(
        _attn_kernel,
        grid=(batch, ATTN_HEADS),
        in_specs=[spec(0), spec(1), spec(2)],
        out_specs=pl.BlockSpec((None, None, seq, hd), lambda b, h: (h, b, 0, 0)),
        out_shape=jax.ShapeDtypeStruct((ATTN_HEADS, batch, seq, hd), BF16),
        scratch_shapes=[
            pltpu.VMEM((3, 4, seq // 4, hd), BF16),
            pltpu.VMEM((3, 16, seq // 16, hd), BF16),
            pltpu.VMEM((4, seq // 4, hd), F32),
            pltpu.VMEM((3, 4, seq // 4, hd), F32),
            pltpu.VMEM((3, 4, seq // 4, hd), F32),
            pltpu.VMEM((3, 4, seq // 4, hd), F32),
            pltpu.VMEM((2, 2, BAND, 2 * BAND), F32),
            pltpu.VMEM((2 * BAND, 2 * BAND), F32),
            pltpu.VMEM((seq, hd), F32),
        ],
        compiler_params=_cparams(("parallel", "parallel")),
        name="dilated_attention",
    )(qkv4, qkv4, qkv4)
    return out.reshape(ATTN_HEADS, batch * seq, hd)


def _conv_silu(u_ref, start, first, w, b):
    cur = u_ref[pl.ds(start, CHUNK), :].astype(F32)
    prev_start = pl.multiple_of(jnp.maximum(start - 16, 0), 16)
    prev = u_ref[pl.ds(prev_start, 16), :].astype(F32)
    prev = jnp.where(first, 0.0, prev)
    ext = jnp.concatenate([prev, cur], axis=0)
    y = b + w[CONV_WIDTH - 1:CONV_WIDTH, :] * cur
    for back in range(1, CONV_WIDTH):
        shifted = pltpu.roll(ext, back, axis=0)[16:, :]
        y = y + w[CONV_WIDTH - 1 - back:CONV_WIDTH - back, :] * shifted
    return _silu(y)


def _mlstm_kernel(bias_ref, uq_ref, uk_ref, v_ref, o_ref, z_ref, gt_ref,
                  cwq_ref, cbq_ref, cwk_ref, cbk_ref, ng_ref, out_ref,
                  c_ref, n_ref, m_ref):
    j = pl.program_id(1)
    seq = uq_ref.shape[0]
    dk, dv = MLSTM_QK_DIM, MLSTM_V_DIM
    c_ref[...] = jnp.zeros_like(c_ref)
    n_ref[...] = jnp.zeros_like(n_ref)
    m_ref[...] = jnp.zeros_like(m_ref)

    ti = lax.broadcasted_iota(jnp.int32, (CHUNK, CHUNK), 0)
    si = lax.broadcasted_iota(jnp.int32, (CHUNK, CHUNK), 1)
    causal = ti >= si
    csum = (ti <= si).astype(F32)
    head_rows = lax.broadcasted_iota(jnp.int32, (MLSTM_HEADS, CHUNK), 0)

    def chunk_body(c, carry):
        start = pl.multiple_of(c * CHUNK, CHUNK)
        first = c == 0
        qq = _conv_silu(uq_ref, start, first, cwq_ref[...], cbq_ref[...])
        kk = _conv_silu(uk_ref, start, first, cwk_ref[...], cbk_ref[...]) * dk ** -0.5
        gates = gt_ref[c] + bias_ref[...]
        ig_all = gates[:MLSTM_HEADS]
        lf_all = _log_sigmoid(gates[MLSTM_HEADS:])
        b_all = jnp.dot(lf_all, csum, preferred_element_type=F32,
                        precision=lax.Precision.HIGHEST)
        c_all = ig_all - b_all
        for hh in range(2):
            pick = head_rows == 2 * j + hh
            q = qq[:, hh * dk:(hh + 1) * dk]
            k = kk[:, hh * dk:(hh + 1) * dk]
            qb, kb = q.astype(BF16), k.astype(BF16)
            vb = v_ref[hh, pl.ds(start, CHUNK), :]
            b_row = jnp.sum(jnp.where(pick, b_all, 0.0), axis=0, keepdims=True)
            c_row = jnp.sum(jnp.where(pick, c_all, 0.0), axis=0, keepdims=True)
            b_last = b_row[:, CHUNK - 1:CHUNK]
            cols = jnp.concatenate([b_row, c_row, jnp.zeros((6, CHUNK), F32)], axis=0).T
            b_col, c_col = cols[:, 0:1], cols[:, 1:2]

            m_prev = m_ref[hh]
            a_col = b_col + m_prev
            dmat = jnp.where(causal, b_col + c_row, MASKED)
            m_t = jnp.maximum(a_col, jnp.max(dmat, axis=-1, keepdims=True))
            w_intra = jnp.exp(dmat - m_t)
            w_inter = jnp.exp(a_col - m_t)
            sc = lax.dot_general(qb, kb, (((1,), (1,)), ((), ())),
                                 preferred_element_type=F32) * w_intra
            c_st = c_ref[hh]
            n_st = n_ref[hh]
            q_c = jnp.dot(qb, c_st.astype(BF16), preferred_element_type=F32)

            m_new = jnp.maximum(b_last + m_prev, b_last + jnp.max(c_row, axis=-1, keepdims=True))
            w_old = jnp.exp(b_last + m_prev - m_new)
            w_s = jnp.exp(b_last + c_col - m_new)
            kw = k * w_s
            lhs = jnp.concatenate([sc.astype(BF16), kw.T.astype(BF16)], axis=0)
            both = jnp.dot(lhs, vb, preferred_element_type=F32)

            num = w_inter * q_c + both[:CHUNK]
            den = (w_inter * jnp.sum(q * n_st, axis=-1, keepdims=True)
                   + jnp.sum(sc, axis=-1, keepdims=True))
            h_out = num / jnp.maximum(jnp.abs(den), jnp.exp(-m_t))

            c_ref[hh] = w_old * c_st + both[CHUNK:]
            n_ref[hh] = w_old * n_st + jnp.sum(kw, axis=0, keepdims=True)
            m_ref[hh] = m_new

            og = o_ref[hh, pl.ds(start, CHUNK), :].astype(F32)
            zg = z_ref[hh, pl.ds(start, CHUNK), :].astype(F32)
            cell = _sigmoid(og) * h_out
            cell = cell * lax.rsqrt(jnp.mean(cell * cell, axis=-1, keepdims=True) + NORM_EPS)
            cell = cell * ng_ref[:, hh * dv:(hh + 1) * dv]
            out_ref[hh, pl.ds(start, CHUNK), :] = (cell * _silu(zg)).astype(out_ref.dtype)
        return carry

    lax.fori_loop(0, seq // CHUNK, chunk_body, 0)


def _mlstm(slabs, gates_t, b_if, conv_w, conv_b, norm_g, batch, seq):
    n_slabs = slabs.shape[0]
    s4d = slabs.reshape(n_slabs, batch, seq, SLAB)
    half = MLSTM_HEADS // 2

    def pair_spec(group):
        return pl.BlockSpec((2, None, seq, SLAB),
                            lambda b, j: (group * GROUP_SLABS // 2 + j, b, 0, 0))

    out = pl.pallas_call(
        _mlstm_kernel,
        grid=(batch, half),
        in_specs=[
            pl.BlockSpec((2 * MLSTM_HEADS, 1), lambda b, j: (0, 0)),
            pl.BlockSpec((None, None, seq, SLAB), lambda b, j: (S_MQK * GROUP_SLABS + j, b, 0, 0)),
            pl.BlockSpec((None, None, seq, SLAB), lambda b, j: (S_MQK * GROUP_SLABS + half + j, b, 0, 0)),
            pair_spec(S_MV), pair_spec(S_MO), pair_spec(S_MZ),
            pl.BlockSpec((seq // CHUNK, 2 * MLSTM_HEADS, CHUNK), lambda b, j: (b, 0, 0)),
            pl.BlockSpec((CONV_WIDTH, SLAB), lambda b, j: (0, j)),
            pl.BlockSpec((1, SLAB), lambda b, j: (0, j)),
            pl.BlockSpec((CONV_WIDTH, SLAB), lambda b, j: (0, half + j)),
            pl.BlockSpec((1, SLAB), lambda b, j: (0, half + j)),
            pl.BlockSpec((1, 2 * MLSTM_V_DIM), lambda b, j: (0, j)),
        ],
        out_specs=pl.BlockSpec((2, None, seq, SLAB), lambda b, j: (j, b, 0, 0)),
        out_shape=jax.ShapeDtypeStruct((MLSTM_HEADS, batch, seq, SLAB), BF16),
        scratch_shapes=[
            pltpu.VMEM((2, MLSTM_QK_DIM, MLSTM_V_DIM), F32),
            pltpu.VMEM((2, 1, MLSTM_QK_DIM), F32),
            pltpu.VMEM((2, 1, 1), F32),
        ],
        compiler_params=_cparams(("parallel", "parallel")),
        name="mlstm",
    )(b_if.reshape(-1, 1), s4d, s4d, s4d, s4d, s4d, gates_t, conv_w, conv_b.reshape(1, -1), conv_w,
      conv_b.reshape(1, -1), norm_g.reshape(1, -1))
    return out.reshape(MLSTM_HEADS, batch * seq, SLAB)


OUT_TM = 256


def _merge_out_kernel(attn_ref, az_ref, mem_ref, ga_ref, gm_ref, x_ref, wa_ref, wm_ref, wo_ref,
                      fg_ref, out_ref):
    n = GROUP_SLABS
    attn = jnp.concatenate([attn_ref[h] for h in range(ATTN_HEADS)], axis=1).astype(F32)
    az = jnp.concatenate([az_ref[c] for c in range(n)], axis=1).astype(F32)
    gated = (attn * _silu(az)).astype(BF16)
    mem = jnp.concatenate([mem_ref[c] for c in range(n)], axis=1)
    y_a = jnp.dot(gated, wa_ref[...], preferred_element_type=F32)
    y_m = jnp.dot(mem, wm_ref[...], preferred_element_type=F32)
    g_a = jnp.concatenate([ga_ref[c] for c in range(n)], axis=1).astype(F32)
    g_m = jnp.concatenate([gm_ref[c] for c in range(n)], axis=1).astype(F32)
    merged = (_sigmoid(g_a) * y_a + _sigmoid(g_m) * y_m).astype(BF16)
    y = x_ref[...] + jnp.dot(merged, wo_ref[...], preferred_element_type=F32)
    y = y * lax.rsqrt(jnp.mean(y * y, axis=-1, keepdims=True) + NORM_EPS)
    out_ref[...] = y * fg_ref[...]


def _merge_out(attn, slabs, mem, x2, w_a, w_m, w_o, final_g):
    t = x2.shape[0]

    def slab_spec(group):
        return pl.BlockSpec((GROUP_SLABS, OUT_TM, SLAB), lambda i: (group, i, 0))

    def weight_spec():
        return pl.BlockSpec((D_MODEL, D_MODEL), lambda i: (0, 0), pipeline_mode=pl.Buffered(1))

    return pl.pallas_call(
        _merge_out_kernel,
        grid=(t // OUT_TM,),
        in_specs=[
            pl.BlockSpec((ATTN_HEADS, OUT_TM, ATTN_HEAD_DIM), lambda i: (0, i, 0)),
            slab_spec(S_AZ), slab_spec(0), slab_spec(S_GA), slab_spec(S_GM),
            pl.BlockSpec((OUT_TM, D_MODEL), lambda i: (i, 0)),
            weight_spec(), weight_spec(), weight_spec(),
            pl.BlockSpec((1, D_MODEL), lambda i: (0, 0)),
        ],
        out_specs=pl.BlockSpec((OUT_TM, D_MODEL), lambda i: (i, 0)),
        out_shape=jax.ShapeDtypeStruct((t, D_MODEL), F32),
        compiler_params=_cparams(("parallel",)),
        name="merge_out",
    )(attn, slabs, mem, slabs, slabs, x2, w_a, w_m, w_o, final_g)


def kernel(x, norm_g, w_in, b_if, conv_w, conv_b, mlstm_norm_g, w_attn_branch, w_mlstm_branch,
           w_out, final_norm_g):
    batch, seq, d = x.shape
    assert d == D_MODEL and seq % (16 * BAND) == 0 and (batch * seq) % IN_TM == 0
    t = batch * seq
    x2 = x.reshape(t, d)

    gate_lo = 4 * D_MODEL + 2 * MLSTM_HEADS * MLSTM_QK_DIM + MLSTM_HEADS * MLSTM_V_DIM
    gate_hi = gate_lo + 2 * MLSTM_HEADS
    w_big = jnp.concatenate([w_in[:, :gate_lo], w_in[:, gate_hi:]], axis=1).astype(BF16)
    w_gate = jnp.pad(w_in[:, gate_lo:gate_hi], ((0, 0), (0, GATE_LANES - 2 * MLSTM_HEADS))).astype(BF16)
    assert w_big.shape[1] == QKV_COLS + N_SLAB_GROUPS * D_MODEL

    qkv, slabs, gates_t = _in_proj(x2, norm_g.reshape(1, d), w_big, w_gate)
    attn = _attention(qkv, batch, seq)
    mem = _mlstm(slabs, gates_t, b_if, conv_w, conv_b, mlstm_norm_g, batch, seq)
    out = _merge_out(attn, slabs, mem, x2, w_attn_branch.astype(BF16), w_mlstm_branch.astype(BF16),
                     w_out.astype(BF16), final_norm_g.reshape(1, d))
    return out.reshape(batch, seq, d)
```

```python
import jax
import jax.numpy as jnp
from jax import lax
from jax.experimental import pallas as pl
from jax.experimental.pallas import tpu as pltpu

F32 = jnp.float32
BF16 = jnp.bfloat16

D_MODEL = 2048
ATTN_HEADS = 16
ATTN_HEAD_DIM = 128
MLSTM_HEADS = 8
MLSTM_QK_DIM = 128
MLSTM_V_DIM = 256
CONV_WIDTH = 4
NORM_EPS = 1e-6
BAND = 128
ILP = 8
SLAB = 256
GROUP_SLABS = D_MODEL // SLAB
QKV_COLS = 3 * D_MODEL
S_AZ, S_MQK, S_MV, S_MO, S_MZ, S_GA, S_GM = range(7)
N_SLAB_GROUPS = 7
GATE_LANES = 128
MASKED = -1e30
CHUNK = 128
VMEM_LIMIT = 56 * 1024 * 1024


def _cparams(sem):
    return pltpu.CompilerParams(dimension_semantics=sem, vmem_limit_bytes=VMEM_LIMIT)


def _sigmoid(x):
    return 1.0 / (1.0 + jnp.exp(-x))


def _silu(x):
    return x * _sigmoid(x)


def _log_sigmoid(x):
    return jnp.minimum(x, 0.0) - jnp.log(1.0 + jnp.exp(-jnp.abs(x)))


IN_TM = 1024
IN_TN = 1024
QKV_STEPS = QKV_COLS // IN_TN


def _in_proj_kernel(x_ref, g_ref, w_ref, wg_ref, qkv_ref, p_ref, gt_ref, hn_ref):
    n = pl.program_id(1)

    @pl.when(n == 0)
    def _():
        x = x_ref[...]
        ms = jnp.mean(x * x, axis=-1, keepdims=True)
        hn = (x * lax.rsqrt(ms + NORM_EPS) * g_ref[...]).astype(BF16)
        hn_ref[...] = hn
        gates = jnp.dot(hn, wg_ref[...], preferred_element_type=F32)
        gates_t = gates.T
        for c in range(IN_TM // CHUNK):
            gt_ref[c] = gates_t[:2 * MLSTM_HEADS, c * CHUNK:(c + 1) * CHUNK]

    acc = jnp.dot(hn_ref[...], w_ref[...], preferred_element_type=F32)

    @pl.when(n < QKV_STEPS)
    def _():
        scale = jnp.where(n < D_MODEL // IN_TN, ATTN_HEAD_DIM ** -0.5, 1.0).astype(F32)
        for j in range(IN_TN // ATTN_HEAD_DIM):
            qkv_ref[j] = acc[:, j * ATTN_HEAD_DIM:(j + 1) * ATTN_HEAD_DIM] * scale

    @pl.when(n >= QKV_STEPS)
    def _():
        for j in range(IN_TN // SLAB):
            p_ref[j] = acc[:, j * SLAB:(j + 1) * SLAB].astype(BF16)


def _in_proj(x2, norm_g, w_big, w_gate):
    t = x2.shape[0]
    n_cols = w_big.shape[1]
    grid = (t // IN_TM, n_cols // IN_TN)
    return pl.pallas_call(
        _in_proj_kernel,
        grid=grid,
        in_specs=[
            pl.BlockSpec((IN_TM, D_MODEL), lambda i, n: (i, 0)),
            pl.BlockSpec((1, D_MODEL), lambda i, n: (0, 0)),
            pl.BlockSpec((D_MODEL, IN_TN), lambda i, n: (0, n)),
            pl.BlockSpec((D_MODEL, GATE_LANES), lambda i, n: (0, 0)),
        ],
        out_specs=[
            pl.BlockSpec((IN_TN // ATTN_HEAD_DIM, IN_TM, ATTN_HEAD_DIM),
                         lambda i, n: (jnp.minimum(n, QKV_STEPS - 1), i, 0)),
            pl.BlockSpec((IN_TN // SLAB, IN_TM, SLAB),
                         lambda i, n: (jnp.maximum(n - QKV_STEPS, 0), i, 0)),
            pl.BlockSpec((IN_TM // CHUNK, 2 * MLSTM_HEADS, CHUNK), lambda i, n: (i, 0, 0)),
        ],
        out_shape=[
            jax.ShapeDtypeStruct((QKV_COLS // ATTN_HEAD_DIM, t, ATTN_HEAD_DIM), F32),
            jax.ShapeDtypeStruct(((n_cols - QKV_COLS) // SLAB, t, SLAB), BF16),
            jax.ShapeDtypeStruct((t // CHUNK, 2 * MLSTM_HEADS, CHUNK), F32),
        ],
        scratch_shapes=[pltpu.VMEM((IN_TM, D_MODEL), BF16)],
        compiler_params=_cparams(("parallel", "arbitrary")),
        name="in_proj",
    )(x2, norm_g, w_big, w_gate)


def _band_bias(slope, dilation, first):
    qi = lax.broadcasted_iota(jnp.int32, (BAND, 2 * BAND), 0)
    ki = lax.broadcasted_iota(jnp.int32, (BAND, 2 * BAND), 1)
    dist = BAND + qi - ki
    valid = (dist >= 0) & (dist <= BAND)
    if first:
        valid = valid & (ki >= BAND)
    return jnp.where(valid, -slope * (dist * dilation).astype(F32), MASKED)


def _dense_bias(slope, first):
    qi = lax.broadcasted_iota(jnp.int32, (BAND, 2 * BAND), 0)
    ki = lax.broadcasted_iota(jnp.int32, (BAND, 2 * BAND), 1)
    pq, jq = qi // 32, qi % 32
    half, pk, jk = ki // BAND, (ki % BAND) // 32, ki % 32
    dist = 4 * (jq - jk + 32 * (1 - half)) + pq - pk
    valid = (dist >= 0) & (dist <= BAND)
    if first:
        valid = valid & (half == 1)
    return jnp.where(valid, -slope * dist.astype(F32), MASKED)


def _wide_bias(slope):
    qi = lax.broadcasted_iota(jnp.int32, (2 * BAND, 2 * BAND), 0)
    ki = lax.broadcasted_iota(jnp.int32, (2 * BAND, 2 * BAND), 1)
    dist = qi - ki
    valid = (dist >= 0) & (dist <= BAND)
    return jnp.where(valid, -slope * (dist * 16).astype(F32), MASKED)


def _softmax_block(q, k, v, bias):
    s = lax.dot_general(q, k, (((1,), (1,)), ((), ())), preferred_element_type=F32) + bias
    m = jnp.max(s, axis=-1, keepdims=True)
    p = jnp.exp(s - m).astype(BF16)
    v_ext = jnp.concatenate([v, jnp.ones_like(v)], axis=1)
    r = jnp.dot(p, v_ext, preferred_element_type=F32)
    hd = ATTN_HEAD_DIM
    return r[:, :hd], r[:, hd:], jnp.broadcast_to(m, (q.shape[0], hd))


def _attn_kernel(q_ref, k_ref, v_ref, o_ref,
                 x4_ref, x16_ref, tmp_ref, num_ref, den_ref, mx_ref, bias_ref, bias16_ref, onat_ref):
    seq = q_ref.shape[0]
    n_u4, n_u16 = seq // 4, seq // 16
    head = (pl.program_id(1) + 1).astype(F32)
    slope = jnp.exp2(jnp.full((1, 1), -8.0 / ATTN_HEADS, F32) * head)
    bias_ref[0, 0] = _dense_bias(slope, False)
    bias_ref[0, 1] = _dense_bias(slope, True)
    bias_ref[1, 0] = _band_bias(slope, 4, False)
    bias_ref[1, 1] = _band_bias(slope, 4, True)
    bias16_ref[...] = _wide_bias(slope)

    for ti, src in enumerate((q_ref, k_ref, v_ref)):
        for p4 in range(4):
            rows = src[pl.ds(p4, n_u4, stride=4), :]
            tmp_ref[p4] = rows
            x4_ref[ti, p4] = rows.astype(BF16)
        for p16 in range(16):
            x16_ref[ti, p16] = tmp_ref[p16 % 4, pl.ds(p16 // 4, n_u16, stride=4), :].astype(BF16)


    def dense_body(it, carry):
        for u in range(ILP):
            n = it * ILP + u
            cur = pl.multiple_of(n * 32, 32)
            prev = pl.multiple_of(jnp.maximum(n - 1, 0) * 32, 32)
            first = (n == 0).astype(jnp.int32)

            def rows(ti, start):
                return [x4_ref[ti, p, pl.ds(start, 32), :] for p in range(4)]

            q = jnp.concatenate(rows(0, cur), axis=0)
            k = jnp.concatenate(rows(1, prev) + rows(1, cur), axis=0)
            v = jnp.concatenate(rows(2, prev) + rows(2, cur), axis=0)
            num, den, mx = _softmax_block(q, k, v, bias_ref[0, first])
            for p in range(4):
                part = slice(p * 32, (p + 1) * 32)
                num_ref[0, p, pl.ds(cur, 32), :] = num[part]
                den_ref[0, p, pl.ds(cur, 32), :] = den[part]
                mx_ref[0, p, pl.ds(cur, 32), :] = mx[part]
        return carry

    lax.fori_loop(0, n_u4 // (32 * ILP), dense_body, 0)

    blocks4 = n_u4 // BAND

    def band_body(it, carry):
        for u in range(ILP // 4):
            n = it * (ILP // 4) + u
            cur = pl.multiple_of(n * BAND, BAND)
            prev = pl.multiple_of(jnp.maximum(n - 1, 0) * BAND, BAND)
            first = (n == 0).astype(jnp.int32)
            for p in range(4):
                q = x4_ref[0, p, pl.ds(cur, BAND), :]
                k = jnp.concatenate([x4_ref[1, p, pl.ds(prev, BAND), :], x4_ref[1, p, pl.ds(cur, BAND), :]], axis=0)
                v = jnp.concatenate([x4_ref[2, p, pl.ds(prev, BAND), :], x4_ref[2, p, pl.ds(cur, BAND), :]], axis=0)
                num, den, mx = _softmax_block(q, k, v, bias_ref[1, first])
                num_ref[1, p, pl.ds(cur, BAND), :] = num
                den_ref[1, p, pl.ds(cur, BAND), :] = den
                mx_ref[1, p, pl.ds(cur, BAND), :] = mx
        return carry

    lax.fori_loop(0, blocks4 // (ILP // 4), band_body, 0)

    def wide_body(a, carry):
        for p4 in range(4):
            p16 = 4 * a + p4
            num, den, mx = _softmax_block(x16_ref[0, p16], x16_ref[1, p16], x16_ref[2, p16], bias16_ref[...])
            rows = pl.ds(a, n_u16, stride=4)
            num_ref[2, p4, rows, :] = num
            den_ref[2, p4, rows, :] = den
            mx_ref[2, p4, rows, :] = mx
        return carry

    lax.fori_loop(0, 4, wide_body, 0)

    def merge_body(it, carry):
        p, n = it // blocks4, it % blocks4
        rows = pl.ds(pl.multiple_of(n * BAND, BAND), BAND)
        m0, m1, m2 = mx_ref[0, p, rows, :], mx_ref[1, p, rows, :], mx_ref[2, p, rows, :]
        m_all = jnp.maximum(jnp.maximum(m0, m1), m2)
        w0, w1, w2 = jnp.exp(m0 - m_all), jnp.exp(m1 - m_all), jnp.exp(m2 - m_all)
        num = w0 * num_ref[0, p, rows, :] + w1 * num_ref[1, p, rows, :] + w2 * num_ref[2, p, rows, :]
        den = w0 * den_ref[0, p, rows, :] + w1 * den_ref[1, p, rows, :] + w2 * den_ref[2, p, rows, :]
        onat_ref[pl.ds(p + 4 * BAND * n, BAND, stride=4), :] = num / den
        return carry

    lax.fori_loop(0, 4 * blocks4, merge_body, 0)
    o_ref[...] = onat_ref[...].astype(o_ref.dtype)


def _attention(qkv, batch, seq):
    assert seq // 16 == 2 * BAND
    hd = ATTN_HEAD_DIM
    qkv4 = qkv.reshape(3 * ATTN_HEADS, batch, seq, hd)

    def spec(which):
        return pl.BlockSpec((None, None, seq, hd), lambda b, h: (which * ATTN_HEADS + h, b, 0, 0))

    out = pl.pallas_call(
        _attn_kernel,
        grid=(batch, ATTN_HEADS),
        in_specs=[spec(0), spec(1), spec(2)],
        out_specs=pl.BlockSpec((None, None, seq, hd), lambda b, h: (h, b, 0, 0)),
        out_shape=jax.ShapeDtypeStruct((ATTN_HEADS, batch, seq, hd), BF16),
        scratch_shapes=[
            pltpu.VMEM((3, 4, seq // 4, hd), BF16),
            pltpu.VMEM((3, 16, seq // 16, hd), BF16),
            pltpu.VMEM((4, seq // 4, hd), F32),
            pltpu.VMEM((3, 4, seq // 4, hd), F32),
            pltpu.VMEM((3, 4, seq // 4, hd), F32),
            pltpu.VMEM((3, 4, seq // 4, hd), F32),
            pltpu.VMEM((2, 2, BAND, 2 * BAND), F32),
            pltpu.VMEM((2 * BAND, 2 * BAND), F32),
            pltpu.VMEM((seq, hd), F32),
        ],
        compiler_params=_cparams(("parallel", "parallel")),
        name="dilated_attention",
    )(qkv4, qkv4, qkv4)
    return out.reshape(ATTN_HEADS, batch * seq, hd)


def _conv_silu(u_ref, start, first, w, b):
    cur = u_ref[pl.ds(start, CHUNK), :].astype(F32)
    prev_start = pl.multiple_of(jnp.maximum(start - 16, 0), 16)
    prev = u_ref[pl.ds(prev_start, 16), :].astype(F32)
    prev = jnp.where(first, 0.0, prev)
    ext = jnp.concatenate([prev, cur], axis=0)
    y = b + w[CONV_WIDTH - 1:CONV_WIDTH, :] * cur
    for back in range(1, CONV_WIDTH):
        shifted = pltpu.roll(ext, back, axis=0)[16:, :]
        y = y + w[CONV_WIDTH - 1 - back:CONV_WIDTH - back, :] * shifted
    return _silu(y)


def _mlstm_kernel(bias_ref, uq_ref, uk_ref, v_ref, o_ref, z_ref, gt_ref,
                  cwq_ref, cbq_ref, cwk_ref, cbk_ref, ng_ref, out_ref,
                  c_ref, n_ref, m_ref):
    j = pl.program_id(1)
    seq = uq_ref.shape[0]
    dk, dv = MLSTM_QK_DIM, MLSTM_V_DIM
    c_ref[...] = jnp.zeros_like(c_ref)
    n_ref[...] = jnp.zeros_like(n_ref)
    m_ref[...] = jnp.zeros_like(m_ref)

    ti = lax.broadcasted_iota(jnp.int32, (CHUNK, CHUNK), 0)
    si = lax.broadcasted_iota(jnp.int32, (CHUNK, CHUNK), 1)
    causal = ti >= si
    csum = (ti <= si).astype(F32)
    head_rows = lax.broadcasted_iota(jnp.int32, (MLSTM_HEADS, CHUNK), 0)

    def chunk_body(c, carry):
        start = pl.multiple_of(c * CHUNK, CHUNK)
        first = c == 0
        qq = _conv_silu(uq_ref, start, first, cwq_ref[...], cbq_ref[...])
        kk = _conv_silu(uk_ref, start, first, cwk_ref[...], cbk_ref[...]) * dk ** -0.5
        gates = gt_ref[c] + bias_ref[...]
        ig_all = gates[:MLSTM_HEADS]
        lf_all = _log_sigmoid(gates[MLSTM_HEADS:])
        b_all = jnp.dot(lf_all, csum, preferred_element_type=F32,
                        precision=lax.Precision.HIGHEST)
        c_all = ig_all - b_all
        for hh in range(2):
            pick = head_rows == 2 * j + hh
            q = qq[:, hh * dk:(hh + 1) * dk]
            k = kk[:, hh * dk:(hh + 1) * dk]
            qb, kb = q.astype(BF16), k.astype(BF16)
            vb = v_ref[hh, pl.ds(start, CHUNK), :]
            b_row = jnp.sum(jnp.where(pick, b_all, 0.0), axis=0, keepdims=True)
            c_row = jnp.sum(jnp.where(pick, c_all, 0.0), axis=0, keepdims=True)
            b_last = b_row[:, CHUNK - 1:CHUNK]
            cols = jnp.concatenate([b_row, c_row, jnp.zeros((6, CHUNK), F32)], axis=0).T
            b_col, c_col = cols[:, 0:1], cols[:, 1:2]

            m_prev = m_ref[hh]
            a_col = b_col + m_prev
            dmat = jnp.where(causal, b_col + c_row, MASKED)
            m_t = jnp.maximum(a_col, jnp.max(dmat, axis=-1, keepdims=True))
            w_intra = jnp.exp(dmat - m_t)
            w_inter = jnp.exp(a_col - m_t)
            sc = lax.dot_general(qb, kb, (((1,), (1,)), ((), ())),
                                 preferred_element_type=F32) * w_intra
            c_st = c_ref[hh]
            n_st = n_ref[hh]
            q_c = jnp.dot(qb, c_st.astype(BF16), preferred_element_type=F32)

            m_new = jnp.maximum(b_last + m_prev, b_last + jnp.max(c_row, axis=-1, keepdims=True))
            w_old = jnp.exp(b_last + m_prev - m_new)
            w_s = jnp.exp(b_last + c_col - m_new)
            kw = k * w_s
            lhs = jnp.concatenate([sc.astype(BF16), kw.T.astype(BF16)], axis=0)
            both = jnp.dot(lhs, vb, preferred_element_type=F32)

            num = w_inter * q_c + both[:CHUNK]
            den = (w_inter * jnp.sum(q * n_st, axis=-1, keepdims=True)
                   + jnp.sum(sc, axis=-1, keepdims=True))
            h_out = num / jnp.maximum(jnp.abs(den), jnp.exp(-m_t))

            c_ref[hh] = w_old * c_st + both[CHUNK:]
            n_ref[hh] = w_old * n_st + jnp.sum(kw, axis=0, keepdims=True)
            m_ref[hh] = m_new

            og = o_ref[hh, pl.ds(start, CHUNK), :].astype(F32)
            zg = z_ref[hh, pl.ds(start, CHUNK), :].astype(F32)
            cell = _sigmoid(og) * h_out
            cell = cell * lax.rsqrt(jnp.mean(cell * cell, axis=-1, keepdims=True) + NORM_EPS)
            cell = cell * ng_ref[:, hh * dv:(hh + 1) * dv]
            out_ref[hh, pl.ds(start, CHUNK), :] = (cell * _silu(zg)).astype(out_ref.dtype)
        return carry

    lax.fori_loop(0, seq // CHUNK, chunk_body, 0)


def _mlstm(slabs, gates_t, b_if, conv_w, conv_b, norm_g, batch, seq):
    n_slabs = slabs.shape[0]
    s4d = slabs.reshape(n_slabs, batch, seq, SLAB)
    half = MLSTM_HEADS // 2

    def pair_spec(group):
        return pl.BlockSpec((2, None, seq, SLAB),
                            lambda b, j: (group * GROUP_SLABS // 2 + j, b, 0, 0))

    out = pl.pallas_call(
        _mlstm_kernel,
        grid=(batch, half),
        in_specs=[
            pl.BlockSpec((2 * MLSTM_HEADS, 1), lambda b, j: (0, 0)),
            pl.BlockSpec((None, None, seq, SLAB), lambda b, j: (S_MQK * GROUP_SLABS + j, b, 0, 0)),
            pl.BlockSpec((None, None, seq, SLAB), lambda b, j: (S_MQK * GROUP_SLABS + half + j, b, 0, 0)),
            pair_spec(S_MV), pair_spec(S_MO), pair_spec(S_MZ),
            pl.BlockSpec((seq // CHUNK, 2 * MLSTM_HEADS, CHUNK), lambda b, j: (b, 0, 0)),
            pl.BlockSpec((CONV_WIDTH, SLAB), lambda b, j: (0, j)),
            pl.BlockSpec((1, SLAB), lambda b, j: (0, j)),
            pl.BlockSpec((CONV_WIDTH, SLAB), lambda b, j: (0, half + j)),
            pl.BlockSpec((1, SLAB), lambda b, j: (0, half + j)),
            pl.BlockSpec((1, 2 * MLSTM_V_DIM), lambda b, j: (0, j)),
        ],
        out_specs=pl.BlockSpec((2, None, seq, SLAB), lambda b, j: (j, b, 0, 0)),
        out_shape=jax.ShapeDtypeStruct((MLSTM_HEADS, batch, seq, SLAB), BF16),
        scratch_shapes=[
            pltpu.VMEM((2, MLSTM_QK_DIM, MLSTM_V_DIM), F32),
            pltpu.VMEM((2, 1, MLSTM_QK_DIM), F32),
            pltpu.VMEM((2, 1, 1), F32),
        ],
        compiler_params=_cparams(("parallel", "parallel")),
        name="mlstm",
    )(b_if.reshape(-1, 1), s4d, s4d, s4d, s4d, s4d, gates_t, conv_w, conv_b.reshape(1, -1), conv_w,
      conv_b.reshape(1, -1), norm_g.reshape(1, -1))
    return out.reshape(MLSTM_HEADS, batch * seq, SLAB)


OUT_TM = 256


def _merge_out_kernel(attn_ref, az_ref, mem_ref, ga_ref, gm_ref, x_ref, wa_ref, wm_ref, wo_ref,
                      fg_ref, out_ref):
    n = GROUP_SLABS
    attn = jnp.concatenate([attn_ref[h] for h in range(ATTN_HEADS)], axis=1).astype(F32)
    az = jnp.concatenate([az_ref[c] for c in range(n)], axis=1).astype(F32)
    gated = (attn * _silu(az)).astype(BF16)
    mem = jnp.concatenate([mem_ref[c] for c in range(n)], axis=1)
    y_a = jnp.dot(gated, wa_ref[...], preferred_element_type=F32)
    y_m = jnp.dot(mem, wm_ref[...], preferred_element_type=F32)
    g_a = jnp.concatenate([ga_ref[c] for c in range(n)], axis=1).astype(F32)
    g_m = jnp.concatenate([gm_ref[c] for c in range(n)], axis=1).astype(F32)
    merged = (_sigmoid(g_a) * y_a + _sigmoid(g_m) * y_m).astype(BF16)
    y = x_ref[...] + jnp.dot(merged, wo_ref[...], preferred_element_type=F32)
    y = y * lax.rsqrt(jnp.mean(y * y, axis=-1, keepdims=True) + NORM_EPS)
    out_ref[...] = y * fg_ref[...]


def _merge_out(attn, slabs, mem, x2, w_a, w_m, w_o, final_g):
    t = x2.shape[0]

    def slab_spec(group):
        return pl.BlockSpec((GROUP_SLABS, OUT_TM, SLAB), lambda i: (group, i, 0))

    def weight_spec():
        return pl.BlockSpec((D_MODEL, D_MODEL), lambda i: (0, 0), pipeline_mode=pl.Buffered(1))

    return pl.pallas_call(
        _merge_out_kernel,
        grid=(t // OUT_TM,),
        in_specs=[
            pl.BlockSpec((ATTN_HEADS, OUT_TM, ATTN_HEAD_DIM), lambda i: (0, i, 0)),
            slab_spec(S_AZ), slab_spec(0), slab_spec(S_GA), slab_spec(S_GM),
            pl.BlockSpec((OUT_TM, D_MODEL), lambda i: (i, 0)),
            weight_spec(), weight_spec(), weight_spec(),
            pl.BlockSpec((1, D_MODEL), lambda i: (0, 0)),
        ],
        out_specs=pl.BlockSpec((OUT_TM, D_MODEL), lambda i: (i, 0)),
        out_shape=jax.ShapeDtypeStruct((t, D_MODEL), F32),
        compiler_params=_cparams(("parallel",)),
        name="merge_out",
    )(attn, slabs, mem, slabs, slabs, x2, w_a, w_m, w_o, final_g)


def kernel(x, norm_g, w_in, b_if, conv_w, conv_b, mlstm_norm_g, w_attn_branch, w_mlstm_branch,
           w_out, final_norm_g):
    batch, seq, d = x.shape
    assert d == D_MODEL and seq % (16 * BAND) == 0 and (batch * seq) % IN_TM == 0
    t = batch * seq
    x2 = x.reshape(t, d)

    gate_lo = 4 * D_MODEL + 2 * MLSTM_HEADS * MLSTM_QK_DIM + MLSTM_HEADS * MLSTM_V_DIM
    gate_hi = gate_lo + 2 * MLSTM_HEADS
    w_big = jnp.concatenate([w_in[:, :gate_lo], w_in[:, gate_hi:]], axis=1).astype(BF16)
    w_gate = jnp.pad(w_in[:, gate_lo:gate_hi], ((0, 0), (0, GATE_LANES - 2 * MLSTM_HEADS))).astype(BF16)
    assert w_big.shape[1] == QKV_COLS + N_SLAB_GROUPS * D_MODEL

    qkv, slabs, gates_t = _in_proj(x2, norm_g.reshape(1, d), w_big, w_gate)
    attn = _attention(qkv, batch, seq)
    mem = _mlstm(slabs, gates_t, b_if, conv_w, conv_b, mlstm_norm_g, batch, seq)
    out = _merge_out(attn, slabs, mem, x2, w_attn_branch.astype(BF16), w_mlstm_branch.astype(BF16),
                     w_out.astype(BF16), final_norm_g.reshape(1, d))
    return out.reshape(batch, seq, d)
```

```python
import functools

import jax
import jax.numpy as jnp
from jax import lax
from jax.experimental import pallas as pl
from jax.experimental.pallas import tpu as pltpu

F32 = jnp.float32
BF16 = jnp.bfloat16

D_MODEL = 2048
ATTN_HEADS = 16
ATTN_HEAD_DIM = 128
MLSTM_HEADS = 8
MLSTM_QK_DIM = 128
MLSTM_V_DIM = 256
CONV_WIDTH = 4
NORM_EPS = 1e-6
BAND = 128
ILP = 8
SLAB = 256
GROUP_SLABS = D_MODEL // SLAB
QKV_COLS = 3 * D_MODEL
S_AZ, S_MQK, S_MV, S_MO, S_MZ, S_GA, S_GM = range(7)
N_SLAB_GROUPS = 7
GATE_LANES = 128
MASKED = -1e30
CHUNK = 128
VMEM_LIMIT = 56 * 1024 * 1024


def _cparams(sem):
    return pltpu.CompilerParams(dimension_semantics=sem, vmem_limit_bytes=VMEM_LIMIT)


def _sigmoid(x):
    return 1.0 / (1.0 + jnp.exp(-x))


def _silu(x):
    return x * _sigmoid(x)


def _log_sigmoid(x):
    return jnp.minimum(x, 0.0) - jnp.log(1.0 + jnp.exp(-jnp.abs(x)))


IN_TM = 1024
IN_TN = 1024
QKV_STEPS = QKV_COLS // IN_TN


CONV_STEP0 = QKV_STEPS + S_MQK * (D_MODEL // IN_TN)
CONV_STEPS = D_MODEL // IN_TN
CONV_HALO = 8


def _in_proj_kernel(x_ref, g_ref, w_ref, wg_ref, cw_ref, cb_ref, qkv_ref, p_ref, gt_ref,
                    hn_ref, halo_ref, *, tiles_per_seq):
    i = pl.program_id(0)
    n = pl.program_id(1)

    @pl.when((i == 0) & (n == 0))
    def _():
        halo_ref[...] = jnp.zeros_like(halo_ref)

    @pl.when(n == 0)
    def _():
        x = x_ref[...]
        ms = jnp.mean(x * x, axis=-1, keepdims=True)
        hn = (x * lax.rsqrt(ms + NORM_EPS) * g_ref[...]).astype(BF16)
        hn_ref[...] = hn
        gates = jnp.dot(hn, wg_ref[...], preferred_element_type=F32)
        gates_t = gates.T
        for c in range(IN_TM // CHUNK):
            gt_ref[c] = gates_t[:2 * MLSTM_HEADS, c * CHUNK:(c + 1) * CHUNK]

    def slab_acc(j):
        return jnp.dot(hn_ref[...], w_ref[:, j * SLAB:(j + 1) * SLAB], preferred_element_type=F32)

    @pl.when(n < QKV_STEPS)
    def _():
        scale = jnp.where(n < D_MODEL // IN_TN, ATTN_HEAD_DIM ** -0.5, 1.0).astype(F32)
        for j in range(IN_TN // SLAB):
            acc = slab_acc(j) * scale
            qkv_ref[2 * j] = acc[:, :ATTN_HEAD_DIM]
            qkv_ref[2 * j + 1] = acc[:, ATTN_HEAD_DIM:]

    is_conv = (n >= CONV_STEP0) & (n < CONV_STEP0 + CONV_STEPS)

    @pl.when((n >= QKV_STEPS) & jnp.logical_not(is_conv))
    def _():
        for j in range(IN_TN // SLAB):
            p_ref[j] = slab_acc(j).astype(BF16)

    @pl.when(is_conv)
    def _():
        step = n - CONV_STEP0
        seq_start = (i % tiles_per_seq) == 0
        post = jnp.where(step == 1, MLSTM_QK_DIM ** -0.5, 1.0).astype(F32)
        for j in range(IN_TN // SLAB):
            cols = slice(j * SLAB, (j + 1) * SLAB)
            acc = slab_acc(j)
            prev = jnp.where(seq_start, 0.0, halo_ref[step, :, cols])
            halo_ref[step, :, cols] = acc[IN_TM - CONV_HALO:, :]
            ext = jnp.concatenate([prev, acc], axis=0)
            y = cb_ref[:, cols] + cw_ref[CONV_WIDTH - 1:CONV_WIDTH, cols] * acc
            for back in range(1, CONV_WIDTH):
                shifted = pltpu.roll(ext, back, axis=0)[CONV_HALO:, :]
                y = y + cw_ref[CONV_WIDTH - 1 - back:CONV_WIDTH - back, cols] * shifted
            p_ref[j] = (_silu(y) * post).astype(BF16)


def _in_proj(x2, norm_g, w_big, w_gate, conv_w, conv_b, seq):
    t = x2.shape[0]
    n_cols = w_big.shape[1]
    grid = (t // IN_TM, n_cols // IN_TN)
    assert seq % IN_TM == 0 and CONV_STEPS * IN_TN == conv_w.shape[1]

    def conv_step(i, n):
        return (0, jnp.clip(n - CONV_STEP0, 0, CONV_STEPS - 1))

    return pl.pallas_call(
        functools.partial(_in_proj_kernel, tiles_per_seq=seq // IN_TM),
        grid=grid,
        in_specs=[
            pl.BlockSpec((IN_TM, D_MODEL), lambda i, n: (i, 0)),
            pl.BlockSpec((1, D_MODEL), lambda i, n: (0, 0)),
            pl.BlockSpec((D_MODEL, IN_TN), lambda i, n: (0, n)),
            pl.BlockSpec((D_MODEL, GATE_LANES), lambda i, n: (0, 0)),
            pl.BlockSpec((CONV_WIDTH, IN_TN), conv_step),
            pl.BlockSpec((1, IN_TN), conv_step),
        ],
        out_specs=[
            pl.BlockSpec((IN_TN // ATTN_HEAD_DIM, IN_TM, ATTN_HEAD_DIM),
                         lambda i, n: (jnp.minimum(n, QKV_STEPS - 1), i, 0)),
            pl.BlockSpec((IN_TN // SLAB, IN_TM, SLAB),
                         lambda i, n: (jnp.maximum(n - QKV_STEPS, 0), i, 0)),
            pl.BlockSpec((IN_TM // CHUNK, 2 * MLSTM_HEADS, CHUNK), lambda i, n: (i, 0, 0)),
        ],
        out_shape=[
            jax.ShapeDtypeStruct((QKV_COLS // ATTN_HEAD_DIM, t, ATTN_HEAD_DIM), F32),
            jax.ShapeDtypeStruct(((n_cols - QKV_COLS) // SLAB, t, SLAB), BF16),
            jax.ShapeDtypeStruct((t // CHUNK, 2 * MLSTM_HEADS, CHUNK), F32),
        ],
        scratch_shapes=[pltpu.VMEM((IN_TM, D_MODEL), BF16),
                        pltpu.VMEM((CONV_STEPS, CONV_HALO, IN_TN), F32)],
        compiler_params=_cparams(("arbitrary", "arbitrary")),
        name="in_proj",
    )(x2, norm_g, w_big, w_gate, conv_w, conv_b)


def _band_bias(slope, dilation, first):
    qi = lax.broadcasted_iota(jnp.int32, (BAND, 2 * BAND), 0)
    ki = lax.broadcasted_iota(jnp.int32, (BAND, 2 * BAND), 1)
    dist = BAND + qi - ki
    valid = (dist >= 0) & (dist <= BAND)
    if first:
        valid = valid & (ki >= BAND)
    return jnp.where(valid, -slope * (dist * dilation).astype(F32), MASKED)


def _dense_bias(slope, first):
    qi = lax.broadcasted_iota(jnp.int32, (BAND, 2 * BAND), 0)
    ki = lax.broadcasted_iota(jnp.int32, (BAND, 2 * BAND), 1)
    pq, jq = qi // 32, qi % 32
    half, pk, jk = ki // BAND, (ki % BAND) // 32, ki % 32
    dist = 4 * (jq - jk + 32 * (1 - half)) + pq - pk
    valid = (dist >= 0) & (dist <= BAND)
    if first:
        valid = valid & (half == 1)
    return jnp.where(valid, -slope * dist.astype(F32), MASKED)


def _wide_bias(slope):
    qi = lax.broadcasted_iota(jnp.int32, (2 * BAND, 2 * BAND), 0)
    ki = lax.broadcasted_iota(jnp.int32, (2 * BAND, 2 * BAND), 1)
    dist = qi - ki
    valid = (dist >= 0) & (dist <= BAND)
    return jnp.where(valid, -slope * (dist * 16).astype(F32), MASKED)


def _softmax_block(q, k, v, bias):
    s = lax.dot_general(q, k, (((1,), (1,)), ((), ())), preferred_element_type=F32) + bias
    m = jnp.max(s, axis=-1, keepdims=True)
    p = jnp.exp(s - m).astype(BF16)
    v_ext = jnp.concatenate([v, jnp.ones_like(v)], axis=1)
    r = jnp.dot(p, v_ext, preferred_element_type=F32)
    hd = ATTN_HEAD_DIM
    return r[:, :hd], r[:, hd:], jnp.broadcast_to(m, (q.shape[0], hd))


def _attn_kernel(q_ref, k_ref, v_ref, o_ref,
                 x4_ref, x16_ref, tmp_ref, num_ref, den_ref, mx_ref, bias_ref, bias16_ref, onat_ref):
    seq = q_ref.shape[0]
    n_u4, n_u16 = seq // 4, seq // 16
    slope = jnp.exp2(jnp.full((1, 1), -8.0 / ATTN_HEADS, F32) * (pl.program_id(1) + 1))
    bias_ref[0, 0] = _dense_bias(slope, False)
    bias_ref[0, 1] = _dense_bias(slope, True)
    bias_ref[1, 0] = _band_bias(slope, 4, False)
    bias_ref[1, 1] = _band_bias(slope, 4, True)
    bias16_ref[...] = _wide_bias(slope)

    for ti, src in enumerate((q_ref, k_ref, v_ref)):
        for p4 in range(4):
            rows = src[pl.ds(p4, n_u4, stride=4), :]
            tmp_ref[p4] = rows
            x4_ref[ti, p4] = rows.astype(BF16)
        for p16 in range(16):
            x16_ref[ti, p16] = tmp_ref[p16 % 4, pl.ds(p16 // 4, n_u16, stride=4), :].astype(BF16)


    def dense_body(it, carry):
        for u in range(ILP):
            n = it * ILP + u
            cur = pl.multiple_of(n * 32, 32)
            prev = pl.multiple_of(jnp.maximum(n - 1, 0) * 32, 32)
            first = jnp.where(n == 0, 1, 0)

            def rows(ti, start):
                return [x4_ref[ti, p, pl.ds(start, 32), :] for p in range(4)]

            q = jnp.concatenate(rows(0, cur), axis=0)
            k = jnp.concatenate(rows(1, prev) + rows(1, cur), axis=0)
            v = jnp.concatenate(rows(2, prev) + rows(2, cur), axis=0)
            num, den, mx = _softmax_block(q, k, v, bias_ref[0, first])
            for p in range(4):
                part = slice(p * 32, (p + 1) * 32)
                num_ref[0, p, pl.ds(cur, 32), :] = num[part]
                den_ref[0, p, pl.ds(cur, 32), :] = den[part]
                mx_ref[0, p, pl.ds(cur, 32), :] = mx[part]
        return carry

    lax.fori_loop(0, n_u4 // (32 * ILP), dense_body, 0)

    blocks4 = n_u4 // BAND

    def band_body(it, carry):
        for u in range(ILP // 4):
            n = it * (ILP // 4) + u
            cur = pl.multiple_of(n * BAND, BAND)
            prev = pl.multiple_of(jnp.maximum(n - 1, 0) * BAND, BAND)
            first = jnp.where(n == 0, 1, 0)
            for p in range(4):
                q = x4_ref[0, p, pl.ds(cur, BAND), :]
                k = jnp.concatenate([x4_ref[1, p, pl.ds(prev, BAND), :], x4_ref[1, p, pl.ds(cur, BAND), :]], axis=0)
                v = jnp.concatenate([x4_ref[2, p, pl.ds(prev, BAND), :], x4_ref[2, p, pl.ds(cur, BAND), :]], axis=0)
                num, den, mx = _softmax_block(q, k, v, bias_ref[1, first])
                num_ref[1, p, pl.ds(cur, BAND), :] = num
                den_ref[1, p, pl.ds(cur, BAND), :] = den
                mx_ref[1, p, pl.ds(cur, BAND), :] = mx
        return carry

    lax.fori_loop(0, blocks4 // (ILP // 4), band_body, 0)

    def wide_body(a, carry):
        for p4 in range(4):
            p16 = 4 * a + p4
            num, den, mx = _softmax_block(x16_ref[0, p16], x16_ref[1, p16], x16_ref[2, p16], bias16_ref[...])
            rows = pl.ds(a, n_u16, stride=4)
            num_ref[2, p4, rows, :] = num
            den_ref[2, p4, rows, :] = den
            mx_ref[2, p4, rows, :] = mx
        return carry

    lax.fori_loop(0, 4, wide_body, 0)

    def merge_body(it, carry):
        p, n = it // blocks4, it % blocks4
        rows = pl.ds(pl.multiple_of(n * BAND, BAND), BAND)
        m0, m1, m2 = mx_ref[0, p, rows, :], mx_ref[1, p, rows, :], mx_ref[2, p, rows, :]
        m_all = jnp.maximum(jnp.maximum(m0, m1), m2)
        w0, w1, w2 = jnp.exp(m0 - m_all), jnp.exp(m1 - m_all), jnp.exp(m2 - m_all)
        num = w0 * num_ref[0, p, rows, :] + w1 * num_ref[1, p, rows, :] + w2 * num_ref[2, p, rows, :]
        den = w0 * den_ref[0, p, rows, :] + w1 * den_ref[1, p, rows, :] + w2 * den_ref[2, p, rows, :]
        onat_ref[pl.ds(p + 4 * BAND * n, BAND, stride=4), :] = num / den
        return carry

    lax.fori_loop(0, 4 * blocks4, merge_body, 0)
    o_ref[...] = onat_ref[...].astype(o_ref.dtype)


def _attention(qkv, batch, seq):
    assert seq // 16 == 2 * BAND
    hd = ATTN_HEAD_DIM
    qkv4 = qkv.reshape(3 * ATTN_HEADS, batch, seq, hd)

    def spec(which):
        return pl.BlockSpec((None, None, seq, hd), lambda b, h: (which * ATTN_HEADS + h, b, 0, 0))

    out = pl.pallas_call(
        _attn_kernel,
        grid=(batch, ATTN_HEADS),
        in_specs=[spec(0), spec(1), spec(2)],
        out_specs=pl.BlockSpec((None, None, seq, hd), lambda b, h: (h, b, 0, 0)),
        out_shape=jax.ShapeDtypeStruct((ATTN_HEADS, batch, seq, hd), BF16),
        scratch_shapes=[
            pltpu.VMEM((3, 4, seq // 4, hd), BF16),
            pltpu.VMEM((3, 16, seq // 16, hd), BF16),
            pltpu.VMEM((4, seq // 4, hd), F32),
            pltpu.VMEM((3, 4, seq // 4, hd), F32),
            pltpu.VMEM((3, 4, seq // 4, hd), F32),
            pltpu.VMEM((3, 4, seq // 4, hd), F32),
            pltpu.VMEM((2, 2, BAND, 2 * BAND), F32),
            pltpu.VMEM((2 * BAND, 2 * BAND), F32),
            pltpu.VMEM((seq, hd), F32),
        ],
        compiler_params=_cparams(("parallel", "parallel")),
        name="dilated_attention",
    )(qkv4, qkv4, qkv4)
    return out.reshape(ATTN_HEADS, batch * seq, hd)


MLSTM_TS = 512
ONES_LANES = 128


def _mlstm_kernel(bias_ref, q_ref, k_ref, v_ref, gt_ref, out_ref, c_ref, m_ref):
    s = pl.program_id(1)
    dk, dv = MLSTM_QK_DIM, MLSTM_V_DIM
    L = CHUNK

    @pl.when(s == 0)
    def _():
        c_ref[...] = jnp.zeros_like(c_ref)
        m_ref[...] = jnp.zeros_like(m_ref)

    ti = lax.broadcasted_iota(jnp.int32, (CHUNK, CHUNK), 0)
    si = lax.broadcasted_iota(jnp.int32, (CHUNK, CHUNK), 1)
    causal = ti >= si
    csum = jnp.where(ti <= si, 1.0, 0.0).astype(F32)

    def chunk_body(c, carry):
        start = pl.multiple_of(c * CHUNK, CHUNK)
        gates = gt_ref[c] + bias_ref[...]
        ig_all = gates[:MLSTM_HEADS]
        lf_all = _log_sigmoid(gates[MLSTM_HEADS:])
        b_all = jnp.dot(lf_all, csum, preferred_element_type=F32,
                        precision=lax.Precision.HIGHEST)
        c_all = ig_all - b_all
        cols_all = jnp.concatenate([b_all, c_all], axis=0).T
        for h in range(MLSTM_HEADS):
            sl, lane0 = h // 2, (h % 2) * dk
            qb = q_ref[sl, pl.ds(start, L), lane0:lane0 + dk]
            kb = k_ref[sl, pl.ds(start, L), lane0:lane0 + dk]
            vb = v_ref[h, pl.ds(start, L), :]
            v_ext = jnp.concatenate([vb, jnp.ones((L, ONES_LANES), BF16)], axis=1)
            b_row, c_row = b_all[h:h + 1], c_all[h:h + 1]
            b_last = b_row[:, L - 1:L]
            b_b = jnp.broadcast_to(cols_all[:, h:h + 1], (L, L))
            c_b = jnp.broadcast_to(cols_all[:, MLSTM_HEADS + h:MLSTM_HEADS + h + 1], (L, L))

            m_prev = m_ref[h]
            a_b = b_b + m_prev
            dmat = jnp.where(causal, b_b + c_row, MASKED)
            m_t = jnp.maximum(a_b, jnp.broadcast_to(jnp.max(dmat, axis=-1, keepdims=True), (L, L)))
            w_intra = jnp.exp(dmat - m_t)
            w_inter = jnp.exp(a_b - m_t)
            sc = lax.dot_general(qb, kb, (((1,), (1,)), ((), ())),
                                 preferred_element_type=F32) * w_intra
            c_st = c_ref[h]
            q_c = jnp.dot(qb, c_st.astype(BF16), preferred_element_type=F32)

            m_new = b_last + jnp.maximum(m_prev, jnp.max(c_row, axis=-1, keepdims=True))
            w_old = jnp.exp(b_last + m_prev - m_new)
            kw = kb.astype(F32) * jnp.exp(b_last + c_b - m_new)
            lhs = jnp.concatenate([sc.astype(BF16), kw.T.astype(BF16)], axis=0)
            both = jnp.dot(lhs, v_ext, preferred_element_type=F32)

            den = w_inter * q_c[:, dv:] + both[:L, dv:]
            inv = 1.0 / jnp.maximum(jnp.abs(den), jnp.exp(-m_t))
            for half in range(dv // L):
                lanes = slice(half * L, (half + 1) * L)
                h_half = (w_inter * q_c[:, lanes] + both[:L, lanes]) * inv
                out_ref[h, pl.ds(start, L), lanes] = h_half.astype(out_ref.dtype)

            c_ref[h] = w_old * c_st + both[L:]
            m_ref[h] = m_new
        return carry

    lax.fori_loop(0, MLSTM_TS // CHUNK, chunk_body, 0)


def _mlstm(slabs, gates_t, b_if, batch, seq):
    n_slabs = slabs.shape[0]
    s4d = slabs.reshape(n_slabs, batch, seq, SLAB)
    half = MLSTM_HEADS // 2
    tiles = seq // MLSTM_TS

    def tile_spec(first_slab, n):
        return pl.BlockSpec((n, None, MLSTM_TS, SLAB), lambda b, s: (first_slab // n, b, s, 0))

    q0 = S_MQK * GROUP_SLABS
    out = pl.pallas_call(
        _mlstm_kernel,
        grid=(batch, tiles),
        in_specs=[
            pl.BlockSpec((2 * MLSTM_HEADS, 1), lambda b, s: (0, 0)),
            tile_spec(q0, half), tile_spec(q0 + half, half),
            tile_spec(S_MV * GROUP_SLABS, MLSTM_HEADS),
            pl.BlockSpec((MLSTM_TS // CHUNK, 2 * MLSTM_HEADS, CHUNK), lambda b, s: (b * tiles + s, 0, 0)),
        ],
        out_specs=pl.BlockSpec((MLSTM_HEADS, None, MLSTM_TS, SLAB), lambda b, s: (0, b, s, 0)),
        out_shape=jax.ShapeDtypeStruct((MLSTM_HEADS, batch, seq, SLAB), BF16),
        scratch_shapes=[
            pltpu.VMEM((MLSTM_HEADS, MLSTM_QK_DIM, MLSTM_V_DIM + ONES_LANES), F32),
            pltpu.VMEM((MLSTM_HEADS, 1, 1), F32),
        ],
        compiler_params=_cparams(("parallel", "arbitrary")),
        name="mlstm",
    )(b_if.reshape(-1, 1), s4d, s4d, s4d, gates_t)
    return out.reshape(MLSTM_HEADS, batch * seq, SLAB)


OUT_TM = 256


def _merge_out_kernel(attn_ref, az_ref, cell_ref, mo_ref, mz_ref, ga_ref, gm_ref, x_ref,
                      wa_ref, wm_ref, wo_ref, ng_ref, fg_ref, out_ref):
    n = GROUP_SLABS
    attn = jnp.concatenate([attn_ref[h] for h in range(ATTN_HEADS)], axis=1).astype(F32)
    az = jnp.concatenate([az_ref[c] for c in range(n)], axis=1).astype(F32)
    gated = (attn * _silu(az)).astype(BF16)
    mem = []
    for h in range(MLSTM_HEADS):
        cell = _sigmoid(mo_ref[h].astype(F32)) * cell_ref[h].astype(F32)
        cell = cell * lax.rsqrt(jnp.mean(cell * cell, axis=-1, keepdims=True) + NORM_EPS)
        cell = cell * ng_ref[:, h * SLAB:(h + 1) * SLAB]
        mem.append((cell * _silu(mz_ref[h].astype(F32))).astype(BF16))
    mem = jnp.concatenate(mem, axis=1)
    y_a = jnp.dot(gated, wa_ref[...], preferred_element_type=F32)
    y_m = jnp.dot(mem, wm_ref[...], preferred_element_type=F32)
    g_a = jnp.concatenate([ga_ref[c] for c in range(n)], axis=1).astype(F32)
    g_m = jnp.concatenate([gm_ref[c] for c in range(n)], axis=1).astype(F32)
    merged = (_sigmoid(g_a) * y_a + _sigmoid(g_m) * y_m).astype(BF16)
    y = x_ref[...] + jnp.dot(merged, wo_ref[...], preferred_element_type=F32)
    y = y * lax.rsqrt(jnp.mean(y * y, axis=-1, keepdims=True) + NORM_EPS)
    out_ref[...] = y * fg_ref[...]


def _merge_out(attn, slabs, cell, x2, w_a, w_m, w_o, norm_g, final_g):
    t = x2.shape[0]
    assert MLSTM_V_DIM == SLAB and MLSTM_HEADS == GROUP_SLABS

    def slab_spec(group):
        return pl.BlockSpec((GROUP_SLABS, OUT_TM, SLAB), lambda i: (group, i, 0))

    def weight_spec():
        return pl.BlockSpec((D_MODEL, D_MODEL), lambda i: (0, 0), pipeline_mode=pl.Buffered(1))

    return pl.pallas_call(
        _merge_out_kernel,
        grid=(t // OUT_TM,),
        in_specs=[
            pl.BlockSpec((ATTN_HEADS, OUT_TM, ATTN_HEAD_DIM), lambda i: (0, i, 0)),
            slab_spec(S_AZ), slab_spec(0), slab_spec(S_MO), slab_spec(S_MZ),
            slab_spec(S_GA), slab_spec(S_GM),
            pl.BlockSpec((OUT_TM, D_MODEL), lambda i: (i, 0)),
            weight_spec(), weight_spec(), weight_spec(),
            pl.BlockSpec((1, D_MODEL), lambda i: (0, 0)),
            pl.BlockSpec((1, D_MODEL), lambda i: (0, 0)),
        ],
        out_specs=pl.BlockSpec((OUT_TM, D_MODEL), lambda i: (i, 0)),
        out_shape=jax.ShapeDtypeStruct((t, D_MODEL), F32),
        compiler_params=_cparams(("parallel",)),
        name="merge_out",
    )(attn, slabs, cell, slabs, slabs, slabs, slabs, x2, w_a, w_m, w_o, norm_g, final_g)


def kernel(x, norm_g, w_in, b_if, conv_w, conv_b, mlstm_norm_g, w_attn_branch, w_mlstm_branch,
           w_out, final_norm_g):
    batch, seq, d = x.shape
    assert d == D_MODEL and seq % (16 * BAND) == 0 and (batch * seq) % IN_TM == 0
    t = batch * seq
    x2 = x.reshape(t, d)

    gate_lo = 4 * D_MODEL + 2 * MLSTM_HEADS * MLSTM_QK_DIM + MLSTM_HEADS * MLSTM_V_DIM
    gate_hi = gate_lo + 2 * MLSTM_HEADS
    w_big = jnp.concatenate([w_in[:, :gate_lo], w_in[:, gate_hi:]], axis=1).astype(BF16)
    w_gate = jnp.pad(w_in[:, gate_lo:gate_hi], ((0, 0), (0, GATE_LANES - 2 * MLSTM_HEADS))).astype(BF16)
    assert w_big.shape[1] == QKV_COLS + N_SLAB_GROUPS * D_MODEL

    qkv, slabs, gates_t = _in_proj(x2, norm_g.reshape(1, d), w_big, w_gate, conv_w,
                                   conv_b.reshape(1, -1), seq)
    attn = _attention(qkv, batch, seq)
    cell = _mlstm(slabs, gates_t, b_if, batch, seq)
    out = _merge_out(attn, slabs, cell, x2, w_attn_branch.astype(BF16), w_mlstm_branch.astype(BF16),
                     w_out.astype(BF16), mlstm_norm_g.reshape(1, d), final_norm_g.reshape(1, d))
    return out.reshape(batch, seq, d)
```

```python
import functools

import jax
import jax.numpy as jnp
from jax import lax
from jax.experimental import pallas as pl
from jax.experimental.pallas import tpu as pltpu

F32 = jnp.float32
BF16 = jnp.bfloat16

D_MODEL = 2048
ATTN_HEADS = 16
ATTN_HEAD_DIM = 128
MLSTM_HEADS = 8
MLSTM_QK_DIM = 128
MLSTM_V_DIM = 256
CONV_WIDTH = 4
NORM_EPS = 1e-6
BAND = 128
ATTN_ITERS = 2
SLAB = 256
GROUP_SLABS = D_MODEL // SLAB
QKV_COLS = 3 * D_MODEL
S_AZ, S_MQK, S_MV, S_MO, S_MZ, S_GA, S_GM = range(7)
N_SLAB_GROUPS = 7
GATE_LANES = 128
MASKED = -1e30
CHUNK = 128
VMEM_LIMIT = 56 * 1024 * 1024


def _cparams(sem):
    return pltpu.CompilerParams(dimension_semantics=sem, vmem_limit_bytes=VMEM_LIMIT)


NEG_LOG2E = -1.4426950408889634


def _sigmoid(x):
    return 1.0 / (1.0 + jnp.exp2(x * NEG_LOG2E))


def _silu(x):
    return x * _sigmoid(x)


def _log_sigmoid(x):
    return jnp.minimum(x, 0.0) - jnp.log(1.0 + jnp.exp(-jnp.abs(x)))


IN_TM = 1024
IN_TN = 1024
QKV_STEPS = QKV_COLS // IN_TN


CONV_STEP0 = QKV_STEPS + S_MQK * (D_MODEL // IN_TN)
CONV_STEPS = D_MODEL // IN_TN
CONV_HALO = 8


def _in_proj_kernel(x_ref, g_ref, w_ref, wg_ref, cw_ref, cb_ref, qkv_ref, p_ref, gt_ref,
                    hn_ref, halo_ref, acc_ref, *, tiles_per_seq):
    i = pl.program_id(0)
    n = pl.program_id(1)

    @pl.when((i == 0) & (n == 0))
    def _():
        halo_ref[...] = jnp.zeros_like(halo_ref)

    @pl.when(n == 0)
    def _():
        x = x_ref[...]
        ms = jnp.mean(x * x, axis=-1, keepdims=True)
        hn = (x * lax.rsqrt(ms + NORM_EPS) * g_ref[...]).astype(BF16)
        hn_ref[...] = hn
        gates = jnp.dot(hn, wg_ref[...], preferred_element_type=F32)
        gates_t = gates.T
        for c in range(IN_TM // CHUNK):
            gt_ref[c] = gates_t[:2 * MLSTM_HEADS, c * CHUNK:(c + 1) * CHUNK]

    def slab_acc(j):
        return jnp.dot(hn_ref[...], w_ref[:, j * SLAB:(j + 1) * SLAB], preferred_element_type=F32)

    @pl.when(n < QKV_STEPS)
    def _():
        scale = jnp.where(n < D_MODEL // IN_TN, ATTN_HEAD_DIM ** -0.5, 1.0).astype(F32)
        for j in range(IN_TN // SLAB):
            acc = slab_acc(j) * scale
            qkv_ref[2 * j] = acc[:, :ATTN_HEAD_DIM]
            qkv_ref[2 * j + 1] = acc[:, ATTN_HEAD_DIM:]

    is_conv = (n >= CONV_STEP0) & (n < CONV_STEP0 + CONV_STEPS)

    @pl.when((n >= QKV_STEPS) & jnp.logical_not(is_conv))
    def _():
        for j in range(IN_TN // SLAB):
            p_ref[j] = slab_acc(j).astype(BF16)

    @pl.when(is_conv)
    def _():
        step = n - CONV_STEP0
        seq_start = (i % tiles_per_seq) == 0
        post = jnp.where(step == 1, MLSTM_QK_DIM ** -0.5, 1.0).astype(F32)
        def conv_slab(j):
            cols = slice(j * SLAB, (j + 1) * SLAB)
            acc = acc_ref[(n + j) % 2]
            prev = jnp.where(seq_start, 0.0, halo_ref[step, :, cols])
            halo_ref[step, :, cols] = acc[IN_TM - CONV_HALO:, :]
            ext = jnp.concatenate([prev, acc], axis=0)
            y = cb_ref[:, cols] + cw_ref[CONV_WIDTH - 1:CONV_WIDTH, cols] * acc
            for back in range(1, CONV_WIDTH):
                shifted = pltpu.roll(ext, back, axis=0)[CONV_HALO:, :]
                y = y + cw_ref[CONV_WIDTH - 1 - back:CONV_WIDTH - back, cols] * shifted
            p_ref[j] = (_silu(y) * post).astype(BF16)

        n_slabs = IN_TN // SLAB
        for j in range(n_slabs):
            acc_ref[(n + j) % 2] = slab_acc(j)
            if j > 0:
                conv_slab(j - 1)
        conv_slab(n_slabs - 1)


def _in_proj(x2, norm_g, w_big, w_gate, conv_w, conv_b, seq):
    t = x2.shape[0]
    n_cols = w_big.shape[1]
    grid = (t // IN_TM, n_cols // IN_TN)
    assert seq % IN_TM == 0 and CONV_STEPS * IN_TN == conv_w.shape[1]

    def conv_step(i, n):
        return (0, jnp.clip(n - CONV_STEP0, 0, CONV_STEPS - 1))

    return pl.pallas_call(
        functools.partial(_in_proj_kernel, tiles_per_seq=seq // IN_TM),
        grid=grid,
        in_specs=[
            pl.BlockSpec((IN_TM, D_MODEL), lambda i, n: (i, 0)),
            pl.BlockSpec((1, D_MODEL), lambda i, n: (0, 0)),
            pl.BlockSpec((D_MODEL, IN_TN), lambda i, n: (0, n)),
            pl.BlockSpec((D_MODEL, GATE_LANES), lambda i, n: (0, 0)),
            pl.BlockSpec((CONV_WIDTH, IN_TN), conv_step),
            pl.BlockSpec((1, IN_TN), conv_step),
        ],
        out_specs=[
            pl.BlockSpec((IN_TN // ATTN_HEAD_DIM, IN_TM, ATTN_HEAD_DIM),
                         lambda i, n: (jnp.minimum(n, QKV_STEPS - 1), i, 0)),
            pl.BlockSpec((IN_TN // SLAB, IN_TM, SLAB),
                         lambda i, n: (jnp.maximum(n - QKV_STEPS, 0), i, 0)),
            pl.BlockSpec((IN_TM // CHUNK, 2 * MLSTM_HEADS, CHUNK), lambda i, n: (i, 0, 0)),
        ],
        out_shape=[
            jax.ShapeDtypeStruct((QKV_COLS // ATTN_HEAD_DIM, t, ATTN_HEAD_DIM), F32),
            jax.ShapeDtypeStruct(((n_cols - QKV_COLS) // SLAB, t, SLAB), BF16),
            jax.ShapeDtypeStruct((t // CHUNK, 2 * MLSTM_HEADS, CHUNK), F32),
        ],
        scratch_shapes=[pltpu.VMEM((IN_TM, D_MODEL), BF16),
                        pltpu.VMEM((CONV_STEPS, CONV_HALO, IN_TN), F32),
                        pltpu.VMEM((2, IN_TM, SLAB), F32)],
        compiler_params=_cparams(("arbitrary", "arbitrary")),
        name="in_proj",
    )(x2, norm_g, w_big, w_gate, conv_w, conv_b)


def _band_bias(slope, dilation, first):
    qi = lax.broadcasted_iota(jnp.int32, (BAND, 2 * BAND), 0)
    ki = lax.broadcasted_iota(jnp.int32, (BAND, 2 * BAND), 1)
    dist = BAND + qi - ki
    valid = (dist >= 0) & (dist <= BAND)
    if first:
        valid = valid & (ki >= BAND)
    return jnp.where(valid, -slope * (dist * dilation).astype(F32), MASKED)


def _dense_bias(slope, first):
    qi = lax.broadcasted_iota(jnp.int32, (BAND, 2 * BAND), 0)
    ki = lax.broadcasted_iota(jnp.int32, (BAND, 2 * BAND), 1)
    pq, jq = qi // 32, qi % 32
    half, pk, jk = ki // BAND, (ki % BAND) // 32, ki % 32
    dist = 4 * (jq - jk + 32 * (1 - half)) + pq - pk
    valid = (dist >= 0) & (dist <= BAND)
    if first:
        valid = valid & (half == 1)
    return jnp.where(valid, -slope * dist.astype(F32), MASKED)


def _wide_bias(slope):
    qi = lax.broadcasted_iota(jnp.int32, (2 * BAND, 2 * BAND), 0)
    ki = lax.broadcasted_iota(jnp.int32, (2 * BAND, 2 * BAND), 1)
    dist = qi - ki
    valid = (dist >= 0) & (dist <= BAND)
    return jnp.where(valid, -slope * (dist * 16).astype(F32), MASKED)


def _softmax_block(q, k, v, bias):
    s = lax.dot_general(q, k, (((1,), (1,)), ((), ())), preferred_element_type=F32) + bias
    m = jnp.max(s, axis=-1, keepdims=True)
    p = jnp.exp(s - m).astype(BF16)
    v_ext = jnp.concatenate([v, jnp.ones_like(v)], axis=1)
    r = jnp.dot(p, v_ext, preferred_element_type=F32)
    hd = ATTN_HEAD_DIM
    return r[:, :hd], r[:, hd:], jnp.broadcast_to(m, (q.shape[0], hd))


def _attn_kernel(q_ref, k_ref, v_ref, o_ref,
                 x4_ref, x16_ref, tmp_ref, num_ref, den_ref, mx_ref, bias_ref, bias16_ref, onat_ref):
    seq = q_ref.shape[0]
    n_u4, n_u16 = seq // 4, seq // 16
    slope = jnp.exp2(jnp.full((1, 1), -8.0 / ATTN_HEADS, F32) * (pl.program_id(1) + 1))
    bias_ref[0, 0] = _dense_bias(slope, False)
    bias_ref[0, 1] = _dense_bias(slope, True)
    bias_ref[1, 0] = _band_bias(slope, 4, False)
    bias_ref[1, 1] = _band_bias(slope, 4, True)
    bias16_ref[...] = _wide_bias(slope)

    for ti, src in enumerate((q_ref, k_ref, v_ref)):
        for p4 in range(4):
            rows = src[pl.ds(p4, n_u4, stride=4), :]
            tmp_ref[p4] = rows
            x4_ref[ti, p4] = rows.astype(BF16)
        for p16 in range(16):
            x16_ref[ti, p16] = tmp_ref[p16 % 4, pl.ds(p16 // 4, n_u16, stride=4), :].astype(BF16)

    def dense_block(n):
        cur = pl.multiple_of(n * 32, 32)
        prev = pl.multiple_of(jnp.maximum(n - 1, 0) * 32, 32)
        first = jnp.where(n == 0, 1, 0)

        def rows(ti, start):
            return [x4_ref[ti, p, pl.ds(start, 32), :] for p in range(4)]

        q = jnp.concatenate(rows(0, cur), axis=0)
        k = jnp.concatenate(rows(1, prev) + rows(1, cur), axis=0)
        v = jnp.concatenate(rows(2, prev) + rows(2, cur), axis=0)
        num, den, mx = _softmax_block(q, k, v, bias_ref[0, first])
        for p in range(4):
            part = slice(p * 32, (p + 1) * 32)
            num_ref[0, p, pl.ds(cur, 32), :] = num[part]
            den_ref[0, p, pl.ds(cur, 32), :] = den[part]
            mx_ref[0, p, pl.ds(cur, 32), :] = mx[part]

    def band_block(p, n):
        cur = pl.multiple_of(n * BAND, BAND)
        prev = pl.multiple_of(jnp.maximum(n - 1, 0) * BAND, BAND)
        first = jnp.where(n == 0, 1, 0)
        q = x4_ref[0, p, pl.ds(cur, BAND), :]
        k = jnp.concatenate([x4_ref[1, p, pl.ds(prev, BAND), :], x4_ref[1, p, pl.ds(cur, BAND), :]], axis=0)
        v = jnp.concatenate([x4_ref[2, p, pl.ds(prev, BAND), :], x4_ref[2, p, pl.ds(cur, BAND), :]], axis=0)
        num, den, mx = _softmax_block(q, k, v, bias_ref[1, first])
        num_ref[1, p, pl.ds(cur, BAND), :] = num
        den_ref[1, p, pl.ds(cur, BAND), :] = den
        mx_ref[1, p, pl.ds(cur, BAND), :] = mx

    def wide_block(p4, a):
        p16 = 4 * a + p4
        num, den, mx = _softmax_block(x16_ref[0, p16], x16_ref[1, p16], x16_ref[2, p16], bias16_ref[...])
        rows = pl.ds(a, n_u16, stride=4)
        num_ref[2, p4, rows, :] = num
        den_ref[2, p4, rows, :] = den
        mx_ref[2, p4, rows, :] = mx

    blocks4 = n_u4 // BAND
    dense_per_it = n_u4 // 32 // ATTN_ITERS
    band_per_it = blocks4 // ATTN_ITERS
    wide_per_it = 4 // ATTN_ITERS

    def pattern_body(it, carry):
        for u in range(dense_per_it):
            dense_block(it * dense_per_it + u)
        for u in range(band_per_it):
            for p in range(4):
                band_block(p, it * band_per_it + u)
        for u in range(wide_per_it):
            for p4 in range(4):
                wide_block(p4, it * wide_per_it + u)
        return carry

    lax.fori_loop(0, ATTN_ITERS, pattern_body, 0)

    def merge_body(it, carry):
        p, n = it // blocks4, it % blocks4
        rows = pl.ds(pl.multiple_of(n * BAND, BAND), BAND)
        m0, m1, m2 = mx_ref[0, p, rows, :], mx_ref[1, p, rows, :], mx_ref[2, p, rows, :]
        m_all = jnp.maximum(jnp.maximum(m0, m1), m2)
        w0, w1, w2 = jnp.exp(m0 - m_all), jnp.exp(m1 - m_all), jnp.exp(m2 - m_all)
        num = w0 * num_ref[0, p, rows, :] + w1 * num_ref[1, p, rows, :] + w2 * num_ref[2, p, rows, :]
        den = w0 * den_ref[0, p, rows, :] + w1 * den_ref[1, p, rows, :] + w2 * den_ref[2, p, rows, :]
        onat_ref[pl.ds(p + 4 * BAND * n, BAND, stride=4), :] = num / den
        return carry

    lax.fori_loop(0, 4 * blocks4, merge_body, 0)
    o_ref[...] = onat_ref[...].astype(o_ref.dtype)


def _attention(qkv, batch, seq):
    assert seq // 16 == 2 * BAND
    hd = ATTN_HEAD_DIM
    qkv4 = qkv.reshape(3 * ATTN_HEADS, batch, seq, hd)

    def spec(which):
        return pl.BlockSpec((None, None, seq, hd), lambda b, h: (which * ATTN_HEADS + h, b, 0, 0))

    out = pl.pallas_call(
        _attn_kernel,
        grid=(batch, ATTN_HEADS),
        in_specs=[spec(0), spec(1), spec(2)],
        out_specs=pl.BlockSpec((None, None, seq, hd), lambda b, h: (h, b, 0, 0)),
        out_shape=jax.ShapeDtypeStruct((ATTN_HEADS, batch, seq, hd), BF16),
        scratch_shapes=[
            pltpu.VMEM((3, 4, seq // 4, hd), BF16),
            pltpu.VMEM((3, 16, seq // 16, hd), BF16),
            pltpu.VMEM((4, seq // 4, hd), F32),
            pltpu.VMEM((3, 4, seq // 4, hd), F32),
            pltpu.VMEM((3, 4, seq // 4, hd), F32),
            pltpu.VMEM((3, 4, seq // 4, hd), F32),
            pltpu.VMEM((2, 2, BAND, 2 * BAND), F32),
            pltpu.VMEM((2 * BAND, 2 * BAND), F32),
            pltpu.VMEM((seq, hd), F32),
        ],
        compiler_params=_cparams(("parallel", "parallel")),
        name="dilated_attention",
    )(qkv4, qkv4, qkv4)
    return out.reshape(ATTN_HEADS, batch * seq, hd)


MLSTM_TS = 512
ONES_LANES = 128


def _mlstm_kernel(bias_ref, q_ref, k_ref, v_ref, gt_ref, out_ref, c_ref, m_ref):
    s = pl.program_id(1)
    dk, dv = MLSTM_QK_DIM, MLSTM_V_DIM
    L = CHUNK

    @pl.when(s == 0)
    def _():
        c_ref[...] = jnp.zeros_like(c_ref)
        m_ref[...] = jnp.zeros_like(m_ref)

    ti = lax.broadcasted_iota(jnp.int32, (CHUNK, CHUNK), 0)
    si = lax.broadcasted_iota(jnp.int32, (CHUNK, CHUNK), 1)
    causal = ti >= si
    csum = jnp.where(ti <= si, 1.0, 0.0).astype(F32)

    def chunk_body(c, carry):
        start = pl.multiple_of(c * CHUNK, CHUNK)
        gates = gt_ref[c] + bias_ref[...]
        ig_all = gates[:MLSTM_HEADS]
        lf_all = _log_sigmoid(gates[MLSTM_HEADS:])
        b_all = jnp.dot(lf_all, csum, preferred_element_type=F32,
                        precision=lax.Precision.HIGHEST)
        c_all = ig_all - b_all
        cols_all = jnp.concatenate([b_all, c_all], axis=0).T
        for h in range(MLSTM_HEADS):
            sl, lane0 = h // 2, (h % 2) * dk
            qb = q_ref[sl, pl.ds(start, L), lane0:lane0 + dk]
            kb = k_ref[sl, pl.ds(start, L), lane0:lane0 + dk]
            vb = v_ref[h, pl.ds(start, L), :]
            v_ext = jnp.concatenate([vb, jnp.ones((L, ONES_LANES), BF16)], axis=1)
            b_row, c_row = b_all[h:h + 1], c_all[h:h + 1]
            b_last = b_row[:, L - 1:L]
            b_b = jnp.broadcast_to(cols_all[:, h:h + 1], (L, L))
            c_b = jnp.broadcast_to(cols_all[:, MLSTM_HEADS + h:MLSTM_HEADS + h + 1], (L, L))

            m_prev = m_ref[h]
            a_b = b_b + m_prev
            dmat = jnp.where(causal, b_b + c_row, MASKED)
            m_t = jnp.maximum(a_b, jnp.broadcast_to(jnp.max(dmat, axis=-1, keepdims=True), (L, L)))
            w_intra = jnp.exp(dmat - m_t)
            w_inter = jnp.exp(a_b - m_t)
            sc = lax.dot_general(qb, kb, (((1,), (1,)), ((), ())),
                                 preferred_element_type=F32) * w_intra
            c_st = c_ref[h]
            q_c = jnp.dot(qb, c_st.astype(BF16), preferred_element_type=F32)

            m_new = b_last + jnp.maximum(m_prev, jnp.max(c_row, axis=-1, keepdims=True))
            w_old = jnp.exp(b_last + m_prev - m_new)
            kw = kb.astype(F32) * jnp.exp(b_last + c_b - m_new)
            lhs = jnp.concatenate([sc.astype(BF16), kw.T.astype(BF16)], axis=0)
            both = jnp.dot(lhs, v_ext, preferred_element_type=F32)

            den = w_inter * q_c[:, dv:] + both[:L, dv:]
            inv = 1.0 / jnp.maximum(jnp.abs(den), jnp.exp(-m_t))
            for half in range(dv // L):
                lanes = slice(half * L, (half + 1) * L)
                h_half = (w_inter * q_c[:, lanes] + both[:L, lanes]) * inv
                out_ref[h, pl.ds(start, L), lanes] = h_half.astype(out_ref.dtype)

            c_ref[h] = w_old * c_st + both[L:]
            m_ref[h] = m_new
        return carry

    lax.fori_loop(0, MLSTM_TS // CHUNK, chunk_body, 0)


def _mlstm(slabs, gates_t, b_if, batch, seq):
    n_slabs = slabs.shape[0]
    s4d = slabs.reshape(n_slabs, batch, seq, SLAB)
    half = MLSTM_HEADS // 2
    tiles = seq // MLSTM_TS

    def tile_spec(first_slab, n):
        return pl.BlockSpec((n, None, MLSTM_TS, SLAB), lambda b, s: (first_slab // n, b, s, 0))

    q0 = S_MQK * GROUP_SLABS
    out = pl.pallas_call(
        _mlstm_kernel,
        grid=(batch, tiles),
        in_specs=[
            pl.BlockSpec((2 * MLSTM_HEADS, 1), lambda b, s: (0, 0)),
            tile_spec(q0, half), tile_spec(q0 + half, half),
            tile_spec(S_MV * GROUP_SLABS, MLSTM_HEADS),
            pl.BlockSpec((MLSTM_TS // CHUNK, 2 * MLSTM_HEADS, CHUNK), lambda b, s: (b * tiles + s, 0, 0)),
        ],
        out_specs=pl.BlockSpec((MLSTM_HEADS, None, MLSTM_TS, SLAB), lambda b, s: (0, b, s, 0)),
        out_shape=jax.ShapeDtypeStruct((MLSTM_HEADS, batch, seq, SLAB), BF16),
        scratch_shapes=[
            pltpu.VMEM((MLSTM_HEADS, MLSTM_QK_DIM, MLSTM_V_DIM + ONES_LANES), F32),
            pltpu.VMEM((MLSTM_HEADS, 1, 1), F32),
        ],
        compiler_params=_cparams(("parallel", "arbitrary")),
        name="mlstm",
    )(b_if.reshape(-1, 1), s4d, s4d, s4d, gates_t)
    return out.reshape(MLSTM_HEADS, batch * seq, SLAB)


MERGE_TM = 512
OUT_TM = 512


def _weight_spec():
    return pl.BlockSpec((D_MODEL, D_MODEL), lambda i: (0, 0), pipeline_mode=pl.Buffered(1))


def _branch_merge_kernel(attn_ref, az_ref, cell_ref, mo_ref, mz_ref, ga_ref, gm_ref,
                         wa_ref, wm_ref, ng_ref, merged_ref):
    n = GROUP_SLABS
    hd = ATTN_HEAD_DIM
    gated = []
    for c in range(n):
        attn = jnp.concatenate([attn_ref[2 * c].astype(F32), attn_ref[2 * c + 1].astype(F32)], axis=1)
        gated.append(attn * _silu(az_ref[c].astype(F32)))
    gated = jnp.concatenate(gated, axis=1).astype(BF16)
    y_a, mem = [], []
    for h in range(MLSTM_HEADS):
        cols = slice(h * SLAB, (h + 1) * SLAB)
        y_a.append(jnp.dot(gated, wa_ref[:, cols], preferred_element_type=F32))
        cell = _sigmoid(mo_ref[h].astype(F32)) * cell_ref[h].astype(F32)
        cell = cell * lax.rsqrt(jnp.mean(cell * cell, axis=-1, keepdims=True) + NORM_EPS)
        cell = cell * ng_ref[:, cols]
        mem.append(cell * _silu(mz_ref[h].astype(F32)))
    mem = jnp.concatenate(mem, axis=1).astype(BF16)
    for c in range(n):
        cols = slice(c * SLAB, (c + 1) * SLAB)
        y_m = jnp.dot(mem, wm_ref[:, cols], preferred_element_type=F32)
        merged = _sigmoid(ga_ref[c].astype(F32)) * y_a[c] + _sigmoid(gm_ref[c].astype(F32)) * y_m
        merged_ref[:, cols] = merged.astype(merged_ref.dtype)


def _branch_merge(attn, slabs, cell, w_a, w_m, norm_g):
    t = attn.shape[1]
    assert MLSTM_V_DIM == SLAB and MLSTM_HEADS == GROUP_SLABS

    def slab_spec(group):
        return pl.BlockSpec((GROUP_SLABS, MERGE_TM, SLAB), lambda i: (group, i, 0))

    return pl.pallas_call(
        _branch_merge_kernel,
        grid=(t // MERGE_TM,),
        in_specs=[
            pl.BlockSpec((ATTN_HEADS, MERGE_TM, ATTN_HEAD_DIM), lambda i: (0, i, 0)),
            slab_spec(S_AZ), slab_spec(0), slab_spec(S_MO), slab_spec(S_MZ),
            slab_spec(S_GA), slab_spec(S_GM),
            _weight_spec(), _weight_spec(),
            pl.BlockSpec((1, D_MODEL), lambda i: (0, 0)),
        ],
        out_specs=pl.BlockSpec((MERGE_TM, D_MODEL), lambda i: (i, 0)),
        out_shape=jax.ShapeDtypeStruct((t, D_MODEL), BF16),
        compiler_params=_cparams(("parallel",)),
        name="branch_merge",
    )(attn, slabs, cell, slabs, slabs, slabs, slabs, w_a, w_m, norm_g)


def _out_proj_kernel(merged_ref, x_ref, wo_ref, fg_ref, out_ref):
    y = x_ref[...] + jnp.dot(merged_ref[...], wo_ref[...], preferred_element_type=F32)
    y = y * lax.rsqrt(jnp.mean(y * y, axis=-1, keepdims=True) + NORM_EPS)
    out_ref[...] = y * fg_ref[...]


def _out_proj(merged, x2, w_o, final_g):
    t = x2.shape[0]
    return pl.pallas_call(
        _out_proj_kernel,
        grid=(t // OUT_TM,),
        in_specs=[
            pl.BlockSpec((OUT_TM, D_MODEL), lambda i: (i, 0)),
            pl.BlockSpec((OUT_TM, D_MODEL), lambda i: (i, 0)),
            _weight_spec(),
            pl.BlockSpec((1, D_MODEL), lambda i: (0, 0)),
        ],
        out_specs=pl.BlockSpec((OUT_TM, D_MODEL), lambda i: (i, 0)),
        out_shape=jax.ShapeDtypeStruct((t, D_MODEL), F32),
        compiler_params=_cparams(("parallel",)),
        name="out_proj",
    )(merged, x2, w_o, final_g)


def kernel(x, norm_g, w_in, b_if, conv_w, conv_b, mlstm_norm_g, w_attn_branch, w_mlstm_branch,
           w_out, final_norm_g):
    batch, seq, d = x.shape
    assert d == D_MODEL and seq % (16 * BAND) == 0 and (batch * seq) % IN_TM == 0
    t = batch * seq
    x2 = x.reshape(t, d)

    gate_lo = 4 * D_MODEL + 2 * MLSTM_HEADS * MLSTM_QK_DIM + MLSTM_HEADS * MLSTM_V_DIM
    gate_hi = gate_lo + 2 * MLSTM_HEADS
    w_big = jnp.concatenate([w_in[:, :gate_lo], w_in[:, gate_hi:]], axis=1).astype(BF16)
    w_gate = jnp.pad(w_in[:, gate_lo:gate_hi], ((0, 0), (0, GATE_LANES - 2 * MLSTM_HEADS))).astype(BF16)
    assert w_big.shape[1] == QKV_COLS + N_SLAB_GROUPS * D_MODEL

    qkv, slabs, gates_t = _in_proj(x2, norm_g.reshape(1, d), w_big, w_gate, conv_w,
                                   conv_b.reshape(1, -1), seq)
    attn = _attention(qkv, batch, seq)
    cell = _mlstm(slabs, gates_t, b_if, batch, seq)
    merged = _branch_merge(attn, slabs, cell, w_attn_branch.astype(BF16), w_mlstm_branch.astype(BF16),
                           mlstm_norm_g.reshape(1, d))
    out = _out_proj(merged, x2, w_out.astype(BF16), final_norm_g.reshape(1, d))
    return out.reshape(batch, seq, d)
```

```python
import functools

import jax
import jax.numpy as jnp
from jax import lax
from jax.experimental import pallas as pl
from jax.experimental.pallas import tpu as pltpu

F32 = jnp.float32
BF16 = jnp.bfloat16

D_MODEL = 2048
ATTN_HEADS = 16
ATTN_HEAD_DIM = 128
MLSTM_HEADS = 8
MLSTM_QK_DIM = 128
MLSTM_V_DIM = 256
CONV_WIDTH = 4
NORM_EPS = 1e-6
BAND = 128
ATTN_ITERS = 2
SLAB = 256
GROUP_SLABS = D_MODEL // SLAB
QKV_COLS = 3 * D_MODEL
S_AZ, S_MQK, S_MV, S_MO, S_MZ, S_GA, S_GM = range(7)
N_SLAB_GROUPS = 7
GATE_LANES = 128
MASKED = -1e30
CHUNK = 128
VMEM_LIMIT = 56 * 1024 * 1024


def _cparams(sem):
    return pltpu.CompilerParams(dimension_semantics=sem, vmem_limit_bytes=VMEM_LIMIT)


NEG_LOG2E = -1.4426950408889634


def _sigmoid(x):
    return 1.0 / (1.0 + jnp.exp2(x * NEG_LOG2E))


def _silu(x):
    return x * _sigmoid(x)


def _log_sigmoid(x):
    return jnp.minimum(x, 0.0) - jnp.log(1.0 + jnp.exp(-jnp.abs(x)))


IN_TM = 1024
IN_TN = 1024
QKV_STEPS = QKV_COLS // IN_TN


CONV_STEP0 = QKV_STEPS + S_MQK * (D_MODEL // IN_TN)
CONV_STEPS = D_MODEL // IN_TN
CONV_HALO = 8
LO_STEPS = (4 * D_MODEL + 2 * MLSTM_HEADS * MLSTM_QK_DIM + MLSTM_HEADS * MLSTM_V_DIM) // IN_TN


def _in_proj_kernel(x_ref, g_ref, wlo_ref, whi_ref, wg_ref, cw_ref, cb_ref, qkv_ref, p_ref, gt_ref, kt_ref,
                    hn_ref, halo_ref, acc_ref, *, tiles_per_seq):
    i = pl.program_id(0)
    n = pl.program_id(1)

    @pl.when((i == 0) & (n == 0))
    def _():
        halo_ref[...] = jnp.zeros_like(halo_ref)

    @pl.when(n == 0)
    def _():
        x = x_ref[...]
        ms = jnp.mean(x * x, axis=-1, keepdims=True)
        hn = (x * lax.rsqrt(ms + NORM_EPS) * g_ref[...]).astype(BF16)
        hn_ref[...] = hn
        gates = jnp.dot(hn, wg_ref[...], preferred_element_type=F32)
        gates_t = gates.T
        for c in range(IN_TM // CHUNK):
            gt_ref[c] = gates_t[:2 * MLSTM_HEADS, c * CHUNK:(c + 1) * CHUNK]

    def slab_acc(j, w_ref=wlo_ref):
        return jnp.dot(hn_ref[...], w_ref[:, j * SLAB:(j + 1) * SLAB], preferred_element_type=F32)

    @pl.when(n < QKV_STEPS)
    def _():
        scale = jnp.where(n < D_MODEL // IN_TN, ATTN_HEAD_DIM ** -0.5, 1.0).astype(F32)
        for j in range(IN_TN // SLAB):
            acc = slab_acc(j) * scale
            qkv_ref[2 * j] = acc[:, :ATTN_HEAD_DIM]
            qkv_ref[2 * j + 1] = acc[:, ATTN_HEAD_DIM:]

    is_conv = (n >= CONV_STEP0) & (n < CONV_STEP0 + CONV_STEPS)

    @pl.when((n >= QKV_STEPS) & (n < LO_STEPS) & jnp.logical_not(is_conv))
    def _():
        for j in range(IN_TN // SLAB):
            p_ref[j] = slab_acc(j).astype(BF16)

    @pl.when(n >= LO_STEPS)
    def _():
        for j in range(IN_TN // SLAB):
            p_ref[j] = slab_acc(j, whi_ref).astype(BF16)

    @pl.when(is_conv)
    def _():
        step = n - CONV_STEP0
        seq_start = (i % tiles_per_seq) == 0
        post = jnp.where(step == 1, MLSTM_QK_DIM ** -0.5, 1.0).astype(F32)
        def conv_slab(j):
            cols = slice(j * SLAB, (j + 1) * SLAB)
            acc = acc_ref[(n + j) % 2]
            prev = jnp.where(seq_start, 0.0, halo_ref[step, :, cols])
            halo_ref[step, :, cols] = acc[IN_TM - CONV_HALO:, :]
            ext = jnp.concatenate([prev, acc], axis=0)
            y = cb_ref[:, cols] + cw_ref[CONV_WIDTH - 1:CONV_WIDTH, cols] * acc
            for back in range(1, CONV_WIDTH):
                shifted = pltpu.roll(ext, back, axis=0)[CONV_HALO:, :]
                y = y + cw_ref[CONV_WIDTH - 1 - back:CONV_WIDTH - back, cols] * shifted
            out = _silu(y) * post
            p_ref[j] = out.astype(BF16)

            @pl.when(step == 1)
            def _():
                out_t = out.T.astype(BF16)
                for hh in range(SLAB // MLSTM_QK_DIM):
                    for c in range(IN_TM // CHUNK):
                        kt_ref[2 * j + hh, c] = out_t[hh * MLSTM_QK_DIM:(hh + 1) * MLSTM_QK_DIM,
                                                      c * CHUNK:(c + 1) * CHUNK]

        n_slabs = IN_TN // SLAB
        for j in range(n_slabs):
            acc_ref[(n + j) % 2] = slab_acc(j)
            if j > 0:
                conv_slab(j - 1)
        conv_slab(n_slabs - 1)


def _in_proj(x2, norm_g, w_lo, w_hi, w_gate, conv_w, conv_b, seq):
    t = x2.shape[0]
    n_cols = LO_STEPS * IN_TN + w_hi.shape[1]
    grid = (t // IN_TM, n_cols // IN_TN)
    assert seq % IN_TM == 0 and CONV_STEPS * IN_TN == conv_w.shape[1]
    assert w_lo.shape[1] >= LO_STEPS * IN_TN and w_hi.shape[1] % IN_TN == 0

    def conv_step(i, n):
        return (0, jnp.clip(n - CONV_STEP0, 0, CONV_STEPS - 1))

    return pl.pallas_call(
        functools.partial(_in_proj_kernel, tiles_per_seq=seq // IN_TM),
        grid=grid,
        in_specs=[
            pl.BlockSpec((IN_TM, D_MODEL), lambda i, n: (i, 0)),
            pl.BlockSpec((1, D_MODEL), lambda i, n: (0, 0)),
            pl.BlockSpec((D_MODEL, IN_TN), lambda i, n: (0, jnp.minimum(n, LO_STEPS - 1))),
            pl.BlockSpec((D_MODEL, IN_TN), lambda i, n: (0, jnp.maximum(n - LO_STEPS, 0))),
            pl.BlockSpec((D_MODEL, GATE_LANES), lambda i, n: (0, 0)),
            pl.BlockSpec((CONV_WIDTH, IN_TN), conv_step),
            pl.BlockSpec((1, IN_TN), conv_step),
        ],
        out_specs=[
            pl.BlockSpec((IN_TN // ATTN_HEAD_DIM, IN_TM, ATTN_HEAD_DIM),
                         lambda i, n: (jnp.minimum(n, QKV_STEPS - 1), i, 0)),
            pl.BlockSpec((IN_TN // SLAB, IN_TM, SLAB),
                         lambda i, n: (jnp.maximum(n - QKV_STEPS, 0), i, 0)),
            pl.BlockSpec((IN_TM // CHUNK, 2 * MLSTM_HEADS, CHUNK), lambda i, n: (i, 0, 0)),
            pl.BlockSpec((MLSTM_HEADS, IN_TM // CHUNK, MLSTM_QK_DIM, CHUNK), lambda i, n: (0, i, 0, 0)),
        ],
        out_shape=[
            jax.ShapeDtypeStruct((QKV_COLS // ATTN_HEAD_DIM, t, ATTN_HEAD_DIM), F32),
            jax.ShapeDtypeStruct(((n_cols - QKV_COLS) // SLAB, t, SLAB), BF16),
            jax.ShapeDtypeStruct((t // CHUNK, 2 * MLSTM_HEADS, CHUNK), F32),
            jax.ShapeDtypeStruct((MLSTM_HEADS, t // CHUNK, MLSTM_QK_DIM, CHUNK), BF16),
        ],
        scratch_shapes=[pltpu.VMEM((IN_TM, D_MODEL), BF16),
                        pltpu.VMEM((CONV_STEPS, CONV_HALO, IN_TN), F32),
                        pltpu.VMEM((2, IN_TM, SLAB), F32)],
        compiler_params=_cparams(("arbitrary", "arbitrary")),
        name="in_proj",
    )(x2, norm_g, w_lo, w_hi, w_gate, conv_w, conv_b)


def _band_bias(slope, dilation, first):
    qi = lax.broadcasted_iota(jnp.int32, (BAND, 2 * BAND), 0)
    ki = lax.broadcasted_iota(jnp.int32, (BAND, 2 * BAND), 1)
    dist = BAND + qi - ki
    valid = (dist >= 0) & (dist <= BAND)
    if first:
        valid = valid & (ki >= BAND)
    return jnp.where(valid, -slope * (dist * dilation).astype(F32), MASKED)


def _dense_bias(slope, first):
    qi = lax.broadcasted_iota(jnp.int32, (BAND, 2 * BAND), 0)
    ki = lax.broadcasted_iota(jnp.int32, (BAND, 2 * BAND), 1)
    pq, jq = qi // 32, qi % 32
    half, pk, jk = ki // BAND, (ki % BAND) // 32, ki % 32
    dist = 4 * (jq - jk + 32 * (1 - half)) + pq - pk
    valid = (dist >= 0) & (dist <= BAND)
    if first:
        valid = valid & (half == 1)
    return jnp.where(valid, -slope * dist.astype(F32), MASKED)


def _wide_bias(slope):
    qi = lax.broadcasted_iota(jnp.int32, (2 * BAND, 2 * BAND), 0)
    ki = lax.broadcasted_iota(jnp.int32, (2 * BAND, 2 * BAND), 1)
    dist = qi - ki
    valid = (dist >= 0) & (dist <= BAND)
    return jnp.where(valid, -slope * (dist * 16).astype(F32), MASKED)


def _softmax_block(q, k, v, bias):
    s = lax.dot_general(q, k, (((1,), (1,)), ((), ())), preferred_element_type=F32) + bias
    m = jnp.max(s, axis=-1, keepdims=True)
    p = jnp.exp(s - m).astype(BF16)
    v_ext = jnp.concatenate([v, jnp.ones_like(v)], axis=1)
    r = jnp.dot(p, v_ext, preferred_element_type=F32)
    hd = ATTN_HEAD_DIM
    return r[:, :hd], r[:, hd:], jnp.broadcast_to(m, (q.shape[0], hd))


def _attn_kernel(q_ref, k_ref, v_ref, o_ref,
                 x4_ref, x16_ref, tmp_ref, num_ref, den_ref, mx_ref, bias_ref, bias16_ref, onat_ref):
    seq = q_ref.shape[0]
    n_u4, n_u16 = seq // 4, seq // 16
    slope = jnp.exp2(jnp.full((1, 1), -8.0 / ATTN_HEADS, F32) * (pl.program_id(1) + 1))
    bias_ref[0, 0] = _dense_bias(slope, False)
    bias_ref[0, 1] = _dense_bias(slope, True)
    bias_ref[1, 0] = _band_bias(slope, 4, False)
    bias_ref[1, 1] = _band_bias(slope, 4, True)
    bias16_ref[...] = _wide_bias(slope)

    for ti, src in enumerate((q_ref, k_ref, v_ref)):
        for p4 in range(4):
            rows = src[pl.ds(p4, n_u4, stride=4), :]
            tmp_ref[p4] = rows
            x4_ref[ti, p4] = rows.astype(BF16)
        for p16 in range(16):
            x16_ref[ti, p16] = tmp_ref[p16 % 4, pl.ds(p16 // 4, n_u16, stride=4), :].astype(BF16)

    def dense_block(n):
        cur = pl.multiple_of(n * 32, 32)
        prev = pl.multiple_of(jnp.maximum(n - 1, 0) * 32, 32)
        first = jnp.where(n == 0, 1, 0)

        def rows(ti, start):
            return [x4_ref[ti, p, pl.ds(start, 32), :] for p in range(4)]

        q = jnp.concatenate(rows(0, cur), axis=0)
        k = jnp.concatenate(rows(1, prev) + rows(1, cur), axis=0)
        v = jnp.concatenate(rows(2, prev) + rows(2, cur), axis=0)
        num, den, mx = _softmax_block(q, k, v, bias_ref[0, first])
        for p in range(4):
            part = slice(p * 32, (p + 1) * 32)
            num_ref[0, p, pl.ds(cur, 32), :] = num[part]
            den_ref[0, p, pl.ds(cur, 32), :] = den[part]
            mx_ref[0, p, pl.ds(cur, 32), :] = mx[part]

    def band_block(p, n):
        cur = pl.multiple_of(n * BAND, BAND)
        prev = pl.multiple_of(jnp.maximum(n - 1, 0) * BAND, BAND)
        first = jnp.where(n == 0, 1, 0)
        q = x4_ref[0, p, pl.ds(cur, BAND), :]
        k = jnp.concatenate([x4_ref[1, p, pl.ds(prev, BAND), :], x4_ref[1, p, pl.ds(cur, BAND), :]], axis=0)
        v = jnp.concatenate([x4_ref[2, p, pl.ds(prev, BAND), :], x4_ref[2, p, pl.ds(cur, BAND), :]], axis=0)
        num, den, mx = _softmax_block(q, k, v, bias_ref[1, first])
        num_ref[1, p, pl.ds(cur, BAND), :] = num
        den_ref[1, p, pl.ds(cur, BAND), :] = den
        mx_ref[1, p, pl.ds(cur, BAND), :] = mx

    def wide_block(p4, a):
        p16 = 4 * a + p4
        num, den, mx = _softmax_block(x16_ref[0, p16], x16_ref[1, p16], x16_ref[2, p16], bias16_ref[...])
        rows = pl.ds(a, n_u16, stride=4)
        num_ref[2, p4, rows, :] = num
        den_ref[2, p4, rows, :] = den
        mx_ref[2, p4, rows, :] = mx

    blocks4 = n_u4 // BAND
    dense_per_it = n_u4 // 32 // ATTN_ITERS
    band_per_it = blocks4 // ATTN_ITERS
    wide_per_it = 4 // ATTN_ITERS

    def pattern_body(it, carry):
        for u in range(dense_per_it):
            dense_block(it * dense_per_it + u)
        for u in range(band_per_it):
            for p in range(4):
                band_block(p, it * band_per_it + u)
        for u in range(wide_per_it):
            for p4 in range(4):
                wide_block(p4, it * wide_per_it + u)
        return carry

    lax.fori_loop(0, ATTN_ITERS, pattern_body, 0)

    def merge_body(it, carry):
        p, n = it // blocks4, it % blocks4
        rows = pl.ds(pl.multiple_of(n * BAND, BAND), BAND)
        m0, m1, m2 = mx_ref[0, p, rows, :], mx_ref[1, p, rows, :], mx_ref[2, p, rows, :]
        m_all = jnp.maximum(jnp.maximum(m0, m1), m2)
        w0, w1, w2 = jnp.exp(m0 - m_all), jnp.exp(m1 - m_all), jnp.exp(m2 - m_all)
        num = w0 * num_ref[0, p, rows, :] + w1 * num_ref[1, p, rows, :] + w2 * num_ref[2, p, rows, :]
        den = w0 * den_ref[0, p, rows, :] + w1 * den_ref[1, p, rows, :] + w2 * den_ref[2, p, rows, :]
        onat_ref[pl.ds(p + 4 * BAND * n, BAND, stride=4), :] = num / den
        return carry

    lax.fori_loop(0, 4 * blocks4, merge_body, 0)
    o_ref[...] = onat_ref[...].astype(o_ref.dtype)


def _attention(qkv, batch, seq):
    assert seq // 16 == 2 * BAND
    hd = ATTN_HEAD_DIM
    qkv4 = qkv.reshape(3 * ATTN_HEADS, batch, seq, hd)

    def spec(which):
        return pl.BlockSpec((None, None, seq, hd), lambda b, h: (which * ATTN_HEADS + h, b, 0, 0))

    out = pl.pallas_call(
        _attn_kernel,
        grid=(batch, ATTN_HEADS),
        in_specs=[spec(0), spec(1), spec(2)],
        out_specs=pl.BlockSpec((None, None, seq, hd), lambda b, h: (h, b, 0, 0)),
        out_shape=jax.ShapeDtypeStruct((ATTN_HEADS, batch, seq, hd), BF16),
        scratch_shapes=[
            pltpu.VMEM((3, 4, seq // 4, hd), BF16),
            pltpu.VMEM((3, 16, seq // 16, hd), BF16),
            pltpu.VMEM((4, seq // 4, hd), F32),
            pltpu.VMEM((3, 4, seq // 4, hd), F32),
            pltpu.VMEM((3, 4, seq // 4, hd), F32),
            pltpu.VMEM((3, 4, seq // 4, hd), F32),
            pltpu.VMEM((2, 2, BAND, 2 * BAND), F32),
            pltpu.VMEM((2 * BAND, 2 * BAND), F32),
            pltpu.VMEM((seq, hd), F32),
        ],
        compiler_params=_cparams(("parallel", "parallel")),
        name="dilated_attention",
    )(qkv4, qkv4, qkv4)
    return out.reshape(ATTN_HEADS, batch * seq, hd)


MLSTM_TS = 512
ONES_LANES = 128


def _mlstm_kernel(bias_ref, q_ref, kt_ref, v_ref, gt_ref, out_ref, c_ref, m_ref):
    s = pl.program_id(1)
    dk, dv = MLSTM_QK_DIM, MLSTM_V_DIM
    L = CHUNK

    @pl.when(s == 0)
    def _():
        c_ref[...] = jnp.zeros_like(c_ref)
        m_ref[...] = jnp.zeros_like(m_ref)

    ti = lax.broadcasted_iota(jnp.int32, (CHUNK, CHUNK), 0)
    si = lax.broadcasted_iota(jnp.int32, (CHUNK, CHUNK), 1)
    causal = ti >= si
    csum = jnp.where(ti <= si, 1.0, 0.0).astype(F32)

    lane = lax.broadcasted_iota(jnp.int32, (MLSTM_HEADS, L), 1)

    def chunk_body(c, carry):
        start = pl.multiple_of(c * L, L)
        gates = gt_ref[c] + bias_ref[...]
        ig_all = gates[:MLSTM_HEADS]
        lf_all = _log_sigmoid(gates[MLSTM_HEADS:])
        b_all = jnp.dot(lf_all, csum, preferred_element_type=F32,
                        precision=lax.Precision.HIGHEST)
        c_all = ig_all - b_all
        cm_all = c_all
        shift = 1
        while shift < L:
            cm_all = jnp.maximum(cm_all, jnp.where(lane >= shift, pltpu.roll(cm_all, shift, axis=1), MASKED))
            shift *= 2
        m_prev = m_ref[...]
        b_last = b_all[:, L - 1:L]
        top = jnp.maximum(m_prev, cm_all)
        m_new = b_last + top[:, L - 1:L]
        w_old = jnp.exp(b_last + m_prev - m_new)
        w_key = jnp.exp(b_last + c_all - m_new)
        cols = jnp.concatenate([
            -top,
            jnp.exp(m_prev - top),
            jnp.exp(-(b_all + top)),
        ], axis=0).T
        m_ref[...] = m_new

        def column(group, h):
            j = group * MLSTM_HEADS + h
            return jnp.broadcast_to(cols[:, j:j + 1], (L, L))

        def q_of(h):
            return q_ref[h // 2, pl.ds(start, L), (h % 2) * dk:(h % 2 + 1) * dk]

        def v_ext_of(h):
            return jnp.concatenate([v_ref[h, pl.ds(start, L), :], jnp.ones((L, ONES_LANES), BF16)], axis=1)

        heads = range(MLSTM_HEADS)
        scores = [jnp.dot(q_of(h), kt_ref[h, c], preferred_element_type=F32) for h in heads]
        incs = [jnp.dot((kt_ref[h, c].astype(F32) * w_key[h:h + 1]).astype(BF16), v_ext_of(h),
                        preferred_element_type=F32) for h in heads]
        nds = []
        for h in heads:
            w_intra = jnp.exp(jnp.where(causal, column(0, h) + c_all[h:h + 1], MASKED))
            lhs = jnp.concatenate([(scores[h] * w_intra).astype(BF16),
                                   (q_of(h).astype(F32) * column(1, h)).astype(BF16)], axis=1)
            rhs = jnp.concatenate([v_ext_of(h), c_ref[h].astype(BF16)], axis=0)
            nds.append(jnp.dot(lhs, rhs, preferred_element_type=F32))
        for h in heads:
            nd = nds[h]
            inv = 1.0 / jnp.maximum(jnp.abs(nd[:, dv:]), column(2, h))
            for half in range(dv // L):
                lanes = slice(half * L, (half + 1) * L)
                out_ref[h, pl.ds(start, L), lanes] = (nd[:, lanes] * inv).astype(out_ref.dtype)
            c_ref[h] = w_old[h:h + 1] * c_ref[h] + incs[h]
        return carry

    lax.fori_loop(0, MLSTM_TS // CHUNK, chunk_body, 0, unroll=2)


def _mlstm(slabs, k_t, gates_t, b_if, batch, seq):
    n_slabs = slabs.shape[0]
    s4d = slabs.reshape(n_slabs, batch, seq, SLAB)
    half = MLSTM_HEADS // 2
    tiles = seq // MLSTM_TS
    tile_chunks = MLSTM_TS // CHUNK

    def tile_spec(first_slab, n):
        return pl.BlockSpec((n, None, MLSTM_TS, SLAB), lambda b, s: (first_slab // n, b, s, 0))

    out = pl.pallas_call(
        _mlstm_kernel,
        grid=(batch, tiles),
        in_specs=[
            pl.BlockSpec((2 * MLSTM_HEADS, 1), lambda b, s: (0, 0)),
            tile_spec(S_MQK * GROUP_SLABS, half),
            pl.BlockSpec((MLSTM_HEADS, tile_chunks, MLSTM_QK_DIM, CHUNK), lambda b, s: (0, b * tiles + s, 0, 0)),
            tile_spec(S_MV * GROUP_SLABS, MLSTM_HEADS),
            pl.BlockSpec((tile_chunks, 2 * MLSTM_HEADS, CHUNK), lambda b, s: (b * tiles + s, 0, 0)),
        ],
        out_specs=pl.BlockSpec((MLSTM_HEADS, None, MLSTM_TS, SLAB), lambda b, s: (0, b, s, 0)),
        out_shape=jax.ShapeDtypeStruct((MLSTM_HEADS, batch, seq, SLAB), BF16),
        scratch_shapes=[
            pltpu.VMEM((MLSTM_HEADS, MLSTM_QK_DIM, MLSTM_V_DIM + ONES_LANES), F32),
            pltpu.VMEM((MLSTM_HEADS, 1), F32),
        ],
        compiler_params=_cparams(("parallel", "arbitrary")),
        name="mlstm",
    )(b_if.reshape(-1, 1), s4d, k_t, s4d, gates_t)
    return out.reshape(MLSTM_HEADS, batch * seq, SLAB)


MERGE_TM = 512
OUT_TM = 512


def _weight_spec():
    return pl.BlockSpec((D_MODEL, D_MODEL), lambda i: (0, 0), pipeline_mode=pl.Buffered(1))


def _branch_merge_kernel(attn_ref, az_ref, cell_ref, mo_ref, mz_ref, ga_ref, gm_ref,
                         wa_ref, wm_ref, ng_ref, merged_ref):
    n = GROUP_SLABS
    hd = ATTN_HEAD_DIM
    gated = []
    for c in range(n):
        attn = jnp.concatenate([attn_ref[2 * c].astype(F32), attn_ref[2 * c + 1].astype(F32)], axis=1)
        gated.append(attn * _silu(az_ref[c].astype(F32)))
    gated = jnp.concatenate(gated, axis=1).astype(BF16)
    y_a, mem = [], []
    for h in range(MLSTM_HEADS):
        cols = slice(h * SLAB, (h + 1) * SLAB)
        y_a.append(jnp.dot(gated, wa_ref[:, cols], preferred_element_type=F32))
        cell = _sigmoid(mo_ref[h].astype(F32)) * cell_ref[h].astype(F32)
        cell = cell * lax.rsqrt(jnp.mean(cell * cell, axis=-1, keepdims=True) + NORM_EPS)
        cell = cell * ng_ref[:, cols]
        mem.append(cell * _silu(mz_ref[h].astype(F32)))
    mem = jnp.concatenate(mem, axis=1).astype(BF16)
    for c in range(n):
        cols = slice(c * SLAB, (c + 1) * SLAB)
        y_m = jnp.dot(mem, wm_ref[:, cols], preferred_element_type=F32)
        merged = _sigmoid(ga_ref[c].astype(F32)) * y_a[c] + _sigmoid(gm_ref[c].astype(F32)) * y_m
        merged_ref[:, cols] = merged.astype(merged_ref.dtype)


def _branch_merge(attn, slabs, cell, w_a, w_m, norm_g):
    t = attn.shape[1]
    assert MLSTM_V_DIM == SLAB and MLSTM_HEADS == GROUP_SLABS

    def slab_spec(group):
        return pl.BlockSpec((GROUP_SLABS, MERGE_TM, SLAB), lambda i: (group, i, 0))

    return pl.pallas_call(
        _branch_merge_kernel,
        grid=(t // MERGE_TM,),
        in_specs=[
            pl.BlockSpec((ATTN_HEADS, MERGE_TM, ATTN_HEAD_DIM), lambda i: (0, i, 0)),
            slab_spec(S_AZ), slab_spec(0), slab_spec(S_MO), slab_spec(S_MZ),
            slab_spec(S_GA), slab_spec(S_GM),
            _weight_spec(), _weight_spec(),
            pl.BlockSpec((1, D_MODEL), lambda i: (0, 0)),
        ],
        out_specs=pl.BlockSpec((MERGE_TM, D_MODEL), lambda i: (i, 0)),
        out_shape=jax.ShapeDtypeStruct((t, D_MODEL), BF16),
        compiler_params=_cparams(("parallel",)),
        name="branch_merge",
    )(attn, slabs, cell, slabs, slabs, slabs, slabs, w_a, w_m, norm_g)


def _out_proj_kernel(merged_ref, x_ref, wo_ref, fg_ref, out_ref):
    y = x_ref[...] + jnp.dot(merged_ref[...], wo_ref[...], preferred_element_type=F32)
    y = y * lax.rsqrt(jnp.mean(y * y, axis=-1, keepdims=True) + NORM_EPS)
    out_ref[...] = y * fg_ref[...]


def _out_proj(merged, x2, w_o, final_g):
    t = x2.shape[0]
    return pl.pallas_call(
        _out_proj_kernel,
        grid=(t // OUT_TM,),
        in_specs=[
            pl.BlockSpec((OUT_TM, D_MODEL), lambda i: (i, 0)),
            pl.BlockSpec((OUT_TM, D_MODEL), lambda i: (i, 0)),
            _weight_spec(),
            pl.BlockSpec((1, D_MODEL), lambda i: (0, 0)),
        ],
        out_specs=pl.BlockSpec((OUT_TM, D_MODEL), lambda i: (i, 0)),
        out_shape=jax.ShapeDtypeStruct((t, D_MODEL), F32),
        compiler_params=_cparams(("parallel",)),
        name="out_proj",
    )(merged, x2, w_o, final_g)


def kernel(x, norm_g, w_in, b_if, conv_w, conv_b, mlstm_norm_g, w_attn_branch, w_mlstm_branch,
           w_out, final_norm_g):
    batch, seq, d = x.shape
    assert d == D_MODEL and seq % (16 * BAND) == 0 and (batch * seq) % IN_TM == 0
    t = batch * seq
    x2 = x.reshape(t, d)

    gate_lo = LO_STEPS * IN_TN
    gate_hi = gate_lo + 2 * MLSTM_HEADS
    w_bf = w_in.astype(BF16)
    w_hi = w_bf[:, gate_hi:]
    w_gate = jnp.pad(w_bf[:, gate_lo:gate_hi], ((0, 0), (0, GATE_LANES - 2 * MLSTM_HEADS)))
    assert gate_lo + w_hi.shape[1] == QKV_COLS + N_SLAB_GROUPS * D_MODEL

    qkv, slabs, gates_t, k_t = _in_proj(x2, norm_g.reshape(1, d), w_bf, w_hi, w_gate, conv_w,
                                   conv_b.reshape(1, -1), seq)
    attn = _attention(qkv, batch, seq)
    cell = _mlstm(slabs, k_t, gates_t, b_if, batch, seq)
    merged = _branch_merge(attn, slabs, cell, w_attn_branch.astype(BF16), w_mlstm_branch.astype(BF16),
                           mlstm_norm_g.reshape(1, d))
    out = _out_proj(merged, x2, w_out.astype(BF16), final_norm_g.reshape(1, d))
    return out.reshape(batch, seq, d)
```

```python
import functools

import jax
import jax.numpy as jnp
from jax import lax
from jax.experimental import pallas as pl
from jax.experimental.pallas import tpu as pltpu

F32 = jnp.float32
BF16 = jnp.bfloat16

D_MODEL = 2048
ATTN_HEADS = 16
ATTN_HEAD_DIM = 128
MLSTM_HEADS = 8
MLSTM_QK_DIM = 128
MLSTM_V_DIM = 256
CONV_WIDTH = 4
NORM_EPS = 1e-6
BAND = 128
ATTN_ITERS = 2
SLAB = 256
GROUP_SLABS = D_MODEL // SLAB
QKV_COLS = 3 * D_MODEL
S_AZ, S_MQK, S_MV, S_MO, S_MZ, S_GA, S_GM = range(7)
N_SLAB_GROUPS = 7
GATE_LANES = 128
MASKED = -1e30
CHUNK = 128
VMEM_LIMIT = 56 * 1024 * 1024


def _cparams(sem):
    return pltpu.CompilerParams(dimension_semantics=sem, vmem_limit_bytes=VMEM_LIMIT)


NEG_LOG2E = -1.4426950408889634


def _sigmoid(x):
    return 1.0 / (1.0 + jnp.exp2(x * NEG_LOG2E))


def _silu(x):
    return x * _sigmoid(x)


def _log_sigmoid(x):
    return jnp.minimum(x, 0.0) - jnp.log(1.0 + jnp.exp(-jnp.abs(x)))


IN_TM = 1024
IN_TN = 1024
QKV_STEPS = QKV_COLS // IN_TN


CONV_STEP0 = QKV_STEPS + S_MQK * (D_MODEL // IN_TN)
CONV_STEPS = D_MODEL // IN_TN
CONV_HALO = 8
LO_STEPS = (4 * D_MODEL + 2 * MLSTM_HEADS * MLSTM_QK_DIM + MLSTM_HEADS * MLSTM_V_DIM) // IN_TN


def _in_proj_kernel(x_ref, g_ref, wlo_ref, whi_ref, wg_ref, cw_ref, cb_ref, qkv_ref, p_ref, gt_ref,
                    hn_ref, halo_ref, acc_ref, *, tiles_per_seq):
    i = pl.program_id(0)
    n = pl.program_id(1)

    @pl.when((i == 0) & (n == 0))
    def _():
        halo_ref[...] = jnp.zeros_like(halo_ref)

    @pl.when(n == 0)
    def _():
        x = x_ref[...]
        ms = jnp.mean(x * x, axis=-1, keepdims=True)
        hn = (x * lax.rsqrt(ms + NORM_EPS) * g_ref[...]).astype(BF16)
        hn_ref[...] = hn
        gates = jnp.dot(hn, wg_ref[...], preferred_element_type=F32)
        gates_t = gates.T
        for c in range(IN_TM // CHUNK):
            gt_ref[c] = gates_t[:2 * MLSTM_HEADS, c * CHUNK:(c + 1) * CHUNK]

    def slab_acc(j, w_ref=wlo_ref):
        return jnp.dot(hn_ref[...], w_ref[:, j * SLAB:(j + 1) * SLAB], preferred_element_type=F32)

    @pl.when(n < QKV_STEPS)
    def _():
        scale = jnp.where(n < D_MODEL // IN_TN, ATTN_HEAD_DIM ** -0.5, 1.0).astype(F32)
        for j in range(IN_TN // SLAB):
            acc = slab_acc(j) * scale
            qkv_ref[2 * j] = acc[:, :ATTN_HEAD_DIM]
            qkv_ref[2 * j + 1] = acc[:, ATTN_HEAD_DIM:]

    is_conv = (n >= CONV_STEP0) & (n < CONV_STEP0 + CONV_STEPS)

    @pl.when((n >= QKV_STEPS) & (n < LO_STEPS) & jnp.logical_not(is_conv))
    def _():
        for j in range(IN_TN // SLAB):
            p_ref[j] = slab_acc(j).astype(BF16)

    @pl.when(n >= LO_STEPS)
    def _():
        for j in range(IN_TN // SLAB):
            p_ref[j] = slab_acc(j, whi_ref).astype(BF16)

    @pl.when(is_conv)
    def _():
        step = n - CONV_STEP0
        seq_start = (i % tiles_per_seq) == 0
        post = jnp.where(step == 1, MLSTM_QK_DIM ** -0.5, 1.0).astype(F32)
        def conv_slab(j):
            cols = slice(j * SLAB, (j + 1) * SLAB)
            acc = acc_ref[(n + j) % 2]
            prev = jnp.where(seq_start, 0.0, halo_ref[step, :, cols])
            halo_ref[step, :, cols] = acc[IN_TM - CONV_HALO:, :]
            ext = jnp.concatenate([prev, acc], axis=0)
            y = cb_ref[:, cols] + cw_ref[CONV_WIDTH - 1:CONV_WIDTH, cols] * acc
            for back in range(1, CONV_WIDTH):
                shifted = pltpu.roll(ext, back, axis=0)[CONV_HALO:, :]
                y = y + cw_ref[CONV_WIDTH - 1 - back:CONV_WIDTH - back, cols] * shifted
            p_ref[j] = (_silu(y) * post).astype(BF16)

        n_slabs = IN_TN // SLAB
        for j in range(n_slabs):
            acc_ref[(n + j) % 2] = slab_acc(j)
            if j > 0:
                conv_slab(j - 1)
        conv_slab(n_slabs - 1)


def _in_proj(x2, norm_g, w_lo, w_hi, w_gate, conv_w, conv_b, seq):
    t = x2.shape[0]
    n_cols = LO_STEPS * IN_TN + w_hi.shape[1]
    grid = (t // IN_TM, n_cols // IN_TN)
    assert seq % IN_TM == 0 and CONV_STEPS * IN_TN == conv_w.shape[1]
    assert w_lo.shape[1] >= LO_STEPS * IN_TN and w_hi.shape[1] % IN_TN == 0

    def conv_step(i, n):
        return (0, jnp.clip(n - CONV_STEP0, 0, CONV_STEPS - 1))

    return pl.pallas_call(
        functools.partial(_in_proj_kernel, tiles_per_seq=seq // IN_TM),
        grid=grid,
        in_specs=[
            pl.BlockSpec((IN_TM, D_MODEL), lambda i, n: (i, 0)),
            pl.BlockSpec((1, D_MODEL), lambda i, n: (0, 0)),
            pl.BlockSpec((D_MODEL, IN_TN), lambda i, n: (0, jnp.minimum(n, LO_STEPS - 1))),
            pl.BlockSpec((D_MODEL, IN_TN), lambda i, n: (0, jnp.maximum(n - LO_STEPS, 0))),
            pl.BlockSpec((D_MODEL, GATE_LANES), lambda i, n: (0, 0)),
            pl.BlockSpec((CONV_WIDTH, IN_TN), conv_step),
            pl.BlockSpec((1, IN_TN), conv_step),
        ],
        out_specs=[
            pl.BlockSpec((IN_TN // ATTN_HEAD_DIM, IN_TM, ATTN_HEAD_DIM),
                         lambda i, n: (jnp.minimum(n, QKV_STEPS - 1), i, 0)),
            pl.BlockSpec((IN_TN // SLAB, IN_TM, SLAB),
                         lambda i, n: (jnp.maximum(n - QKV_STEPS, 0), i, 0)),
            pl.BlockSpec((IN_TM // CHUNK, 2 * MLSTM_HEADS, CHUNK), lambda i, n: (i, 0, 0)),
        ],
        out_shape=[
            jax.ShapeDtypeStruct((QKV_COLS // ATTN_HEAD_DIM, t, ATTN_HEAD_DIM), F32),
            jax.ShapeDtypeStruct(((n_cols - QKV_COLS) // SLAB, t, SLAB), BF16),
            jax.ShapeDtypeStruct((t // CHUNK, 2 * MLSTM_HEADS, CHUNK), F32),
        ],
        scratch_shapes=[pltpu.VMEM((IN_TM, D_MODEL), BF16),
                        pltpu.VMEM((CONV_STEPS, CONV_HALO, IN_TN), F32),
                        pltpu.VMEM((2, IN_TM, SLAB), F32)],
        compiler_params=_cparams(("arbitrary", "arbitrary")),
        name="in_proj",
    )(x2, norm_g, w_lo, w_hi, w_gate, conv_w, conv_b)


def _band_bias(slope, dilation, first):
    qi = lax.broadcasted_iota(jnp.int32, (BAND, 2 * BAND), 0)
    ki = lax.broadcasted_iota(jnp.int32, (BAND, 2 * BAND), 1)
    dist = BAND + qi - ki
    valid = (dist >= 0) & (dist <= BAND)
    if first:
        valid = valid & (ki >= BAND)
    return jnp.where(valid, -slope * (dist * dilation).astype(F32), MASKED)


def _dense_bias(slope, first):
    qi = lax.broadcasted_iota(jnp.int32, (BAND, 2 * BAND), 0)
    ki = lax.broadcasted_iota(jnp.int32, (BAND, 2 * BAND), 1)
    pq, jq = qi // 32, qi % 32
    half, pk, jk = ki // BAND, (ki % BAND) // 32, ki % 32
    dist = 4 * (jq - jk + 32 * (1 - half)) + pq - pk
    valid = (dist >= 0) & (dist <= BAND)
    if first:
        valid = valid & (half == 1)
    return jnp.where(valid, -slope * dist.astype(F32), MASKED)


def _wide_bias(slope):
    qi = lax.broadcasted_iota(jnp.int32, (2 * BAND, 2 * BAND), 0)
    ki = lax.broadcasted_iota(jnp.int32, (2 * BAND, 2 * BAND), 1)
    dist = qi - ki
    valid = (dist >= 0) & (dist <= BAND)
    return jnp.where(valid, -slope * (dist * 16).astype(F32), MASKED)


def _softmax_block(q, k, v, bias):
    s = lax.dot_general(q, k, (((1,), (1,)), ((), ())), preferred_element_type=F32) + bias
    m = jnp.max(s, axis=-1, keepdims=True)
    p = jnp.exp(s - m).astype(BF16)
    v_ext = jnp.concatenate([v, jnp.ones_like(v)], axis=1)
    r = jnp.dot(p, v_ext, preferred_element_type=F32)
    hd = ATTN_HEAD_DIM
    return r[:, :hd], r[:, hd:], jnp.broadcast_to(m, (q.shape[0], hd))


def _attn_kernel(q_ref, k_ref, v_ref, o_ref,
                 x4_ref, x16_ref, tmp_ref, num_ref, den_ref, mx_ref, bias_ref, bias16_ref, onat_ref):
    seq = q_ref.shape[0]
    n_u4, n_u16 = seq // 4, seq // 16
    slope = jnp.exp2(jnp.full((1, 1), -8.0 / ATTN_HEADS, F32) * (pl.program_id(1) + 1))
    bias_ref[0, 0] = _dense_bias(slope, False)
    bias_ref[0, 1] = _dense_bias(slope, True)
    bias_ref[1, 0] = _band_bias(slope, 4, False)
    bias_ref[1, 1] = _band_bias(slope, 4, True)
    bias16_ref[...] = _wide_bias(slope)

    for ti, src in enumerate((q_ref, k_ref, v_ref)):
        for p4 in range(4):
            rows = src[pl.ds(p4, n_u4, stride=4), :]
            tmp_ref[p4] = rows
            x4_ref[ti, p4] = rows.astype(BF16)
        for p16 in range(16):
            x16_ref[ti, p16] = tmp_ref[p16 % 4, pl.ds(p16 // 4, n_u16, stride=4), :].astype(BF16)

    def dense_block(n):
        cur = pl.multiple_of(n * 32, 32)
        prev = pl.multiple_of(jnp.maximum(n - 1, 0) * 32, 32)
        first = jnp.where(n == 0, 1, 0)

        def rows(ti, start):
            return [x4_ref[ti, p, pl.ds(start, 32), :] for p in range(4)]

        q = jnp.concatenate(rows(0, cur), axis=0)
        k = jnp.concatenate(rows(1, prev) + rows(1, cur), axis=0)
        v = jnp.concatenate(rows(2, prev) + rows(2, cur), axis=0)
        num, den, mx = _softmax_block(q, k, v, bias_ref[0, first])
        for p in range(4):
            part = slice(p * 32, (p + 1) * 32)
            num_ref[0, p, pl.ds(cur, 32), :] = num[part]
            den_ref[0, p, pl.ds(cur, 32), :] = den[part]
            mx_ref[0, p, pl.ds(cur, 32), :] = mx[part]

    def band_block(p, n):
        cur = pl.multiple_of(n * BAND, BAND)
        prev = pl.multiple_of(jnp.maximum(n - 1, 0) * BAND, BAND)
        first = jnp.where(n == 0, 1, 0)
        q = x4_ref[0, p, pl.ds(cur, BAND), :]
        k = jnp.concatenate([x4_ref[1, p, pl.ds(prev, BAND), :], x4_ref[1, p, pl.ds(cur, BAND), :]], axis=0)
        v = jnp.concatenate([x4_ref[2, p, pl.ds(prev, BAND), :], x4_ref[2, p, pl.ds(cur, BAND), :]], axis=0)
        num, den, mx = _softmax_block(q, k, v, bias_ref[1, first])
        num_ref[1, p, pl.ds(cur, BAND), :] = num
        den_ref[1, p, pl.ds(cur, BAND), :] = den
        mx_ref[1, p, pl.ds(cur, BAND), :] = mx

    def wide_block(p4, a):
        p16 = 4 * a + p4
        num, den, mx = _softmax_block(x16_ref[0, p16], x16_ref[1, p16], x16_ref[2, p16], bias16_ref[...])
        rows = pl.ds(a, n_u16, stride=4)
        num_ref[2, p4, rows, :] = num
        den_ref[2, p4, rows, :] = den
        mx_ref[2, p4, rows, :] = mx

    blocks4 = n_u4 // BAND
    dense_per_it = n_u4 // 32 // ATTN_ITERS
    band_per_it = blocks4 // ATTN_ITERS
    wide_per_it = 4 // ATTN_ITERS

    def pattern_body(it, carry):
        for u in range(dense_per_it):
            dense_block(it * dense_per_it + u)
        for u in range(band_per_it):
            for p in range(4):
                band_block(p, it * band_per_it + u)
        for u in range(wide_per_it):
            for p4 in range(4):
                wide_block(p4, it * wide_per_it + u)
        return carry

    lax.fori_loop(0, ATTN_ITERS, pattern_body, 0)

    def merge_body(it, carry):
        p, n = it // blocks4, it % blocks4
        rows = pl.ds(pl.multiple_of(n * BAND, BAND), BAND)
        m0, m1, m2 = mx_ref[0, p, rows, :], mx_ref[1, p, rows, :], mx_ref[2, p, rows, :]
        m_all = jnp.maximum(jnp.maximum(m0, m1), m2)
        w0, w1, w2 = jnp.exp(m0 - m_all), jnp.exp(m1 - m_all), jnp.exp(m2 - m_all)
        num = w0 * num_ref[0, p, rows, :] + w1 * num_ref[1, p, rows, :] + w2 * num_ref[2, p, rows, :]
        den = w0 * den_ref[0, p, rows, :] + w1 * den_ref[1, p, rows, :] + w2 * den_ref[2, p, rows, :]
        onat_ref[pl.ds(p + 4 * BAND * n, BAND, stride=4), :] = num / den
        return carry

    lax.fori_loop(0, 4 * blocks4, merge_body, 0)
    o_ref[...] = onat_ref[...].astype(o_ref.dtype)


def _attention(qkv, batch, seq):
    assert seq // 16 == 2 * BAND
    hd = ATTN_HEAD_DIM
    qkv4 = qkv.reshape(3 * ATTN_HEADS, batch, seq, hd)

    def spec(which):
        return pl.BlockSpec((None, None, seq, hd), lambda b, h: (which * ATTN_HEADS + h, b, 0, 0))

    out = pl.pallas_call(
        _attn_kernel,
        grid=(batch, ATTN_HEADS),
        in_specs=[spec(0), spec(1), spec(2)],
        out_specs=pl.BlockSpec((None, None, seq, hd), lambda b, h: (h, b, 0, 0)),
        out_shape=jax.ShapeDtypeStruct((ATTN_HEADS, batch, seq, hd), BF16),
        scratch_shapes=[
            pltpu.VMEM((3, 4, seq // 4, hd), BF16),
            pltpu.VMEM((3, 16, seq // 16, hd), BF16),
            pltpu.VMEM((4, seq // 4, hd), F32),
            pltpu.VMEM((3, 4, seq // 4, hd), F32),
            pltpu.VMEM((3, 4, seq // 4, hd), F32),
            pltpu.VMEM((3, 4, seq // 4, hd), F32),
            pltpu.VMEM((2, 2, BAND, 2 * BAND), F32),
            pltpu.VMEM((2 * BAND, 2 * BAND), F32),
            pltpu.VMEM((seq, hd), F32),
        ],
        compiler_params=_cparams(("parallel", "parallel")),
        name="dilated_attention",
    )(qkv4, qkv4, qkv4)
    return out.reshape(ATTN_HEADS, batch * seq, hd)


MLSTM_TS = 512
ONES_LANES = 128


def _mlstm_kernel(bias_ref, q_ref, k_ref, v_ref, gt_ref, out_ref, c_ref, m_ref):
    s = pl.program_id(1)
    dk, dv = MLSTM_QK_DIM, MLSTM_V_DIM
    L = CHUNK

    @pl.when(s == 0)
    def _():
        c_ref[...] = jnp.zeros_like(c_ref)
        m_ref[...] = jnp.zeros_like(m_ref)

    ti = lax.broadcasted_iota(jnp.int32, (CHUNK, CHUNK), 0)
    si = lax.broadcasted_iota(jnp.int32, (CHUNK, CHUNK), 1)
    causal = ti >= si
    csum = jnp.where(ti <= si, 1.0, 0.0).astype(F32)

    lane = lax.broadcasted_iota(jnp.int32, (MLSTM_HEADS, L), 1)

    def chunk_body(c, carry):
        start = pl.multiple_of(c * L, L)
        gates = gt_ref[c] + bias_ref[...]
        ig_all = gates[:MLSTM_HEADS]
        lf_all = _log_sigmoid(gates[MLSTM_HEADS:])
        b_all = jnp.dot(lf_all, csum, preferred_element_type=F32,
                        precision=lax.Precision.HIGHEST)
        c_all = ig_all - b_all
        cm_all = c_all
        shift = 1
        while shift < L:
            cm_all = jnp.maximum(cm_all, jnp.where(lane >= shift, pltpu.roll(cm_all, shift, axis=1), MASKED))
            shift *= 2
        m_prev = m_ref[...]
        b_last = b_all[:, L - 1:L]
        top = jnp.maximum(m_prev, cm_all)
        m_new = b_last + top[:, L - 1:L]
        w_old = jnp.exp(b_last + m_prev - m_new)
        cols = jnp.concatenate([
            -top,
            jnp.exp(m_prev - top),
            jnp.exp(-(b_all + top)),
            jnp.exp(b_last + c_all - m_new),
        ], axis=0).T
        m_ref[...] = m_new

        def column(group, h):
            j = group * MLSTM_HEADS + h
            return jnp.broadcast_to(cols[:, j:j + 1], (L, L))

        def q_of(h):
            return q_ref[h // 2, pl.ds(start, L), (h % 2) * dk:(h % 2 + 1) * dk]

        def v_ext_of(h):
            return jnp.concatenate([v_ref[h, pl.ds(start, L), :], jnp.ones((L, ONES_LANES), BF16)], axis=1)

        def k_of(h):
            return k_ref[h // 2, pl.ds(start, L), (h % 2) * dk:(h % 2 + 1) * dk]

        nt = (((1,), (1,)), ((), ()))
        heads = range(MLSTM_HEADS)
        scores = [lax.dot_general(q_of(h), k_of(h), nt, preferred_element_type=F32) for h in heads]
        incs = [jnp.dot((k_of(h).astype(F32) * column(3, h)).T.astype(BF16), v_ext_of(h),
                        preferred_element_type=F32) for h in heads]
        nds = []
        for h in heads:
            w_intra = jnp.exp(jnp.where(causal, column(0, h) + c_all[h:h + 1], MASKED))
            lhs = jnp.concatenate([(scores[h] * w_intra).astype(BF16),
                                   (q_of(h).astype(F32) * column(1, h)).astype(BF16)], axis=1)
            rhs = jnp.concatenate([v_ext_of(h), c_ref[h].astype(BF16)], axis=0)
            nds.append(jnp.dot(lhs, rhs, preferred_element_type=F32))
        for h in heads:
            nd = nds[h]
            inv = 1.0 / jnp.maximum(jnp.abs(nd[:, dv:]), column(2, h))
            for half in range(dv // L):
                lanes = slice(half * L, (half + 1) * L)
                out_ref[h, pl.ds(start, L), lanes] = (nd[:, lanes] * inv).astype(out_ref.dtype)
            c_ref[h] = w_old[h:h + 1] * c_ref[h] + incs[h]
        return carry

    lax.fori_loop(0, MLSTM_TS // CHUNK, chunk_body, 0, unroll=2)


def _mlstm(slabs, gates_t, b_if, batch, seq):
    n_slabs = slabs.shape[0]
    s4d = slabs.reshape(n_slabs, batch, seq, SLAB)
    half = MLSTM_HEADS // 2
    tiles = seq // MLSTM_TS
    tile_chunks = MLSTM_TS // CHUNK

    def tile_spec(first_slab, n):
        return pl.BlockSpec((n, None, MLSTM_TS, SLAB), lambda b, s: (first_slab // n, b, s, 0))

    out = pl.pallas_call(
        _mlstm_kernel,
        grid=(batch, tiles),
        in_specs=[
            pl.BlockSpec((2 * MLSTM_HEADS, 1), lambda b, s: (0, 0)),
            tile_spec(S_MQK * GROUP_SLABS, half), tile_spec(S_MQK * GROUP_SLABS + half, half),
            tile_spec(S_MV * GROUP_SLABS, MLSTM_HEADS),
            pl.BlockSpec((tile_chunks, 2 * MLSTM_HEADS, CHUNK), lambda b, s: (b * tiles + s, 0, 0)),
        ],
        out_specs=pl.BlockSpec((MLSTM_HEADS, None, MLSTM_TS, SLAB), lambda b, s: (0, b, s, 0)),
        out_shape=jax.ShapeDtypeStruct((MLSTM_HEADS, batch, seq, SLAB), BF16),
        scratch_shapes=[
            pltpu.VMEM((MLSTM_HEADS, MLSTM_QK_DIM, MLSTM_V_DIM + ONES_LANES), F32),
            pltpu.VMEM((MLSTM_HEADS, 1), F32),
        ],
        compiler_params=_cparams(("parallel", "arbitrary")),
        name="mlstm",
    )(b_if.reshape(-1, 1), s4d, s4d, s4d, gates_t)
    return out.reshape(MLSTM_HEADS, batch * seq, SLAB)


MERGE_TM = 512
OUT_TM = 512


def _weight_spec():
    return pl.BlockSpec((D_MODEL, D_MODEL), lambda i: (0, 0), pipeline_mode=pl.Buffered(1))


def _branch_merge_kernel(attn_ref, az_ref, cell_ref, mo_ref, mz_ref, ga_ref, gm_ref,
                         wa_ref, wm_ref, ng_ref, merged_ref):
    n = GROUP_SLABS
    hd = ATTN_HEAD_DIM
    gated = []
    for c in range(n):
        attn = jnp.concatenate([attn_ref[2 * c].astype(F32), attn_ref[2 * c + 1].astype(F32)], axis=1)
        gated.append(attn * _silu(az_ref[c].astype(F32)))
    gated = jnp.concatenate(gated, axis=1).astype(BF16)
    y_a, mem = [], []
    for h in range(MLSTM_HEADS):
        cols = slice(h * SLAB, (h + 1) * SLAB)
        y_a.append(jnp.dot(gated, wa_ref[:, cols], preferred_element_type=F32))
        cell = _sigmoid(mo_ref[h].astype(F32)) * cell_ref[h].astype(F32)
        cell = cell * lax.rsqrt(jnp.mean(cell * cell, axis=-1, keepdims=True) + NORM_EPS)
        cell = cell * ng_ref[:, cols]
        mem.append(cell * _silu(mz_ref[h].astype(F32)))
    mem = jnp.concatenate(mem, axis=1).astype(BF16)
    for c in range(n):
        cols = slice(c * SLAB, (c + 1) * SLAB)
        y_m = jnp.dot(mem, wm_ref[:, cols], preferred_element_type=F32)
        merged = _sigmoid(ga_ref[c].astype(F32)) * y_a[c] + _sigmoid(gm_ref[c].astype(F32)) * y_m
        merged_ref[:, cols] = merged.astype(merged_ref.dtype)


def _branch_merge(attn, slabs, cell, w_a, w_m, norm_g):
    t = attn.shape[1]
    assert MLSTM_V_DIM == SLAB and MLSTM_HEADS == GROUP_SLABS

    def slab_spec(group):
        return pl.BlockSpec((GROUP_SLABS, MERGE_TM, SLAB), lambda i: (group, i, 0))

    return pl.pallas_call(
        _branch_merge_kernel,
        grid=(t // MERGE_TM,),
        in_specs=[
            pl.BlockSpec((ATTN_HEADS, MERGE_TM, ATTN_HEAD_DIM), lambda i: (0, i, 0)),
            slab_spec(S_AZ), slab_spec(0), slab_spec(S_MO), slab_spec(S_MZ),
            slab_spec(S_GA), slab_spec(S_GM),
            _weight_spec(), _weight_spec(),
            pl.BlockSpec((1, D_MODEL), lambda i: (0, 0)),
        ],
        out_specs=pl.BlockSpec((MERGE_TM, D_MODEL), lambda i: (i, 0)),
        out_shape=jax.ShapeDtypeStruct((t, D_MODEL), BF16),
        compiler_params=_cparams(("parallel",)),
        name="branch_merge",
    )(attn, slabs, cell, slabs, slabs, slabs, slabs, w_a, w_m, norm_g)


def _out_proj_kernel(merged_ref, x_ref, wo_ref, fg_ref, out_ref):
    y = x_ref[...] + jnp.dot(merged_ref[...], wo_ref[...], preferred_element_type=F32)
    y = y * lax.rsqrt(jnp.mean(y * y, axis=-1, keepdims=True) + NORM_EPS)
    out_ref[...] = y * fg_ref[...]


def _out_proj(merged, x2, w_o, final_g):
    t = x2.shape[0]
    return pl.pallas_call(
        _out_proj_kernel,
        grid=(t // OUT_TM,),
        in_specs=[
            pl.BlockSpec((OUT_TM, D_MODEL), lambda i: (i, 0)),
            pl.BlockSpec((OUT_TM, D_MODEL), lambda i: (i, 0)),
            _weight_spec(),
            pl.BlockSpec((1, D_MODEL), lambda i: (0, 0)),
        ],
        out_specs=pl.BlockSpec((OUT_TM, D_MODEL), lambda i: (i, 0)),
        out_shape=jax.ShapeDtypeStruct((t, D_MODEL), F32),
        compiler_params=_cparams(("parallel",)),
        name="out_proj",
    )(merged, x2, w_o, final_g)


def kernel(x, norm_g, w_in, b_if, conv_w, conv_b, mlstm_norm_g, w_attn_branch, w_mlstm_branch,
           w_out, final_norm_g):
    batch, seq, d = x.shape
    assert d == D_MODEL and seq % (16 * BAND) == 0 and (batch * seq) % IN_TM == 0
    t = batch * seq
    x2 = x.reshape(t, d)

    gate_lo = LO_STEPS * IN_TN
    gate_hi = gate_lo + 2 * MLSTM_HEADS
    w_bf = w_in.astype(BF16)
    w_hi = w_bf[:, gate_hi:]
    w_gate = jnp.pad(w_bf[:, gate_lo:gate_hi], ((0, 0), (0, GATE_LANES - 2 * MLSTM_HEADS)))
    assert gate_lo + w_hi.shape[1] == QKV_COLS + N_SLAB_GROUPS * D_MODEL

    qkv, slabs, gates_t = _in_proj(x2, norm_g.reshape(1, d), w_bf, w_hi, w_gate, conv_w,
                                   conv_b.reshape(1, -1), seq)
    attn = _attention(qkv, batch, seq)
    cell = _mlstm(slabs, gates_t, b_if, batch, seq)
    merged = _branch_merge(attn, slabs, cell, w_attn_branch.astype(BF16), w_mlstm_branch.astype(BF16),
                           mlstm_norm_g.reshape(1, d))
    out = _out_proj(merged, x2, w_out.astype(BF16), final_norm_g.reshape(1, d))
    return out.reshape(batch, seq, d)
```

```python
import functools

import jax
import jax.numpy as jnp
from jax import lax
from jax.experimental import pallas as pl
from jax.experimental.pallas import tpu as pltpu

F32 = jnp.float32
BF16 = jnp.bfloat16

D_MODEL = 2048
ATTN_HEADS = 16
ATTN_HEAD_DIM = 128
MLSTM_HEADS = 8
MLSTM_QK_DIM = 128
MLSTM_V_DIM = 256
CONV_WIDTH = 4
NORM_EPS = 1e-6
BAND = 128
ATTN_ITERS = 2
SLAB = 256
GROUP_SLABS = D_MODEL // SLAB
QKV_COLS = 3 * D_MODEL
S_AZ, S_MQK, S_MV, S_MO, S_MZ, S_GA, S_GM = range(7)
N_SLAB_GROUPS = 7
GATE_LANES = 128
MASKED = -1e30
CHUNK = 128
VMEM_LIMIT = 56 * 1024 * 1024


def _cparams(sem):
    return pltpu.CompilerParams(dimension_semantics=sem, vmem_limit_bytes=VMEM_LIMIT)


LOG2E = 1.4426950408889634
NEG_LOG2E = -LOG2E


def _sigmoid(x):
    return 1.0 / (1.0 + jnp.exp2(x * NEG_LOG2E))


def _silu(x):
    half = 0.5 * x
    return half + half * jnp.tanh(half)


def _log_sigmoid(x):
    return jnp.minimum(x, 0.0) - jnp.log(1.0 + jnp.exp(-jnp.abs(x)))


IN_TM = 1024
IN_TN = 1024
QKV_STEPS = QKV_COLS // IN_TN


CONV_STEP0 = QKV_STEPS + S_MQK * (D_MODEL // IN_TN)
CONV_STEPS = D_MODEL // IN_TN
CONV_HALO = 8
LO_STEPS = (4 * D_MODEL + 2 * MLSTM_HEADS * MLSTM_QK_DIM + MLSTM_HEADS * MLSTM_V_DIM) // IN_TN


def _in_proj_kernel(x_ref, g_ref, wlo_ref, whi_ref, wg_ref, cw_ref, cb_ref, qkv_ref, p_ref, gt_ref,
                    hn_ref, halo_ref, acc_ref, *, tiles_per_seq):
    i = pl.program_id(0)
    n = pl.program_id(1)

    @pl.when((i == 0) & (n == 0))
    def _():
        halo_ref[...] = jnp.zeros_like(halo_ref)

    @pl.when(n == 0)
    def _():
        x = x_ref[...]
        ms = jnp.mean(x * x, axis=-1, keepdims=True)
        hn = (x * lax.rsqrt(ms + NORM_EPS) * g_ref[...]).astype(BF16)
        hn_ref[...] = hn
        gates = jnp.dot(hn, wg_ref[...], preferred_element_type=F32)
        gates_t = gates.T
        for c in range(IN_TM // CHUNK):
            gt_ref[c] = gates_t[:2 * MLSTM_HEADS, c * CHUNK:(c + 1) * CHUNK]

    def slab_acc(j, w_ref=wlo_ref):
        return jnp.dot(hn_ref[...], w_ref[:, j * SLAB:(j + 1) * SLAB], preferred_element_type=F32)

    @pl.when(n < QKV_STEPS)
    def _():
        scale = jnp.where(n < D_MODEL // IN_TN, ATTN_HEAD_DIM ** -0.5 * LOG2E, 1.0).astype(F32)
        for j in range(IN_TN // SLAB):
            acc = slab_acc(j) * scale
            qkv_ref[2 * j] = acc[:, :ATTN_HEAD_DIM]
            qkv_ref[2 * j + 1] = acc[:, ATTN_HEAD_DIM:]

    is_conv = (n >= CONV_STEP0) & (n < CONV_STEP0 + CONV_STEPS)

    @pl.when((n >= QKV_STEPS) & (n < LO_STEPS) & jnp.logical_not(is_conv))
    def _():
        for j in range(IN_TN // SLAB):
            p_ref[j] = slab_acc(j).astype(BF16)

    @pl.when(n >= LO_STEPS)
    def _():
        for j in range(IN_TN // SLAB):
            p_ref[j] = slab_acc(j, whi_ref).astype(BF16)

    @pl.when(is_conv)
    def _():
        step = n - CONV_STEP0
        seq_start = (i % tiles_per_seq) == 0
        post = jnp.where(step == 1, MLSTM_QK_DIM ** -0.5, 1.0).astype(F32)
        def conv_slab(j):
            cols = slice(j * SLAB, (j + 1) * SLAB)
            slot = (n + j) % 2
            acc_ref[slot, 0:CONV_HALO, :] = jnp.where(seq_start, 0.0, halo_ref[step, :, cols])
            halo_ref[step, :, cols] = acc_ref[slot, IN_TM:IN_TM + CONV_HALO, :]
            y = cb_ref[:, cols]
            for back in range(CONV_WIDTH):
                tap = acc_ref[slot, CONV_HALO - back:CONV_HALO - back + IN_TM, :]
                y = y + cw_ref[CONV_WIDTH - 1 - back:CONV_WIDTH - back, cols] * tap
            p_ref[j] = (_silu(y) * post).astype(BF16)

        n_slabs = IN_TN // SLAB
        for j in range(n_slabs):
            acc_ref[(n + j) % 2, CONV_HALO:, :] = slab_acc(j)
            if j > 0:
                conv_slab(j - 1)
        conv_slab(n_slabs - 1)


def _in_proj(x2, norm_g, w_lo, w_hi, w_gate, conv_w, conv_b, seq):
    t = x2.shape[0]
    n_cols = LO_STEPS * IN_TN + w_hi.shape[1]
    grid = (t // IN_TM, n_cols // IN_TN)
    assert seq % IN_TM == 0 and CONV_STEPS * IN_TN == conv_w.shape[1]
    assert w_lo.shape[1] >= LO_STEPS * IN_TN and w_hi.shape[1] % IN_TN == 0

    def conv_step(i, n):
        return (0, jnp.clip(n - CONV_STEP0, 0, CONV_STEPS - 1))

    return pl.pallas_call(
        functools.partial(_in_proj_kernel, tiles_per_seq=seq // IN_TM),
        grid=grid,
        in_specs=[
            pl.BlockSpec((IN_TM, D_MODEL), lambda i, n: (i, 0)),
            pl.BlockSpec((1, D_MODEL), lambda i, n: (0, 0)),
            pl.BlockSpec((D_MODEL, IN_TN), lambda i, n: (0, jnp.minimum(n, LO_STEPS - 1))),
            pl.BlockSpec((D_MODEL, IN_TN), lambda i, n: (0, jnp.maximum(n - LO_STEPS, 0))),
            pl.BlockSpec((D_MODEL, GATE_LANES), lambda i, n: (0, 0)),
            pl.BlockSpec((CONV_WIDTH, IN_TN), conv_step),
            pl.BlockSpec((1, IN_TN), conv_step),
        ],
        out_specs=[
            pl.BlockSpec((IN_TN // ATTN_HEAD_DIM, IN_TM, ATTN_HEAD_DIM),
                         lambda i, n: (jnp.minimum(n, QKV_STEPS - 1), i, 0)),
            pl.BlockSpec((IN_TN // SLAB, IN_TM, SLAB),
                         lambda i, n: (jnp.maximum(n - QKV_STEPS, 0), i, 0)),
            pl.BlockSpec((IN_TM // CHUNK, 2 * MLSTM_HEADS, CHUNK), lambda i, n: (i, 0, 0)),
        ],
        out_shape=[
            jax.ShapeDtypeStruct((QKV_COLS // ATTN_HEAD_DIM, t, ATTN_HEAD_DIM), F32),
            jax.ShapeDtypeStruct(((n_cols - QKV_COLS) // SLAB, t, SLAB), BF16),
            jax.ShapeDtypeStruct((t // CHUNK, 2 * MLSTM_HEADS, CHUNK), F32),
        ],
        scratch_shapes=[pltpu.VMEM((IN_TM, D_MODEL), BF16),
                        pltpu.VMEM((CONV_STEPS, CONV_HALO, IN_TN), F32),
                        pltpu.VMEM((2, CONV_HALO + IN_TM, SLAB), F32)],
        compiler_params=_cparams(("arbitrary", "arbitrary")),
        name="in_proj",
    )(x2, norm_g, w_lo, w_hi, w_gate, conv_w, conv_b)


def _band_bias(slope, dilation, first):
    qi = lax.broadcasted_iota(jnp.int32, (BAND, 2 * BAND), 0)
    ki = lax.broadcasted_iota(jnp.int32, (BAND, 2 * BAND), 1)
    dist = BAND + qi - ki
    valid = (dist >= 0) & (dist <= BAND)
    if first:
        valid = valid & (ki >= BAND)
    return jnp.where(valid, -slope * (dist * dilation).astype(F32), MASKED)


def _dense_bias(slope, first):
    qi = lax.broadcasted_iota(jnp.int32, (BAND, 2 * BAND), 0)
    ki = lax.broadcasted_iota(jnp.int32, (BAND, 2 * BAND), 1)
    pq, jq = qi // 32, qi % 32
    half, pk, jk = ki // BAND, (ki % BAND) // 32, ki % 32
    dist = 4 * (jq - jk + 32 * (1 - half)) + pq - pk
    valid = (dist >= 0) & (dist <= BAND)
    if first:
        valid = valid & (half == 1)
    return jnp.where(valid, -slope * dist.astype(F32), MASKED)


def _wide_bias(slope):
    qi = lax.broadcasted_iota(jnp.int32, (2 * BAND, 2 * BAND), 0)
    ki = lax.broadcasted_iota(jnp.int32, (2 * BAND, 2 * BAND), 1)
    dist = qi - ki
    valid = (dist >= 0) & (dist <= BAND)
    return jnp.where(valid, -slope * (dist * 16).astype(F32), MASKED)


def _softmax_block(q, k, v, bias):
    s = lax.dot_general(q, k, (((1,), (1,)), ((), ())), preferred_element_type=F32) + bias
    m = jnp.max(s, axis=-1, keepdims=True)
    p = jnp.exp2(s - m).astype(BF16)
    v_ext = jnp.concatenate([v, jnp.ones_like(v)], axis=1)
    r = jnp.dot(p, v_ext, preferred_element_type=F32)
    hd = ATTN_HEAD_DIM
    return r[:, :hd], r[:, hd:], jnp.broadcast_to(m, (q.shape[0], hd))


def _attn_kernel(q_ref, k_ref, v_ref, o_ref,
                 x4_ref, x16_ref, tmp_ref, num_ref, den_ref, mx_ref, bias_ref, bias16_ref, onat_ref):
    seq = q_ref.shape[0]
    n_u4, n_u16 = seq // 4, seq // 16
    slope = jnp.exp2(jnp.full((1, 1), -8.0 / ATTN_HEADS, F32) * (pl.program_id(1) + 1)) * LOG2E
    bias_ref[0, 0] = _dense_bias(slope, False)
    bias_ref[0, 1] = _dense_bias(slope, True)
    bias_ref[1, 0] = _band_bias(slope, 4, False)
    bias_ref[1, 1] = _band_bias(slope, 4, True)
    bias16_ref[...] = _wide_bias(slope)

    for ti, src in enumerate((q_ref, k_ref, v_ref)):
        def to_phase4(piece, carry, ti=ti, src=src):
            dst = pl.ds(pl.multiple_of(piece * BAND, BAND), BAND)
            for p4 in range(4):
                rows = src[pl.ds(p4 + 4 * BAND * piece, BAND, stride=4), :]
                tmp_ref[p4, dst, :] = rows
                x4_ref[ti, p4, dst, :] = rows.astype(BF16)
            return carry

        lax.fori_loop(0, n_u4 // BAND, to_phase4, 0)

        def to_phase16(piece, carry, ti=ti):
            dst = pl.ds(pl.multiple_of(piece * BAND, BAND), BAND)
            for p16 in range(16):
                rows = tmp_ref[p16 % 4, pl.ds(p16 // 4 + 4 * BAND * piece, BAND, stride=4), :]
                x16_ref[ti, p16, dst, :] = rows.astype(BF16)
            return carry

        lax.fori_loop(0, n_u16 // BAND, to_phase16, 0)

    def dense_block(n):
        cur = pl.multiple_of(n * 32, 32)
        prev = pl.multiple_of(jnp.maximum(n - 1, 0) * 32, 32)
        first = jnp.where(n == 0, 1, 0)

        def rows(ti, start):
            return [x4_ref[ti, p, pl.ds(start, 32), :] for p in range(4)]

        q = jnp.concatenate(rows(0, cur), axis=0)
        k = jnp.concatenate(rows(1, prev) + rows(1, cur), axis=0)
        v = jnp.concatenate(rows(2, prev) + rows(2, cur), axis=0)
        num, den, mx = _softmax_block(q, k, v, bias_ref[0, first])
        for p in range(4):
            part = slice(p * 32, (p + 1) * 32)
            num_ref[0, p, pl.ds(cur, 32), :] = num[part]
            den_ref[0, p, pl.ds(cur, 32), :] = den[part]
            mx_ref[0, p, pl.ds(cur, 32), :] = mx[part]

    def band_block(p, n):
        cur = pl.multiple_of(n * BAND, BAND)
        prev = pl.multiple_of(jnp.maximum(n - 1, 0) * BAND, BAND)
        first = jnp.where(n == 0, 1, 0)
        q = x4_ref[0, p, pl.ds(cur, BAND), :]
        k = jnp.concatenate([x4_ref[1, p, pl.ds(prev, BAND), :], x4_ref[1, p, pl.ds(cur, BAND), :]], axis=0)
        v = jnp.concatenate([x4_ref[2, p, pl.ds(prev, BAND), :], x4_ref[2, p, pl.ds(cur, BAND), :]], axis=0)
        num, den, mx = _softmax_block(q, k, v, bias_ref[1, first])
        num_ref[1, p, pl.ds(cur, BAND), :] = num
        den_ref[1, p, pl.ds(cur, BAND), :] = den
        mx_ref[1, p, pl.ds(cur, BAND), :] = mx

    def wide_block(p4, a):
        p16 = 4 * a + p4
        num, den, mx = _softmax_block(x16_ref[0, p16], x16_ref[1, p16], x16_ref[2, p16], bias16_ref[...])
        rows = pl.ds(a, n_u16, stride=4)
        num_ref[2, p4, rows, :] = num
        den_ref[2, p4, rows, :] = den
        mx_ref[2, p4, rows, :] = mx

    blocks4 = n_u4 // BAND
    dense_per_it = n_u4 // 32 // ATTN_ITERS
    band_per_it = blocks4 // ATTN_ITERS
    wide_per_it = 4 // ATTN_ITERS

    def pattern_body(it, carry):
        for u in range(dense_per_it):
            dense_block(it * dense_per_it + u)
        for u in range(band_per_it):
            for p in range(4):
                band_block(p, it * band_per_it + u)
        for u in range(wide_per_it):
            for p4 in range(4):
                wide_block(p4, it * wide_per_it + u)
        return carry

    lax.fori_loop(0, ATTN_ITERS, pattern_body, 0)

    def merge_body(it, carry):
        p, n = it // blocks4, it % blocks4
        rows = pl.ds(pl.multiple_of(n * BAND, BAND), BAND)
        m0, m1, m2 = mx_ref[0, p, rows, :], mx_ref[1, p, rows, :], mx_ref[2, p, rows, :]
        m_all = jnp.maximum(jnp.maximum(m0, m1), m2)
        w0, w1, w2 = jnp.exp2(m0 - m_all), jnp.exp2(m1 - m_all), jnp.exp2(m2 - m_all)
        num = w0 * num_ref[0, p, rows, :] + w1 * num_ref[1, p, rows, :] + w2 * num_ref[2, p, rows, :]
        den = w0 * den_ref[0, p, rows, :] + w1 * den_ref[1, p, rows, :] + w2 * den_ref[2, p, rows, :]
        onat_ref[pl.ds(p + 4 * BAND * n, BAND, stride=4), :] = num / den
        return carry

    lax.fori_loop(0, 4 * blocks4, merge_body, 0, unroll=2)
    o_ref[...] = onat_ref[...].astype(o_ref.dtype)


def _attention(qkv, batch, seq):
    assert seq // 16 == 2 * BAND
    hd = ATTN_HEAD_DIM
    qkv4 = qkv.reshape(3 * ATTN_HEADS, batch, seq, hd)

    def spec(which):
        return pl.BlockSpec((None, None, seq, hd), lambda b, h: (which * ATTN_HEADS + h, b, 0, 0))

    out = pl.pallas_call(
        _attn_kernel,
        grid=(batch, ATTN_HEADS),
        in_specs=[spec(0), spec(1), spec(2)],
        out_specs=pl.BlockSpec((None, None, seq, hd), lambda b, h: (h, b, 0, 0)),
        out_shape=jax.ShapeDtypeStruct((ATTN_HEADS, batch, seq, hd), BF16),
        scratch_shapes=[
            pltpu.VMEM((3, 4, seq // 4, hd), BF16),
            pltpu.VMEM((3, 16, seq // 16, hd), BF16),
            pltpu.VMEM((4, seq // 4, hd), F32),
            pltpu.VMEM((3, 4, seq // 4, hd), F32),
            pltpu.VMEM((3, 4, seq // 4, hd), F32),
            pltpu.VMEM((3, 4, seq // 4, hd), F32),
            pltpu.VMEM((2, 2, BAND, 2 * BAND), F32),
            pltpu.VMEM((2 * BAND, 2 * BAND), F32),
            pltpu.VMEM((seq, hd), F32),
        ],
        compiler_params=_cparams(("parallel", "parallel")),
        name="dilated_attention",
    )(qkv4, qkv4, qkv4)
    return out.reshape(ATTN_HEADS, batch * seq, hd)


MLSTM_TS = 512
ONES_LANES = 128


def _mlstm_kernel(bias_ref, q_ref, k_ref, v_ref, gt_ref, out_ref, c_ref, m_ref):
    s = pl.program_id(1)
    dk, dv = MLSTM_QK_DIM, MLSTM_V_DIM
    L = CHUNK

    @pl.when(s == 0)
    def _():
        c_ref[...] = jnp.zeros_like(c_ref)
        m_ref[...] = jnp.zeros_like(m_ref)

    ti = lax.broadcasted_iota(jnp.int32, (CHUNK, CHUNK), 0)
    si = lax.broadcasted_iota(jnp.int32, (CHUNK, CHUNK), 1)
    causal = ti >= si
    csum = jnp.where(ti <= si, 1.0, 0.0).astype(F32)

    lane = lax.broadcasted_iota(jnp.int32, (MLSTM_HEADS, L), 1)

    def chunk_body(c, carry):
        start = pl.multiple_of(c * L, L)
        gates = gt_ref[c] + bias_ref[...]
        ig_all = gates[:MLSTM_HEADS]
        lf_all = _log_sigmoid(gates[MLSTM_HEADS:])
        b_all = jnp.dot(lf_all, csum, preferred_element_type=F32,
                        precision=lax.Precision.HIGHEST)
        c_all = ig_all - b_all
        cm_all = c_all
        shift = 1
        while shift < L:
            cm_all = jnp.maximum(cm_all, jnp.where(lane >= shift, pltpu.roll(cm_all, shift, axis=1), MASKED))
            shift *= 2
        m_prev = m_ref[...]
        b_last = b_all[:, L - 1:L]
        top = jnp.maximum(m_prev, cm_all)
        m_new = b_last + top[:, L - 1:L]
        w_old = jnp.exp(b_last + m_prev - m_new)
        cols = jnp.concatenate([
            -top,
            jnp.exp(m_prev - top),
            jnp.exp(-(b_all + top)),
            jnp.exp(b_last + c_all - m_new),
        ], axis=0).T
        m_ref[...] = m_new

        def column(group, h):
            j = group * MLSTM_HEADS + h
            return jnp.broadcast_to(cols[:, j:j + 1], (L, L))

        def q_of(h):
            return q_ref[h // 2, pl.ds(start, L), (h % 2) * dk:(h % 2 + 1) * dk]

        def v_ext_of(h):
            return jnp.concatenate([v_ref[h, pl.ds(start, L), :], jnp.ones((L, ONES_LANES), BF16)], axis=1)

        def k_of(h):
            return k_ref[h // 2, pl.ds(start, L), (h % 2) * dk:(h % 2 + 1) * dk]

        nt = (((1,), (1,)), ((), ()))
        heads = range(MLSTM_HEADS)
        scores = [lax.dot_general(q_of(h), k_of(h), nt, preferred_element_type=F32) for h in heads]
        incs = [jnp.dot((k_of(h).astype(F32) * column(3, h)).T.astype(BF16), v_ext_of(h),
                        preferred_element_type=F32) for h in heads]
        nds = []
        for h in heads:
            w_intra = jnp.exp(jnp.where(causal, column(0, h) + c_all[h:h + 1], MASKED))
            lhs = jnp.concatenate([(scores[h] * w_intra).astype(BF16),
                                   (q_of(h).astype(F32) * column(1, h)).astype(BF16)], axis=1)
            rhs = jnp.concatenate([v_ext_of(h), c_ref[h].astype(BF16)], axis=0)
            nds.append(jnp.dot(lhs, rhs, preferred_element_type=F32))
        for h in heads:
            nd = nds[h]
            inv = 1.0 / jnp.maximum(jnp.abs(nd[:, dv:]), column(2, h))
            for half in range(dv // L):
                lanes = slice(half * L, (half + 1) * L)
                out_ref[h, pl.ds(start, L), lanes] = (nd[:, lanes] * inv).astype(out_ref.dtype)
            c_ref[h] = w_old[h:h + 1] * c_ref[h] + incs[h]
        return carry

    lax.fori_loop(0, MLSTM_TS // CHUNK, chunk_body, 0, unroll=2)


def _mlstm(slabs, gates_t, b_if, batch, seq):
    n_slabs = slabs.shape[0]
    s4d = slabs.reshape(n_slabs, batch, seq, SLAB)
    half = MLSTM_HEADS // 2
    tiles = seq // MLSTM_TS
    tile_chunks = MLSTM_TS // CHUNK

    def tile_spec(first_slab, n):
        return pl.BlockSpec((n, None, MLSTM_TS, SLAB), lambda b, s: (first_slab // n, b, s, 0))

    out = pl.pallas_call(
        _mlstm_kernel,
        grid=(batch, tiles),
        in_specs=[
            pl.BlockSpec((2 * MLSTM_HEADS, 1), lambda b, s: (0, 0)),
            tile_spec(S_MQK * GROUP_SLABS, half), tile_spec(S_MQK * GROUP_SLABS + half, half),
            tile_spec(S_MV * GROUP_SLABS, MLSTM_HEADS),
            pl.BlockSpec((tile_chunks, 2 * MLSTM_HEADS, CHUNK), lambda b, s: (b * tiles + s, 0, 0)),
        ],
        out_specs=pl.BlockSpec((MLSTM_HEADS, None, MLSTM_TS, SLAB), lambda b, s: (0, b, s, 0)),
        out_shape=jax.ShapeDtypeStruct((MLSTM_HEADS, batch, seq, SLAB), BF16),
        scratch_shapes=[
            pltpu.VMEM((MLSTM_HEADS, MLSTM_QK_DIM, MLSTM_V_DIM + ONES_LANES), F32),
            pltpu.VMEM((MLSTM_HEADS, 1), F32),
        ],
        compiler_params=_cparams(("parallel", "arbitrary")),
        name="mlstm",
    )(b_if.reshape(-1, 1), s4d, s4d, s4d, gates_t)
    return out.reshape(MLSTM_HEADS, batch * seq, SLAB)


MERGE_TM = 512
OUT_TM = 512


def _weight_spec():
    return pl.BlockSpec((D_MODEL, D_MODEL), lambda i: (0, 0), pipeline_mode=pl.Buffered(1))


def _branch_merge_kernel(attn_ref, az_ref, cell_ref, mo_ref, mz_ref, ga_ref, gm_ref,
                         wa_ref, wm_ref, ng_ref, merged_ref):
    n = GROUP_SLABS
    hd = ATTN_HEAD_DIM
    gated = []
    for c in range(n):
        attn = jnp.concatenate([attn_ref[2 * c].astype(F32), attn_ref[2 * c + 1].astype(F32)], axis=1)
        gated.append(attn * _silu(az_ref[c].astype(F32)))
    gated = jnp.concatenate(gated, axis=1).astype(BF16)
    y_a, mem = [], []
    for h in range(MLSTM_HEADS):
        cols = slice(h * SLAB, (h + 1) * SLAB)
        y_a.append(jnp.dot(gated, wa_ref[:, cols], preferred_element_type=F32))
        cell = _sigmoid(mo_ref[h].astype(F32)) * cell_ref[h].astype(F32)
        cell = cell * lax.rsqrt(jnp.mean(cell * cell, axis=-1, keepdims=True) + NORM_EPS)
        cell = cell * ng_ref[:, cols]
        mem.append(cell * _silu(mz_ref[h].astype(F32)))
    mem = jnp.concatenate(mem, axis=1).astype(BF16)
    for c in range(n):
        cols = slice(c * SLAB, (c + 1) * SLAB)
        y_m = jnp.dot(mem, wm_ref[:, cols], preferred_element_type=F32)
        merged = _sigmoid(ga_ref[c].astype(F32)) * y_a[c] + _sigmoid(gm_ref[c].astype(F32)) * y_m
        merged_ref[:, cols] = merged.astype(merged_ref.dtype)


def _branch_merge(attn, slabs, cell, w_a, w_m, norm_g):
    t = attn.shape[1]
    assert MLSTM_V_DIM == SLAB and MLSTM_HEADS == GROUP_SLABS

    def slab_spec(group):
        return pl.BlockSpec((GROUP_SLABS, MERGE_TM, SLAB), lambda i: (group, i, 0))

    return pl.pallas_call(
        _branch_merge_kernel,
        grid=(t // MERGE_TM,),
        in_specs=[
            pl.BlockSpec((ATTN_HEADS, MERGE_TM, ATTN_HEAD_DIM), lambda i: (0, i, 0)),
            slab_spec(S_AZ), slab_spec(0), slab_spec(S_MO), slab_spec(S_MZ),
            slab_spec(S_GA), slab_spec(S_GM),
            _weight_spec(), _weight_spec(),
            pl.BlockSpec((1, D_MODEL), lambda i: (0, 0)),
        ],
        out_specs=pl.BlockSpec((MERGE_TM, D_MODEL), lambda i: (i, 0)),
        out_shape=jax.ShapeDtypeStruct((t, D_MODEL), BF16),
        compiler_params=_cparams(("parallel",)),
        name="branch_merge",
    )(attn, slabs, cell, slabs, slabs, slabs, slabs, w_a, w_m, norm_g)


def _out_proj_kernel(merged_ref, x_ref, wo_ref, fg_ref, out_ref):
    y = x_ref[...] + jnp.dot(merged_ref[...], wo_ref[...], preferred_element_type=F32)
    y = y * lax.rsqrt(jnp.mean(y * y, axis=-1, keepdims=True) + NORM_EPS)
    out_ref[...] = y * fg_ref[...]


def _out_proj(merged, x2, w_o, final_g):
    t = x2.shape[0]
    return pl.pallas_call(
        _out_proj_kernel,
        grid=(t // OUT_TM,),
        in_specs=[
            pl.BlockSpec((OUT_TM, D_MODEL), lambda i: (i, 0)),
            pl.BlockSpec((OUT_TM, D_MODEL), lambda i: (i, 0)),
            _weight_spec(),
            pl.BlockSpec((1, D_MODEL), lambda i: (0, 0)),
        ],
        out_specs=pl.BlockSpec((OUT_TM, D_MODEL), lambda i: (i, 0)),
        out_shape=jax.ShapeDtypeStruct((t, D_MODEL), F32),
        compiler_params=_cparams(("parallel",)),
        name="out_proj",
    )(merged, x2, w_o, final_g)


def kernel(x, norm_g, w_in, b_if, conv_w, conv_b, mlstm_norm_g, w_attn_branch, w_mlstm_branch,
           w_out, final_norm_g):
    batch, seq, d = x.shape
    assert d == D_MODEL and seq % (16 * BAND) == 0 and (batch * seq) % IN_TM == 0
    t = batch * seq
    x2 = x.reshape(t, d)

    gate_lo = LO_STEPS * IN_TN
    gate_hi = gate_lo + 2 * MLSTM_HEADS
    w_bf = w_in.astype(BF16)
    w_hi = w_bf[:, gate_hi:]
    w_gate = jnp.pad(w_bf[:, gate_lo:gate_hi], ((0, 0), (0, GATE_LANES - 2 * MLSTM_HEADS)))
    assert gate_lo + w_hi.shape[1] == QKV_COLS + N_SLAB_GROUPS * D_MODEL

    qkv, slabs, gates_t = _in_proj(x2, norm_g.reshape(1, d), w_bf, w_hi, w_gate, conv_w,
                                   conv_b.reshape(1, -1), seq)
    attn = _attention(qkv, batch, seq)
    cell = _mlstm(slabs, gates_t, b_if, batch, seq)
    merged = _branch_merge(attn, slabs, cell, w_attn_branch.astype(BF16), w_mlstm_branch.astype(BF16),
                           mlstm_norm_g.reshape(1, d))
    out = _out_proj(merged, x2, w_out.astype(BF16), final_norm_g.reshape(1, d))
    return out.reshape(batch, seq, d)
```

```python
import functools

import jax
import jax.numpy as jnp
from jax import lax
from jax.experimental import pallas as pl
from jax.experimental.pallas import tpu as pltpu

F32 = jnp.float32
BF16 = jnp.bfloat16

D_MODEL = 2048
ATTN_HEADS = 16
ATTN_HEAD_DIM = 128
MLSTM_HEADS = 8
MLSTM_QK_DIM = 128
MLSTM_V_DIM = 256
CONV_WIDTH = 4
NORM_EPS = 1e-6
BAND = 128
ATTN_ITERS = 2
SLAB = 256
GROUP_SLABS = D_MODEL // SLAB
QKV_COLS = 3 * D_MODEL
S_AZ, S_MQK, S_MV, S_MO, S_MZ, S_GA, S_GM = range(7)
N_SLAB_GROUPS = 7
GATE_LANES = 128
MASKED = -1e30
CHUNK = 128
VMEM_LIMIT = 56 * 1024 * 1024


def _cparams(sem):
    return pltpu.CompilerParams(dimension_semantics=sem, vmem_limit_bytes=VMEM_LIMIT)


LOG2E = 1.4426950408889634
NEG_LOG2E = -LOG2E


def _sigmoid(x):
    return 1.0 / (1.0 + jnp.exp2(x * NEG_LOG2E))


def _silu(x):
    half = 0.5 * x
    return half + half * jnp.tanh(half)


def _log_sigmoid(x):
    return jnp.minimum(x, 0.0) - jnp.log(1.0 + jnp.exp(-jnp.abs(x)))


PREP_TN = 1024
GATE_COLS = 2 * MLSTM_HEADS
GATE_COL0 = 4 * D_MODEL + 2 * MLSTM_HEADS * MLSTM_QK_DIM + MLSTM_HEADS * MLSTM_V_DIM
GATE_STEP = GATE_COL0 // PREP_TN
LANES = 128


def _weight_prep_kernel(a_ref, b_ref, w_ref, wg_ref):
    n = pl.program_id(0)

    @pl.when(n < GATE_STEP)
    def _():
        w_ref[...] = a_ref[...].astype(BF16)

    @pl.when(n >= GATE_STEP)
    def _():
        shifted = jnp.concatenate([a_ref[:, GATE_COLS:], b_ref[:, :GATE_COLS]], axis=1)
        w_ref[...] = shifted.astype(BF16)

    @pl.when(n == GATE_STEP)
    def _():
        lane = lax.broadcasted_iota(jnp.int32, (D_MODEL, LANES), 1)
        wg_ref[...] = jnp.where(lane < GATE_COLS, a_ref[:, :LANES], 0.0).astype(BF16)


def _weight_prep(w_in):
    d, in_width = w_in.shape
    n_cols = in_width - GATE_COLS
    assert d == D_MODEL and GATE_COL0 % PREP_TN == 0 and n_cols % PREP_TN == 0 and GATE_LANES == LANES
    sub = PREP_TN // LANES
    return pl.pallas_call(
        _weight_prep_kernel,
        grid=(n_cols // PREP_TN,),
        in_specs=[
            pl.BlockSpec((D_MODEL, PREP_TN), lambda n: (0, n)),
            pl.BlockSpec((D_MODEL, LANES), lambda n: (0, (n + 1) * sub)),
        ],
        out_specs=[
            pl.BlockSpec((D_MODEL, PREP_TN), lambda n: (0, n)),
            pl.BlockSpec((D_MODEL, LANES), lambda n: (0, 0)),
        ],
        out_shape=[
            jax.ShapeDtypeStruct((D_MODEL, n_cols), BF16),
            jax.ShapeDtypeStruct((D_MODEL, LANES), BF16),
        ],
        compiler_params=_cparams(("arbitrary",)),
        name="weight_prep",
    )(w_in, w_in)


IN_TM = 1024
IN_TN = 1024
QKV_STEPS = QKV_COLS // IN_TN


CONV_STEP0 = QKV_STEPS + S_MQK * (D_MODEL // IN_TN)
CONV_STEPS = D_MODEL // IN_TN
CONV_HALO = 8


def _in_proj_kernel(x_ref, g_ref, w_ref, wg_ref, cw_ref, cb_ref, qkv_ref, p_ref, gt_ref,
                    hn_ref, halo_ref, acc_ref, *, tiles_per_seq):
    i = pl.program_id(0)
    n = pl.program_id(1)

    @pl.when((i == 0) & (n == 0))
    def _():
        halo_ref[...] = jnp.zeros_like(halo_ref)

    @pl.when(n == 0)
    def _():
        x = x_ref[...]
        ms = jnp.mean(x * x, axis=-1, keepdims=True)
        hn = (x * lax.rsqrt(ms + NORM_EPS) * g_ref[...]).astype(BF16)
        hn_ref[...] = hn
        gates = jnp.dot(hn, wg_ref[...], preferred_element_type=F32)
        gates_t = gates.T
        for c in range(IN_TM // CHUNK):
            gt_ref[c] = gates_t[:2 * MLSTM_HEADS, c * CHUNK:(c + 1) * CHUNK]

    def slab_acc(j):
        return jnp.dot(hn_ref[...], w_ref[:, j * SLAB:(j + 1) * SLAB], preferred_element_type=F32)

    @pl.when(n < QKV_STEPS)
    def _():
        scale = jnp.where(n < D_MODEL // IN_TN, ATTN_HEAD_DIM ** -0.5 * LOG2E, 1.0).astype(F32)
        for j in range(IN_TN // SLAB):
            acc = slab_acc(j) * scale
            qkv_ref[2 * j] = acc[:, :ATTN_HEAD_DIM]
            qkv_ref[2 * j + 1] = acc[:, ATTN_HEAD_DIM:]

    is_conv = (n >= CONV_STEP0) & (n < CONV_STEP0 + CONV_STEPS)

    @pl.when((n >= QKV_STEPS) & jnp.logical_not(is_conv))
    def _():
        for j in range(IN_TN // SLAB):
            p_ref[j] = slab_acc(j).astype(BF16)

    @pl.when(is_conv)
    def _():
        step = n - CONV_STEP0
        seq_start = (i % tiles_per_seq) == 0
        post = jnp.where(step == 1, MLSTM_QK_DIM ** -0.5, 1.0).astype(F32)
        def conv_slab(j):
            cols = slice(j * SLAB, (j + 1) * SLAB)
            slot = (n + j) % 2
            acc_ref[slot, 0:CONV_HALO, :] = jnp.where(seq_start, 0.0, halo_ref[step, :, cols])
            halo_ref[step, :, cols] = acc_ref[slot, IN_TM:IN_TM + CONV_HALO, :]
            y = cb_ref[:, cols]
            for back in range(CONV_WIDTH):
                tap = acc_ref[slot, CONV_HALO - back:CONV_HALO - back + IN_TM, :]
                y = y + cw_ref[CONV_WIDTH - 1 - back:CONV_WIDTH - back, cols] * tap
            p_ref[j] = (_silu(y) * post).astype(BF16)

        n_slabs = IN_TN // SLAB
        for j in range(n_slabs):
            acc_ref[(n + j) % 2, CONV_HALO:, :] = slab_acc(j)
            if j > 0:
                conv_slab(j - 1)
        conv_slab(n_slabs - 1)


def _in_proj(x2, norm_g, w_big, w_gate, conv_w, conv_b, seq):
    t = x2.shape[0]
    n_cols = w_big.shape[1]
    grid = (t // IN_TM, n_cols // IN_TN)
    assert seq % IN_TM == 0 and CONV_STEPS * IN_TN == conv_w.shape[1]

    def conv_step(i, n):
        return (0, jnp.clip(n - CONV_STEP0, 0, CONV_STEPS - 1))

    return pl.pallas_call(
        functools.partial(_in_proj_kernel, tiles_per_seq=seq // IN_TM),
        grid=grid,
        in_specs=[
            pl.BlockSpec((IN_TM, D_MODEL), lambda i, n: (i, 0)),
            pl.BlockSpec((1, D_MODEL), lambda i, n: (0, 0)),
            pl.BlockSpec((D_MODEL, IN_TN), lambda i, n: (0, n)),
            pl.BlockSpec((D_MODEL, GATE_LANES), lambda i, n: (0, 0)),
            pl.BlockSpec((CONV_WIDTH, IN_TN), conv_step),
            pl.BlockSpec((1, IN_TN), conv_step),
        ],
        out_specs=[
            pl.BlockSpec((IN_TN // ATTN_HEAD_DIM, IN_TM, ATTN_HEAD_DIM),
                         lambda i, n: (jnp.minimum(n, QKV_STEPS - 1), i, 0)),
            pl.BlockSpec((IN_TN // SLAB, IN_TM, SLAB),
                         lambda i, n: (jnp.maximum(n - QKV_STEPS, 0), i, 0)),
            pl.BlockSpec((IN_TM // CHUNK, 2 * MLSTM_HEADS, CHUNK), lambda i, n: (i, 0, 0)),
        ],
        out_shape=[
            jax.ShapeDtypeStruct((QKV_COLS // ATTN_HEAD_DIM, t, ATTN_HEAD_DIM), F32),
            jax.ShapeDtypeStruct(((n_cols - QKV_COLS) // SLAB, t, SLAB), BF16),
            jax.ShapeDtypeStruct((t // CHUNK, 2 * MLSTM_HEADS, CHUNK), F32),
        ],
        scratch_shapes=[pltpu.VMEM((IN_TM, D_MODEL), BF16),
                        pltpu.VMEM((CONV_STEPS, CONV_HALO, IN_TN), F32),
                        pltpu.VMEM((2, CONV_HALO + IN_TM, SLAB), F32)],
        compiler_params=_cparams(("arbitrary", "arbitrary")),
        name="in_proj",
    )(x2, norm_g, w_big, w_gate, conv_w, conv_b)


def _band_bias(slope, dilation, first):
    qi = lax.broadcasted_iota(jnp.int32, (BAND, 2 * BAND), 0)
    ki = lax.broadcasted_iota(jnp.int32, (BAND, 2 * BAND), 1)
    dist = BAND + qi - ki
    valid = (dist >= 0) & (dist <= BAND)
    if first:
        valid = valid & (ki >= BAND)
    return jnp.where(valid, -slope * (dist * dilation).astype(F32), MASKED)


def _dense_bias(slope, first):
    qi = lax.broadcasted_iota(jnp.int32, (BAND, 2 * BAND), 0)
    ki = lax.broadcasted_iota(jnp.int32, (BAND, 2 * BAND), 1)
    pq, jq = qi // 32, qi % 32
    half, pk, jk = ki // BAND, (ki % BAND) // 32, ki % 32
    dist = 4 * (jq - jk + 32 * (1 - half)) + pq - pk
    valid = (dist >= 0) & (dist <= BAND)
    if first:
        valid = valid & (half == 1)
    return jnp.where(valid, -slope * dist.astype(F32), MASKED)


def _wide_bias(slope):
    qi = lax.broadcasted_iota(jnp.int32, (2 * BAND, 2 * BAND), 0)
    ki = lax.broadcasted_iota(jnp.int32, (2 * BAND, 2 * BAND), 1)
    dist = qi - ki
    valid = (dist >= 0) & (dist <= BAND)
    return jnp.where(valid, -slope * (dist * 16).astype(F32), MASKED)


def _softmax_block(q, k, v, bias):
    s = lax.dot_general(q, k, (((1,), (1,)), ((), ())), preferred_element_type=F32) + bias
    m = jnp.max(s, axis=-1, keepdims=True)
    p = jnp.exp2(s - m).astype(BF16)
    v_ext = jnp.concatenate([v, jnp.ones_like(v)], axis=1)
    r = jnp.dot(p, v_ext, preferred_element_type=F32)
    hd = ATTN_HEAD_DIM
    return r[:, :hd], r[:, hd:], jnp.broadcast_to(m, (q.shape[0], hd))


def _attn_kernel(q_ref, k_ref, v_ref, o_ref,
                 x4_ref, x16_ref, tmp_ref, num_ref, den_ref, mx_ref, bias_ref, bias16_ref, onat_ref):
    seq = q_ref.shape[0]
    n_u4, n_u16 = seq // 4, seq // 16
    slope = jnp.exp2(jnp.full((1, 1), -8.0 / ATTN_HEADS, F32) * (pl.program_id(1) + 1)) * LOG2E
    bias_ref[0, 0] = _dense_bias(slope, False)
    bias_ref[0, 1] = _dense_bias(slope, True)
    bias_ref[1, 0] = _band_bias(slope, 4, False)
    bias_ref[1, 1] = _band_bias(slope, 4, True)
    bias16_ref[...] = _wide_bias(slope)

    for ti, src in enumerate((q_ref, k_ref, v_ref)):
        def to_phase4(piece, carry, ti=ti, src=src):
            dst = pl.ds(pl.multiple_of(piece * BAND, BAND), BAND)
            for p4 in range(4):
                rows = src[pl.ds(p4 + 4 * BAND * piece, BAND, stride=4), :]
                tmp_ref[p4, dst, :] = rows
                x4_ref[ti, p4, dst, :] = rows.astype(BF16)
            return carry

        lax.fori_loop(0, n_u4 // BAND, to_phase4, 0)

        def to_phase16(piece, carry, ti=ti):
            dst = pl.ds(pl.multiple_of(piece * BAND, BAND), BAND)
            for p16 in range(16):
                rows = tmp_ref[p16 % 4, pl.ds(p16 // 4 + 4 * BAND * piece, BAND, stride=4), :]
                x16_ref[ti, p16, dst, :] = rows.astype(BF16)
            return carry

        lax.fori_loop(0, n_u16 // BAND, to_phase16, 0)

    def dense_block(n):
        cur = pl.multiple_of(n * 32, 32)
        prev = pl.multiple_of(jnp.maximum(n - 1, 0) * 32, 32)
        first = jnp.where(n == 0, 1, 0)

        def rows(ti, start):
            return [x4_ref[ti, p, pl.ds(start, 32), :] for p in range(4)]

        q = jnp.concatenate(rows(0, cur), axis=0)
        k = jnp.concatenate(rows(1, prev) + rows(1, cur), axis=0)
        v = jnp.concatenate(rows(2, prev) + rows(2, cur), axis=0)
        num, den, mx = _softmax_block(q, k, v, bias_ref[0, first])
        for p in range(4):
            part = slice(p * 32, (p + 1) * 32)
            num_ref[0, p, pl.ds(cur, 32), :] = num[part]
            den_ref[0, p, pl.ds(cur, 32), :] = den[part]
            mx_ref[0, p, pl.ds(cur, 32), :] = mx[part]

    def band_block(p, n):
        cur = pl.multiple_of(n * BAND, BAND)
        prev = pl.multiple_of(jnp.maximum(n - 1, 0) * BAND, BAND)
        first = jnp.where(n == 0, 1, 0)
        q = x4_ref[0, p, pl.ds(cur, BAND), :]
        k = jnp.concatenate([x4_ref[1, p, pl.ds(prev, BAND), :], x4_ref[1, p, pl.ds(cur, BAND), :]], axis=0)
        v = jnp.concatenate([x4_ref[2, p, pl.ds(prev, BAND), :], x4_ref[2, p, pl.ds(cur, BAND), :]], axis=0)
        num, den, mx = _softmax_block(q, k, v, bias_ref[1, first])
        num_ref[1, p, pl.ds(cur, BAND), :] = num
        den_ref[1, p, pl.ds(cur, BAND), :] = den
        mx_ref[1, p, pl.ds(cur, BAND), :] = mx

    def wide_block(p4, a):
        p16 = 4 * a + p4
        num, den, mx = _softmax_block(x16_ref[0, p16], x16_ref[1, p16], x16_ref[2, p16], bias16_ref[...])
        rows = pl.ds(a, n_u16, stride=4)
        num_ref[2, p4, rows, :] = num
        den_ref[2, p4, rows, :] = den
        mx_ref[2, p4, rows, :] = mx

    blocks4 = n_u4 // BAND
    dense_per_it = n_u4 // 32 // ATTN_ITERS
    band_per_it = blocks4 // ATTN_ITERS
    wide_per_it = 4 // ATTN_ITERS

    def pattern_body(it, carry):
        for u in range(dense_per_it):
            dense_block(it * dense_per_it + u)
        for u in range(band_per_it):
            for p in range(4):
                band_block(p, it * band_per_it + u)
        for u in range(wide_per_it):
            for p4 in range(4):
                wide_block(p4, it * wide_per_it + u)
        return carry

    lax.fori_loop(0, ATTN_ITERS, pattern_body, 0)

    def merge_body(it, carry):
        p, n = it // blocks4, it % blocks4
        rows = pl.ds(pl.multiple_of(n * BAND, BAND), BAND)
        m0, m1, m2 = mx_ref[0, p, rows, :], mx_ref[1, p, rows, :], mx_ref[2, p, rows, :]
        m_all = jnp.maximum(jnp.maximum(m0, m1), m2)
        w0, w1, w2 = jnp.exp2(m0 - m_all), jnp.exp2(m1 - m_all), jnp.exp2(m2 - m_all)
        num = w0 * num_ref[0, p, rows, :] + w1 * num_ref[1, p, rows, :] + w2 * num_ref[2, p, rows, :]
        den = w0 * den_ref[0, p, rows, :] + w1 * den_ref[1, p, rows, :] + w2 * den_ref[2, p, rows, :]
        onat_ref[pl.ds(p + 4 * BAND * n, BAND, stride=4), :] = num / den
        return carry

    lax.fori_loop(0, 4 * blocks4, merge_body, 0, unroll=2)
    o_ref[...] = onat_ref[...].astype(o_ref.dtype)


def _attention(qkv, batch, seq):
    assert seq // 16 == 2 * BAND
    hd = ATTN_HEAD_DIM
    qkv4 = qkv.reshape(3 * ATTN_HEADS, batch, seq, hd)

    def spec(which):
        return pl.BlockSpec((None, None, seq, hd), lambda b, h: (which * ATTN_HEADS + h, b, 0, 0))

    out = pl.pallas_call(
        _attn_kernel,
        grid=(batch, ATTN_HEADS),
        in_specs=[spec(0), spec(1), spec(2)],
        out_specs=pl.BlockSpec((None, None, seq, hd), lambda b, h: (h, b, 0, 0)),
        out_shape=jax.ShapeDtypeStruct((ATTN_HEADS, batch, seq, hd), BF16),
        scratch_shapes=[
            pltpu.VMEM((3, 4, seq // 4, hd), BF16),
            pltpu.VMEM((3, 16, seq // 16, hd), BF16),
            pltpu.VMEM((4, seq // 4, hd), F32),
            pltpu.VMEM((3, 4, seq // 4, hd), F32),
            pltpu.VMEM((3, 4, seq // 4, hd), F32),
            pltpu.VMEM((3, 4, seq // 4, hd), F32),
            pltpu.VMEM((2, 2, BAND, 2 * BAND), F32),
            pltpu.VMEM((2 * BAND, 2 * BAND), F32),
            pltpu.VMEM((seq, hd), F32),
        ],
        compiler_params=_cparams(("parallel", "parallel")),
        name="dilated_attention",
    )(qkv4, qkv4, qkv4)
    return out.reshape(ATTN_HEADS, batch * seq, hd)


MLSTM_TS = 512
ONES_LANES = 128


def _mlstm_kernel(bias_ref, q_ref, k_ref, v_ref, gt_ref, out_ref, c_ref, m_ref, kt_ref):
    s = pl.program_id(1)
    dk, dv = MLSTM_QK_DIM, MLSTM_V_DIM
    L = CHUNK

    @pl.when(s == 0)
    def _():
        c_ref[...] = jnp.zeros_like(c_ref)
        m_ref[...] = jnp.zeros_like(m_ref)

    ti = lax.broadcasted_iota(jnp.int32, (CHUNK, CHUNK), 0)
    si = lax.broadcasted_iota(jnp.int32, (CHUNK, CHUNK), 1)
    causal = ti >= si
    csum = jnp.where(ti <= si, 1.0, 0.0).astype(F32)

    lane = lax.broadcasted_iota(jnp.int32, (MLSTM_HEADS, L), 1)

    for h in range(MLSTM_HEADS):
        for c in range(MLSTM_TS // L):
            k_hc = k_ref[h // 2, c * L:(c + 1) * L, (h % 2) * dk:(h % 2 + 1) * dk]
            kt_ref[h, c] = k_hc.astype(F32).T.astype(BF16)

    def chunk_body(c, carry):
        start = pl.multiple_of(c * L, L)
        gates = gt_ref[c] + bias_ref[...]
        ig_all = gates[:MLSTM_HEADS]
        lf_all = _log_sigmoid(gates[MLSTM_HEADS:])
        b_all = jnp.dot(lf_all, csum, preferred_element_type=F32,
                        precision=lax.Precision.HIGHEST)
        c_all = ig_all - b_all
        cm_all = c_all
        shift = 1
        while shift < L:
            cm_all = jnp.maximum(cm_all, jnp.where(lane >= shift, pltpu.roll(cm_all, shift, axis=1), MASKED))
            shift *= 2
        m_prev = m_ref[...]
        b_last = b_all[:, L - 1:L]
        top = jnp.maximum(m_prev, cm_all)
        m_new = b_last + top[:, L - 1:L]
        w_old = jnp.exp(b_last + m_prev - m_new)
        w_key = jnp.exp(b_last + c_all - m_new)
        cols = jnp.concatenate([
            -top,
            jnp.exp(m_prev - top),
            jnp.exp(-(b_all + top)),
        ], axis=0).T
        m_ref[...] = m_new

        def column(group, h):
            j = group * MLSTM_HEADS + h
            return jnp.broadcast_to(cols[:, j:j + 1], (L, L))

        def q_of(h):
            return q_ref[h // 2, pl.ds(start, L), (h % 2) * dk:(h % 2 + 1) * dk]

        def v_ext_of(h):
            return jnp.concatenate([v_ref[h, pl.ds(start, L), :], jnp.ones((L, ONES_LANES), BF16)], axis=1)

        heads = range(MLSTM_HEADS)
        scores = [jnp.dot(q_of(h), kt_ref[h, c], preferred_element_type=F32) for h in heads]
        incs = [jnp.dot((kt_ref[h, c].astype(F32) * w_key[h:h + 1]).astype(BF16), v_ext_of(h),
                        preferred_element_type=F32) for h in heads]
        nds = []
        for h in heads:
            w_intra = jnp.exp(jnp.where(causal, column(0, h) + c_all[h:h + 1], MASKED))
            lhs = jnp.concatenate([(scores[h] * w_intra).astype(BF16),
                                   (q_of(h).astype(F32) * column(1, h)).astype(BF16)], axis=1)
            rhs = jnp.concatenate([v_ext_of(h), c_ref[h].astype(BF16)], axis=0)
            nds.append(jnp.dot(lhs, rhs, preferred_element_type=F32))
        for h in heads:
            nd = nds[h]
            inv = 1.0 / jnp.maximum(jnp.abs(nd[:, dv:]), column(2, h))
            for half in range(dv // L):
                lanes = slice(half * L, (half + 1) * L)
                out_ref[h, pl.ds(start, L), lanes] = (nd[:, lanes] * inv).astype(out_ref.dtype)
            c_ref[h] = w_old[h:h + 1] * c_ref[h] + incs[h]
        return carry

    lax.fori_loop(0, MLSTM_TS // CHUNK, chunk_body, 0, unroll=2)


def _mlstm(slabs, gates_t, b_if, batch, seq):
    n_slabs = slabs.shape[0]
    s4d = slabs.reshape(n_slabs, batch, seq, SLAB)
    half = MLSTM_HEADS // 2
    tiles = seq // MLSTM_TS
    tile_chunks = MLSTM_TS // CHUNK

    def tile_spec(first_slab, n):
        return pl.BlockSpec((n, None, MLSTM_TS, SLAB), lambda b, s: (first_slab // n, b, s, 0))

    out = pl.pallas_call(
        _mlstm_kernel,
        grid=(batch, tiles),
        in_specs=[
            pl.BlockSpec((2 * MLSTM_HEADS, 1), lambda b, s: (0, 0)),
            tile_spec(S_MQK * GROUP_SLABS, half), tile_spec(S_MQK * GROUP_SLABS + half, half),
            tile_spec(S_MV * GROUP_SLABS, MLSTM_HEADS),
            pl.BlockSpec((tile_chunks, 2 * MLSTM_HEADS, CHUNK), lambda b, s: (b * tiles + s, 0, 0)),
        ],
        out_specs=pl.BlockSpec((MLSTM_HEADS, None, MLSTM_TS, SLAB), lambda b, s: (0, b, s, 0)),
        out_shape=jax.ShapeDtypeStruct((MLSTM_HEADS, batch, seq, SLAB), BF16),
        scratch_shapes=[
            pltpu.VMEM((MLSTM_HEADS, MLSTM_QK_DIM, MLSTM_V_DIM + ONES_LANES), F32),
            pltpu.VMEM((MLSTM_HEADS, 1), F32),
            pltpu.VMEM((MLSTM_HEADS, tile_chunks, MLSTM_QK_DIM, CHUNK), BF16),
        ],
        compiler_params=_cparams(("parallel", "arbitrary")),
        name="mlstm",
    )(b_if.reshape(-1, 1), s4d, s4d, s4d, gates_t)
    return out.reshape(MLSTM_HEADS, batch * seq, SLAB)


MERGE_TM = 512
OUT_TM = 512


def _weight_spec():
    return pl.BlockSpec((D_MODEL, D_MODEL), lambda i: (0, 0), pipeline_mode=pl.Buffered(1))


def _branch_merge_kernel(attn_ref, az_ref, cell_ref, mo_ref, mz_ref, ga_ref, gm_ref,
                         wa_ref, wm_ref, ng_ref, merged_ref):
    n = GROUP_SLABS
    hd = ATTN_HEAD_DIM
    gated = []
    for c in range(n):
        attn = jnp.concatenate([attn_ref[2 * c].astype(F32), attn_ref[2 * c + 1].astype(F32)], axis=1)
        gated.append(attn * _silu(az_ref[c].astype(F32)))
    gated = jnp.concatenate(gated, axis=1).astype(BF16)
    y_a, mem = [], []
    for h in range(MLSTM_HEADS):
        cols = slice(h * SLAB, (h + 1) * SLAB)
        y_a.append(jnp.dot(gated, wa_ref[:, cols], preferred_element_type=F32))
        cell = _sigmoid(mo_ref[h].astype(F32)) * cell_ref[h].astype(F32)
        cell = cell * lax.rsqrt(jnp.mean(cell * cell, axis=-1, keepdims=True) + NORM_EPS)
        cell = cell * ng_ref[:, cols]
        mem.append(cell * _silu(mz_ref[h].astype(F32)))
    mem = jnp.concatenate(mem, axis=1).astype(BF16)
    for c in range(n):
        cols = slice(c * SLAB, (c + 1) * SLAB)
        y_m = jnp.dot(mem, wm_ref[:, cols], preferred_element_type=F32)
        merged = _sigmoid(ga_ref[c].astype(F32)) * y_a[c] + _sigmoid(gm_ref[c].astype(F32)) * y_m
        merged_ref[:, cols] = merged.astype(merged_ref.dtype)


def _branch_merge(attn, slabs, cell, w_a, w_m, norm_g):
    t = attn.shape[1]
    assert MLSTM_V_DIM == SLAB and MLSTM_HEADS == GROUP_SLABS

    def slab_spec(group):
        return pl.BlockSpec((GROUP_SLABS, MERGE_TM, SLAB), lambda i: (group, i, 0))

    return pl.pallas_call(
        _branch_merge_kernel,
        grid=(t // MERGE_TM,),
        in_specs=[
            pl.BlockSpec((ATTN_HEADS, MERGE_TM, ATTN_HEAD_DIM), lambda i: (0, i, 0)),
            slab_spec(S_AZ), slab_spec(0), slab_spec(S_MO), slab_spec(S_MZ),
            slab_spec(S_GA), slab_spec(S_GM),
            _weight_spec(), _weight_spec(),
            pl.BlockSpec((1, D_MODEL), lambda i: (0, 0)),
        ],
        out_specs=pl.BlockSpec((MERGE_TM, D_MODEL), lambda i: (i, 0)),
        out_shape=jax.ShapeDtypeStruct((t, D_MODEL), BF16),
        compiler_params=_cparams(("parallel",)),
        name="branch_merge",
    )(attn, slabs, cell, slabs, slabs, slabs, slabs, w_a, w_m, norm_g)


def _out_proj_kernel(merged_ref, x_ref, wo_ref, fg_ref, out_ref):
    y = x_ref[...] + jnp.dot(merged_ref[...], wo_ref[...], preferred_element_type=F32)
    y = y * lax.rsqrt(jnp.mean(y * y, axis=-1, keepdims=True) + NORM_EPS)
    out_ref[...] = y * fg_ref[...]


def _out_proj(merged, x2, w_o, final_g):
    t = x2.shape[0]
    return pl.pallas_call(
        _out_proj_kernel,
        grid=(t // OUT_TM,),
        in_specs=[
            pl.BlockSpec((OUT_TM, D_MODEL), lambda i: (i, 0)),
            pl.BlockSpec((OUT_TM, D_MODEL), lambda i: (i, 0)),
            _weight_spec(),
            pl.BlockSpec((1, D_MODEL), lambda i: (0, 0)),
        ],
        out_specs=pl.BlockSpec((OUT_TM, D_MODEL), lambda i: (i, 0)),
        out_shape=jax.ShapeDtypeStruct((t, D_MODEL), F32),
        compiler_params=_cparams(("parallel",)),
        name="out_proj",
    )(merged, x2, w_o, final_g)


def kernel(x, norm_g, w_in, b_if, conv_w, conv_b, mlstm_norm_g, w_attn_branch, w_mlstm_branch,
           w_out, final_norm_g):
    batch, seq, d = x.shape
    assert d == D_MODEL and seq % (16 * BAND) == 0 and (batch * seq) % IN_TM == 0
    t = batch * seq
    x2 = x.reshape(t, d)

    w_big, w_gate = _weight_prep(w_in)
    assert w_big.shape[1] == QKV_COLS + N_SLAB_GROUPS * D_MODEL

    qkv, slabs, gates_t = _in_proj(x2, norm_g.reshape(1, d), w_big, w_gate, conv_w,
                                   conv_b.reshape(1, -1), seq)
    attn = _attention(qkv, batch, seq)
    cell = _mlstm(slabs, gates_t, b_if, batch, seq)
    merged = _branch_merge(attn, slabs, cell, w_attn_branch.astype(BF16), w_mlstm_branch.astype(BF16),
                           mlstm_norm_g.reshape(1, d))
    out = _out_proj(merged, x2, w_out.astype(BF16), final_norm_g.reshape(1, d))
    return out.reshape(batch, seq, d)
```

```python
import functools

import jax
import jax.numpy as jnp
from jax import lax
from jax.experimental import pallas as pl
from jax.experimental.pallas import tpu as pltpu

F32 = jnp.float32
BF16 = jnp.bfloat16

D_MODEL = 2048
ATTN_HEADS = 16
ATTN_HEAD_DIM = 128
MLSTM_HEADS = 8
MLSTM_QK_DIM = 128
MLSTM_V_DIM = 256
CONV_WIDTH = 4
NORM_EPS = 1e-6
BAND = 128
ATTN_ITERS = 2
SLAB = 256
GROUP_SLABS = D_MODEL // SLAB
QKV_COLS = 3 * D_MODEL
S_AZ, S_MQK, S_MV, S_MO, S_MZ, S_GA, S_GM = range(7)
N_SLAB_GROUPS = 7
GATE_LANES = 128
MASKED = -1e30
CHUNK = 128
VMEM_LIMIT = 56 * 1024 * 1024


def _cparams(sem):
    return pltpu.CompilerParams(dimension_semantics=sem, vmem_limit_bytes=VMEM_LIMIT)


LOG2E = 1.4426950408889634
NEG_LOG2E = -LOG2E


def _sigmoid(x):
    return 1.0 / (1.0 + jnp.exp2(x * NEG_LOG2E))


def _silu(x):
    half = 0.5 * x
    return half + half * jnp.tanh(half)


def _log_sigmoid(x):
    return jnp.minimum(x, 0.0) - jnp.log(1.0 + jnp.exp(-jnp.abs(x)))


PREP_TN = 1024
GATE_COLS = 2 * MLSTM_HEADS
GATE_COL0 = 4 * D_MODEL + 2 * MLSTM_HEADS * MLSTM_QK_DIM + MLSTM_HEADS * MLSTM_V_DIM
GATE_STEP = GATE_COL0 // PREP_TN
LANES = 128


def _weight_prep_kernel(a_ref, b_ref, w_ref, wg_ref):
    n = pl.program_id(0)

    @pl.when(n < GATE_STEP)
    def _():
        w_ref[...] = a_ref[...].astype(BF16)

    @pl.when(n >= GATE_STEP)
    def _():
        w_ref[...] = jnp.concatenate([a_ref[GATE_COLS:, :], b_ref[...]], axis=0).astype(BF16)

    @pl.when(n == GATE_STEP)
    def _():
        lane = lax.broadcasted_iota(jnp.int32, (D_MODEL, LANES), 1)
        wg_ref[...] = jnp.where(lane < GATE_COLS, a_ref[:LANES, :].T, 0.0).astype(BF16)


def _weight_prep(w_t):
    in_width, d = w_t.shape
    n_rows = in_width - GATE_COLS
    assert d == D_MODEL and GATE_COL0 % PREP_TN == 0 and n_rows % PREP_TN == 0 and GATE_LANES == LANES
    return pl.pallas_call(
        _weight_prep_kernel,
        grid=(n_rows // PREP_TN,),
        in_specs=[
            pl.BlockSpec((PREP_TN, D_MODEL), lambda n: (n, 0)),
            pl.BlockSpec((GATE_COLS, D_MODEL), lambda n: ((n + 1) * (PREP_TN // GATE_COLS), 0)),
        ],
        out_specs=[
            pl.BlockSpec((PREP_TN, D_MODEL), lambda n: (n, 0)),
            pl.BlockSpec((D_MODEL, LANES), lambda n: (0, 0)),
        ],
        out_shape=[
            jax.ShapeDtypeStruct((n_rows, D_MODEL), BF16),
            jax.ShapeDtypeStruct((D_MODEL, LANES), BF16),
        ],
        compiler_params=_cparams(("arbitrary",)),
        name="weight_prep",
    )(w_t, w_t)


IN_TM = 1024
IN_TN = 1024
QKV_STEPS = QKV_COLS // IN_TN


CONV_STEP0 = QKV_STEPS + S_MQK * (D_MODEL // IN_TN)
CONV_STEPS = D_MODEL // IN_TN
CONV_HALO = 8


def _in_proj_kernel(x_ref, g_ref, w_ref, wg_ref, cw_ref, cb_ref, qkv_ref, p_ref, gt_ref,
                    hn_ref, halo_ref, acc_ref, *, tiles_per_seq):
    i = pl.program_id(0)
    n = pl.program_id(1)

    @pl.when((i == 0) & (n == 0))
    def _():
        halo_ref[...] = jnp.zeros_like(halo_ref)

    @pl.when(n == 0)
    def _():
        x = x_ref[...]
        ms = jnp.mean(x * x, axis=-1, keepdims=True)
        hn = (x * lax.rsqrt(ms + NORM_EPS) * g_ref[...]).astype(BF16)
        hn_ref[...] = hn
        gates = jnp.dot(hn, wg_ref[...], preferred_element_type=F32)
        gates_t = gates.T
        for c in range(IN_TM // CHUNK):
            gt_ref[c] = gates_t[:2 * MLSTM_HEADS, c * CHUNK:(c + 1) * CHUNK]

    def slab_acc(j):
        return lax.dot_general(hn_ref[...], w_ref[j * SLAB:(j + 1) * SLAB, :], (((1,), (1,)), ((), ())),
                               preferred_element_type=F32)

    @pl.when(n < QKV_STEPS)
    def _():
        scale = jnp.where(n < D_MODEL // IN_TN, ATTN_HEAD_DIM ** -0.5 * LOG2E, 1.0).astype(F32)
        for j in range(IN_TN // SLAB):
            acc = slab_acc(j) * scale
            qkv_ref[2 * j] = acc[:, :ATTN_HEAD_DIM]
            qkv_ref[2 * j + 1] = acc[:, ATTN_HEAD_DIM:]

    is_conv = (n >= CONV_STEP0) & (n < CONV_STEP0 + CONV_STEPS)

    @pl.when((n >= QKV_STEPS) & jnp.logical_not(is_conv))
    def _():
        for j in range(IN_TN // SLAB):
            p_ref[j] = slab_acc(j).astype(BF16)

    @pl.when(is_conv)
    def _():
        step = n - CONV_STEP0
        seq_start = (i % tiles_per_seq) == 0
        post = jnp.where(step == 1, MLSTM_QK_DIM ** -0.5, 1.0).astype(F32)
        def conv_slab(j):
            cols = slice(j * SLAB, (j + 1) * SLAB)
            slot = (n + j) % 2
            acc_ref[slot, 0:CONV_HALO, :] = jnp.where(seq_start, 0.0, halo_ref[step, :, cols])
            halo_ref[step, :, cols] = acc_ref[slot, IN_TM:IN_TM + CONV_HALO, :]
            y = cb_ref[:, cols]
            for back in range(CONV_WIDTH):
                tap = acc_ref[slot, CONV_HALO - back:CONV_HALO - back + IN_TM, :]
                y = y + cw_ref[CONV_WIDTH - 1 - back:CONV_WIDTH - back, cols] * tap
            p_ref[j] = (_silu(y) * post).astype(BF16)

        n_slabs = IN_TN // SLAB
        for j in range(n_slabs):
            acc_ref[(n + j) % 2, CONV_HALO:, :] = slab_acc(j)
            if j > 0:
                conv_slab(j - 1)
        conv_slab(n_slabs - 1)


def _in_proj(x2, norm_g, w_big, w_gate, conv_w, conv_b, seq):
    t = x2.shape[0]
    n_cols = w_big.shape[0]
    grid = (t // IN_TM, n_cols // IN_TN)
    assert seq % IN_TM == 0 and CONV_STEPS * IN_TN == conv_w.shape[1]

    def conv_step(i, n):
        return (0, jnp.clip(n - CONV_STEP0, 0, CONV_STEPS - 1))

    return pl.pallas_call(
        functools.partial(_in_proj_kernel, tiles_per_seq=seq // IN_TM),
        grid=grid,
        in_specs=[
            pl.BlockSpec((IN_TM, D_MODEL), lambda i, n: (i, 0)),
            pl.BlockSpec((1, D_MODEL), lambda i, n: (0, 0)),
            pl.BlockSpec((IN_TN, D_MODEL), lambda i, n: (n, 0)),
            pl.BlockSpec((D_MODEL, GATE_LANES), lambda i, n: (0, 0)),
            pl.BlockSpec((CONV_WIDTH, IN_TN), conv_step),
            pl.BlockSpec((1, IN_TN), conv_step),
        ],
        out_specs=[
            pl.BlockSpec((IN_TN // ATTN_HEAD_DIM, IN_TM, ATTN_HEAD_DIM),
                         lambda i, n: (jnp.minimum(n, QKV_STEPS - 1), i, 0)),
            pl.BlockSpec((IN_TN // SLAB, IN_TM, SLAB),
                         lambda i, n: (jnp.maximum(n - QKV_STEPS, 0), i, 0)),
            pl.BlockSpec((IN_TM // CHUNK, 2 * MLSTM_HEADS, CHUNK), lambda i, n: (i, 0, 0)),
        ],
        out_shape=[
            jax.ShapeDtypeStruct((QKV_COLS // ATTN_HEAD_DIM, t, ATTN_HEAD_DIM), F32),
            jax.ShapeDtypeStruct(((n_cols - QKV_COLS) // SLAB, t, SLAB), BF16),
            jax.ShapeDtypeStruct((t // CHUNK, 2 * MLSTM_HEADS, CHUNK), F32),
        ],
        scratch_shapes=[pltpu.VMEM((IN_TM, D_MODEL), BF16),
                        pltpu.VMEM((CONV_STEPS, CONV_HALO, IN_TN), F32),
                        pltpu.VMEM((2, CONV_HALO + IN_TM, SLAB), F32)],
        compiler_params=_cparams(("arbitrary", "arbitrary")),
        name="in_proj",
    )(x2, norm_g, w_big, w_gate, conv_w, conv_b)


def _band_bias(slope, dilation, first):
    qi = lax.broadcasted_iota(jnp.int32, (BAND, 2 * BAND), 0)
    ki = lax.broadcasted_iota(jnp.int32, (BAND, 2 * BAND), 1)
    dist = BAND + qi - ki
    valid = (dist >= 0) & (dist <= BAND)
    if first:
        valid = valid & (ki >= BAND)
    return jnp.where(valid, -slope * (dist * dilation).astype(F32), MASKED)


def _dense_bias(slope, first):
    qi = lax.broadcasted_iota(jnp.int32, (BAND, 2 * BAND), 0)
    ki = lax.broadcasted_iota(jnp.int32, (BAND, 2 * BAND), 1)
    pq, jq = qi // 32, qi % 32
    half, pk, jk = ki // BAND, (ki % BAND) // 32, ki % 32
    dist = 4 * (jq - jk + 32 * (1 - half)) + pq - pk
    valid = (dist >= 0) & (dist <= BAND)
    if first:
        valid = valid & (half == 1)
    return jnp.where(valid, -slope * dist.astype(F32), MASKED)


def _wide_bias(slope):
    qi = lax.broadcasted_iota(jnp.int32, (2 * BAND, 2 * BAND), 0)
    ki = lax.broadcasted_iota(jnp.int32, (2 * BAND, 2 * BAND), 1)
    dist = qi - ki
    valid = (dist >= 0) & (dist <= BAND)
    return jnp.where(valid, -slope * (dist * 16).astype(F32), MASKED)


def _softmax_block(q, k, v, bias):
    s = lax.dot_general(q, k, (((1,), (1,)), ((), ())), preferred_element_type=F32) + bias
    m = jnp.max(s, axis=-1, keepdims=True)
    p = jnp.exp2(s - m).astype(BF16)
    v_ext = jnp.concatenate([v, jnp.ones_like(v)], axis=1)
    r = jnp.dot(p, v_ext, preferred_element_type=F32)
    hd = ATTN_HEAD_DIM
    return r[:, :hd], r[:, hd:], jnp.broadcast_to(m, (q.shape[0], hd))


def _attn_kernel(q_ref, k_ref, v_ref, o_ref,
                 x4_ref, x16_ref, tmp_ref, num_ref, den_ref, mx_ref, bias_ref, bias16_ref, onat_ref):
    seq = q_ref.shape[0]
    n_u4, n_u16 = seq // 4, seq // 16
    slope = jnp.exp2(jnp.full((1, 1), -8.0 / ATTN_HEADS, F32) * (pl.program_id(1) + 1)) * LOG2E
    bias_ref[0, 0] = _dense_bias(slope, False)
    bias_ref[0, 1] = _dense_bias(slope, True)
    bias_ref[1, 0] = _band_bias(slope, 4, False)
    bias_ref[1, 1] = _band_bias(slope, 4, True)
    bias16_ref[...] = _wide_bias(slope)

    for ti, src in enumerate((q_ref, k_ref, v_ref)):
        def to_phase4(piece, carry, ti=ti, src=src):
            dst = pl.ds(pl.multiple_of(piece * BAND, BAND), BAND)
            for p4 in range(4):
                rows = src[pl.ds(p4 + 4 * BAND * piece, BAND, stride=4), :]
                tmp_ref[p4, dst, :] = rows
                x4_ref[ti, p4, dst, :] = rows.astype(BF16)
            return carry

        lax.fori_loop(0, n_u4 // BAND, to_phase4, 0)

        def to_phase16(piece, carry, ti=ti):
            dst = pl.ds(pl.multiple_of(piece * BAND, BAND), BAND)
            for p16 in range(16):
                rows = tmp_ref[p16 % 4, pl.ds(p16 // 4 + 4 * BAND * piece, BAND, stride=4), :]
                x16_ref[ti, p16, dst, :] = rows.astype(BF16)
            return carry

        lax.fori_loop(0, n_u16 // BAND, to_phase16, 0)

    def dense_block(n):
        cur = pl.multiple_of(n * 32, 32)
        prev = pl.multiple_of(jnp.maximum(n - 1, 0) * 32, 32)
        first = jnp.where(n == 0, 1, 0)

        def rows(ti, start):
            return [x4_ref[ti, p, pl.ds(start, 32), :] for p in range(4)]

        q = jnp.concatenate(rows(0, cur), axis=0)
        k = jnp.concatenate(rows(1, prev) + rows(1, cur), axis=0)
        v = jnp.concatenate(rows(2, prev) + rows(2, cur), axis=0)
        num, den, mx = _softmax_block(q, k, v, bias_ref[0, first])
        for p in range(4):
            part = slice(p * 32, (p + 1) * 32)
            num_ref[0, p, pl.ds(cur, 32), :] = num[part]
            den_ref[0, p, pl.ds(cur, 32), :] = den[part]
            mx_ref[0, p, pl.ds(cur, 32), :] = mx[part]

    def band_block(p, n):
        cur = pl.multiple_of(n * BAND, BAND)
        prev = pl.multiple_of(jnp.maximum(n - 1, 0) * BAND, BAND)
        first = jnp.where(n == 0, 1, 0)
        q = x4_ref[0, p, pl.ds(cur, BAND), :]
        k = jnp.concatenate([x4_ref[1, p, pl.ds(prev, BAND), :], x4_ref[1, p, pl.ds(cur, BAND), :]], axis=0)
        v = jnp.concatenate([x4_ref[2, p, pl.ds(prev, BAND), :], x4_ref[2, p, pl.ds(cur, BAND), :]], axis=0)
        num, den, mx = _softmax_block(q, k, v, bias_ref[1, first])
        num_ref[1, p, pl.ds(cur, BAND), :] = num
        den_ref[1, p, pl.ds(cur, BAND), :] = den
        mx_ref[1, p, pl.ds(cur, BAND), :] = mx

    def wide_block(p4, a):
        p16 = 4 * a + p4
        num, den, mx = _softmax_block(x16_ref[0, p16], x16_ref[1, p16], x16_ref[2, p16], bias16_ref[...])
        rows = pl.ds(a, n_u16, stride=4)
        num_ref[2, p4, rows, :] = num
        den_ref[2, p4, rows, :] = den
        mx_ref[2, p4, rows, :] = mx

    blocks4 = n_u4 // BAND
    dense_per_it = n_u4 // 32 // ATTN_ITERS
    band_per_it = blocks4 // ATTN_ITERS
    wide_per_it = 4 // ATTN_ITERS

    def pattern_body(it, carry):
        for u in range(dense_per_it):
            dense_block(it * dense_per_it + u)
        for u in range(band_per_it):
            for p in range(4):
                band_block(p, it * band_per_it + u)
        for u in range(wide_per_it):
            for p4 in range(4):
                wide_block(p4, it * wide_per_it + u)
        return carry

    lax.fori_loop(0, ATTN_ITERS, pattern_body, 0)

    def merge_body(it, carry):
        p, n = it // blocks4, it % blocks4
        rows = pl.ds(pl.multiple_of(n * BAND, BAND), BAND)
        m0, m1, m2 = mx_ref[0, p, rows, :], mx_ref[1, p, rows, :], mx_ref[2, p, rows, :]
        m_all = jnp.maximum(jnp.maximum(m0, m1), m2)
        w0, w1, w2 = jnp.exp2(m0 - m_all), jnp.exp2(m1 - m_all), jnp.exp2(m2 - m_all)
        num = w0 * num_ref[0, p, rows, :] + w1 * num_ref[1, p, rows, :] + w2 * num_ref[2, p, rows, :]
        den = w0 * den_ref[0, p, rows, :] + w1 * den_ref[1, p, rows, :] + w2 * den_ref[2, p, rows, :]
        onat_ref[pl.ds(p + 4 * BAND * n, BAND, stride=4), :] = num / den
        return carry

    lax.fori_loop(0, 4 * blocks4, merge_body, 0, unroll=2)
    o_ref[...] = onat_ref[...].astype(o_ref.dtype)


def _attention(qkv, batch, seq):
    assert seq // 16 == 2 * BAND
    hd = ATTN_HEAD_DIM
    qkv4 = qkv.reshape(3 * ATTN_HEADS, batch, seq, hd)

    def spec(which):
        return pl.BlockSpec((None, None, seq, hd), lambda b, h: (which * ATTN_HEADS + h, b, 0, 0))

    out = pl.pallas_call(
        _attn_kernel,
        grid=(batch, ATTN_HEADS),
        in_specs=[spec(0), spec(1), spec(2)],
        out_specs=pl.BlockSpec((None, None, seq, hd), lambda b, h: (h, b, 0, 0)),
        out_shape=jax.ShapeDtypeStruct((ATTN_HEADS, batch, seq, hd), BF16),
        scratch_shapes=[
            pltpu.VMEM((3, 4, seq // 4, hd), BF16),
            pltpu.VMEM((3, 16, seq // 16, hd), BF16),
            pltpu.VMEM((4, seq // 4, hd), F32),
            pltpu.VMEM((3, 4, seq // 4, hd), F32),
            pltpu.VMEM((3, 4, seq // 4, hd), F32),
            pltpu.VMEM((3, 4, seq // 4, hd), F32),
            pltpu.VMEM((2, 2, BAND, 2 * BAND), F32),
            pltpu.VMEM((2 * BAND, 2 * BAND), F32),
            pltpu.VMEM((seq, hd), F32),
        ],
        compiler_params=_cparams(("parallel", "parallel")),
        name="dilated_attention",
    )(qkv4, qkv4, qkv4)
    return out.reshape(ATTN_HEADS, batch * seq, hd)


MLSTM_TS = 512
ONES_LANES = 128


def _mlstm_kernel(bias_ref, q_ref, k_ref, v_ref, gt_ref, out_ref, c_ref, m_ref, kt_ref):
    s = pl.program_id(1)
    dk, dv = MLSTM_QK_DIM, MLSTM_V_DIM
    L = CHUNK

    @pl.when(s == 0)
    def _():
        c_ref[...] = jnp.zeros_like(c_ref)
        m_ref[...] = jnp.zeros_like(m_ref)

    ti = lax.broadcasted_iota(jnp.int32, (CHUNK, CHUNK), 0)
    si = lax.broadcasted_iota(jnp.int32, (CHUNK, CHUNK), 1)
    causal = ti >= si
    csum = jnp.where(ti <= si, 1.0, 0.0).astype(F32)

    lane = lax.broadcasted_iota(jnp.int32, (MLSTM_HEADS, L), 1)

    for h in range(MLSTM_HEADS):
        for c in range(MLSTM_TS // L):
            k_hc = k_ref[h // 2, c * L:(c + 1) * L, (h % 2) * dk:(h % 2 + 1) * dk]
            kt_ref[h, c] = k_hc.astype(F32).T.astype(BF16)

    def chunk_body(c, carry):
        start = pl.multiple_of(c * L, L)
        gates = gt_ref[c] + bias_ref[...]
        ig_all = gates[:MLSTM_HEADS]
        lf_all = _log_sigmoid(gates[MLSTM_HEADS:])
        b_all = jnp.dot(lf_all, csum, preferred_element_type=F32,
                        precision=lax.Precision.HIGHEST)
        c_all = ig_all - b_all
        cm_all = c_all
        shift = 1
        while shift < L:
            cm_all = jnp.maximum(cm_all, jnp.where(lane >= shift, pltpu.roll(cm_all, shift, axis=1), MASKED))
            shift *= 2
        m_prev = m_ref[...]
        b_last = b_all[:, L - 1:L]
        top = jnp.maximum(m_prev, cm_all)
        m_new = b_last + top[:, L - 1:L]
        w_old = jnp.exp(b_last + m_prev - m_new)
        w_key = jnp.exp(b_last + c_all - m_new)
        cols = jnp.concatenate([
            -top,
            jnp.exp(m_prev - top),
            jnp.exp(-(b_all + top)),
        ], axis=0).T
        m_ref[...] = m_new

        def column(group, h):
            j = group * MLSTM_HEADS + h
            return jnp.broadcast_to(cols[:, j:j + 1], (L, L))

        def q_of(h):
            return q_ref[h // 2, pl.ds(start, L), (h % 2) * dk:(h % 2 + 1) * dk]

        def v_ext_of(h):
            return jnp.concatenate([v_ref[h, pl.ds(start, L), :], jnp.ones((L, ONES_LANES), BF16)], axis=1)

        heads = range(MLSTM_HEADS)
        scores = [jnp.dot(q_of(h), kt_ref[h, c], preferred_element_type=F32) for h in heads]
        incs = [jnp.dot((kt_ref[h, c].astype(F32) * w_key[h:h + 1]).astype(BF16), v_ext_of(h),
                        preferred_element_type=F32) for h in heads]
        nds = []
        for h in heads:
            w_intra = jnp.exp(jnp.where(causal, column(0, h) + c_all[h:h + 1], MASKED))
            lhs = jnp.concatenate([(scores[h] * w_intra).astype(BF16),
                                   (q_of(h).astype(F32) * column(1, h)).astype(BF16)], axis=1)
            rhs = jnp.concatenate([v_ext_of(h), c_ref[h].astype(BF16)], axis=0)
            nds.append(jnp.dot(lhs, rhs, preferred_element_type=F32))
        for h in heads:
            nd = nds[h]
            inv = 1.0 / jnp.maximum(jnp.abs(nd[:, dv:]), column(2, h))
            for half in range(dv // L):
                lanes = slice(half * L, (half + 1) * L)
                out_ref[h, pl.ds(start, L), lanes] = (nd[:, lanes] * inv).astype(out_ref.dtype)
            c_ref[h] = w_old[h:h + 1] * c_ref[h] + incs[h]
        return carry

    lax.fori_loop(0, MLSTM_TS // CHUNK, chunk_body, 0, unroll=2)


def _mlstm(slabs, gates_t, b_if, batch, seq):
    n_slabs = slabs.shape[0]
    s4d = slabs.reshape(n_slabs, batch, seq, SLAB)
    half = MLSTM_HEADS // 2
    tiles = seq // MLSTM_TS
    tile_chunks = MLSTM_TS // CHUNK

    def tile_spec(first_slab, n):
        return pl.BlockSpec((n, None, MLSTM_TS, SLAB), lambda b, s: (first_slab // n, b, s, 0))

    out = pl.pallas_call(
        _mlstm_kernel,
        grid=(batch, tiles),
        in_specs=[
            pl.BlockSpec((2 * MLSTM_HEADS, 1), lambda b, s: (0, 0)),
            tile_spec(S_MQK * GROUP_SLABS, half), tile_spec(S_MQK * GROUP_SLABS + half, half),
            tile_spec(S_MV * GROUP_SLABS, MLSTM_HEADS),
            pl.BlockSpec((tile_chunks, 2 * MLSTM_HEADS, CHUNK), lambda b, s: (b * tiles + s, 0, 0)),
        ],
        out_specs=pl.BlockSpec((MLSTM_HEADS, None, MLSTM_TS, SLAB), lambda b, s: (0, b, s, 0)),
        out_shape=jax.ShapeDtypeStruct((MLSTM_HEADS, batch, seq, SLAB), BF16),
        scratch_shapes=[
            pltpu.VMEM((MLSTM_HEADS, MLSTM_QK_DIM, MLSTM_V_DIM + ONES_LANES), F32),
            pltpu.VMEM((MLSTM_HEADS, 1), F32),
            pltpu.VMEM((MLSTM_HEADS, tile_chunks, MLSTM_QK_DIM, CHUNK), BF16),
        ],
        compiler_params=_cparams(("parallel", "arbitrary")),
        name="mlstm",
    )(b_if.reshape(-1, 1), s4d, s4d, s4d, gates_t)
    return out.reshape(MLSTM_HEADS, batch * seq, SLAB)


MERGE_TM = 512
OUT_TM = 512


def _weight_spec():
    return pl.BlockSpec((D_MODEL, D_MODEL), lambda i: (0, 0), pipeline_mode=pl.Buffered(1))


def _branch_merge_kernel(attn_ref, az_ref, cell_ref, mo_ref, mz_ref, ga_ref, gm_ref,
                         wa_ref, wm_ref, ng_ref, merged_ref):
    n = GROUP_SLABS
    hd = ATTN_HEAD_DIM
    gated = []
    for c in range(n):
        attn = jnp.concatenate([attn_ref[2 * c].astype(F32), attn_ref[2 * c + 1].astype(F32)], axis=1)
        gated.append(attn * _silu(az_ref[c].astype(F32)))
    gated = jnp.concatenate(gated, axis=1).astype(BF16)
    y_a, mem = [], []
    for h in range(MLSTM_HEADS):
        cols = slice(h * SLAB, (h + 1) * SLAB)
        y_a.append(jnp.dot(gated, wa_ref[:, cols], preferred_element_type=F32))
        cell = _sigmoid(mo_ref[h].astype(F32)) * cell_ref[h].astype(F32)
        cell = cell * lax.rsqrt(jnp.mean(cell * cell, axis=-1, keepdims=True) + NORM_EPS)
        cell = cell * ng_ref[:, cols]
        mem.append(cell * _silu(mz_ref[h].astype(F32)))
    mem = jnp.concatenate(mem, axis=1).astype(BF16)
    for c in range(n):
        cols = slice(c * SLAB, (c + 1) * SLAB)
        y_m = jnp.dot(mem, wm_ref[:, cols], preferred_element_type=F32)
        merged = _sigmoid(ga_ref[c].astype(F32)) * y_a[c] + _sigmoid(gm_ref[c].astype(F32)) * y_m
        merged_ref[:, cols] = merged.astype(merged_ref.dtype)


def _branch_merge(attn, slabs, cell, w_a, w_m, norm_g):
    t = attn.shape[1]
    assert MLSTM_V_DIM == SLAB and MLSTM_HEADS == GROUP_SLABS

    def slab_spec(group):
        return pl.BlockSpec((GROUP_SLABS, MERGE_TM, SLAB), lambda i: (group, i, 0))

    return pl.pallas_call(
        _branch_merge_kernel,
        grid=(t // MERGE_TM,),
        in_specs=[
            pl.BlockSpec((ATTN_HEADS, MERGE_TM, ATTN_HEAD_DIM), lambda i: (0, i, 0)),
            slab_spec(S_AZ), slab_spec(0), slab_spec(S_MO), slab_spec(S_MZ),
            slab_spec(S_GA), slab_spec(S_GM),
            _weight_spec(), _weight_spec(),
            pl.BlockSpec((1, D_MODEL), lambda i: (0, 0)),
        ],
        out_specs=pl.BlockSpec((MERGE_TM, D_MODEL), lambda i: (i, 0)),
        out_shape=jax.ShapeDtypeStruct((t, D_MODEL), BF16),
        compiler_params=_cparams(("parallel",)),
        name="branch_merge",
    )(attn, slabs, cell, slabs, slabs, slabs, slabs, w_a, w_m, norm_g)


def _out_proj_kernel(merged_ref, x_ref, wo_ref, fg_ref, out_ref):
    y = x_ref[...] + jnp.dot(merged_ref[...], wo_ref[...], preferred_element_type=F32)
    y = y * lax.rsqrt(jnp.mean(y * y, axis=-1, keepdims=True) + NORM_EPS)
    out_ref[...] = y * fg_ref[...]


def _out_proj(merged, x2, w_o, final_g):
    t = x2.shape[0]
    return pl.pallas_call(
        _out_proj_kernel,
        grid=(t // OUT_TM,),
        in_specs=[
            pl.BlockSpec((OUT_TM, D_MODEL), lambda i: (i, 0)),
            pl.BlockSpec((OUT_TM, D_MODEL), lambda i: (i, 0)),
            _weight_spec(),
            pl.BlockSpec((1, D_MODEL), lambda i: (0, 0)),
        ],
        out_specs=pl.BlockSpec((OUT_TM, D_MODEL), lambda i: (i, 0)),
        out_shape=jax.ShapeDtypeStruct((t, D_MODEL), F32),
        compiler_params=_cparams(("parallel",)),
        name="out_proj",
    )(merged, x2, w_o, final_g)


def kernel(x, norm_g, w_in, b_if, conv_w, conv_b, mlstm_norm_g, w_attn_branch, w_mlstm_branch,
           w_out, final_norm_g):
    batch, seq, d = x.shape
    assert d == D_MODEL and seq % (16 * BAND) == 0 and (batch * seq) % IN_TM == 0
    t = batch * seq
    x2 = x.reshape(t, d)

    w_big, w_gate = _weight_prep(w_in.T)
    assert w_big.shape[0] == QKV_COLS + N_SLAB_GROUPS * D_MODEL

    qkv, slabs, gates_t = _in_proj(x2, norm_g.reshape(1, d), w_big, w_gate, conv_w,
                                   conv_b.reshape(1, -1), seq)
    attn = _attention(qkv, batch, seq)
    cell = _mlstm(slabs, gates_t, b_if, batch, seq)
    merged = _branch_merge(attn, slabs, cell, w_attn_branch.astype(BF16), w_mlstm_branch.astype(BF16),
                           mlstm_norm_g.reshape(1, d))
    out = _out_proj(merged, x2, w_out.astype(BF16), final_norm_g.reshape(1, d))
    return out.reshape(batch, seq, d)
```

```python
import functools

import jax
import jax.numpy as jnp
from jax import lax
from jax.experimental import pallas as pl
from jax.experimental.pallas import tpu as pltpu

F32 = jnp.float32
BF16 = jnp.bfloat16

D_MODEL = 2048
ATTN_HEADS = 16
ATTN_HEAD_DIM = 128
MLSTM_HEADS = 8
MLSTM_QK_DIM = 128
MLSTM_V_DIM = 256
CONV_WIDTH = 4
NORM_EPS = 1e-6
BAND = 128
SLAB = 256
GROUP_SLABS = D_MODEL // SLAB
QKV_COLS = 3 * D_MODEL
S_AZ, S_MQK, S_MV, S_MO, S_MZ, S_GA, S_GM = range(7)
N_SLAB_GROUPS = 7
GATE_LANES = 128
MASKED = -1e30
CHUNK = 128
VMEM_LIMIT = 56 * 1024 * 1024


def _cparams(sem):
    return pltpu.CompilerParams(dimension_semantics=sem, vmem_limit_bytes=VMEM_LIMIT)


LOG2E = 1.4426950408889634
NEG_LOG2E = -LOG2E


def _sigmoid(x):
    return 1.0 / (1.0 + jnp.exp2(x * NEG_LOG2E))


def _silu(x):
    half = 0.5 * x
    return half + half * jnp.tanh(half)


def _log_sigmoid(x):
    return jnp.minimum(x, 0.0) - jnp.log(1.0 + jnp.exp(-jnp.abs(x)))


PREP_TN = 1024
GATE_COLS = 2 * MLSTM_HEADS
GATE_COL0 = 4 * D_MODEL + 2 * MLSTM_HEADS * MLSTM_QK_DIM + MLSTM_HEADS * MLSTM_V_DIM
GATE_STEP = GATE_COL0 // PREP_TN
LANES = 128


def _weight_prep_kernel(a_ref, b_ref, w_ref, wg_ref):
    n = pl.program_id(0)

    @pl.when(n < GATE_STEP)
    def _():
        w_ref[...] = a_ref[...].astype(BF16)

    @pl.when(n >= GATE_STEP)
    def _():
        w_ref[...] = jnp.concatenate([a_ref[GATE_COLS:, :], b_ref[...]], axis=0).astype(BF16)

    @pl.when(n == GATE_STEP)
    def _():
        lane = lax.broadcasted_iota(jnp.int32, (D_MODEL, LANES), 1)
        wg_ref[...] = jnp.where(lane < GATE_COLS, a_ref[:LANES, :].T, 0.0).astype(BF16)


def _weight_prep(w_t):
    in_width, d = w_t.shape
    n_rows = in_width - GATE_COLS
    assert d == D_MODEL and GATE_COL0 % PREP_TN == 0 and n_rows % PREP_TN == 0 and GATE_LANES == LANES
    return pl.pallas_call(
        _weight_prep_kernel,
        grid=(n_rows // PREP_TN,),
        in_specs=[
            pl.BlockSpec((PREP_TN, D_MODEL), lambda n: (n, 0)),
            pl.BlockSpec((GATE_COLS, D_MODEL), lambda n: ((n + 1) * (PREP_TN // GATE_COLS), 0)),
        ],
        out_specs=[
            pl.BlockSpec((PREP_TN, D_MODEL), lambda n: (n, 0)),
            pl.BlockSpec((D_MODEL, LANES), lambda n: (0, 0)),
        ],
        out_shape=[
            jax.ShapeDtypeStruct((n_rows, D_MODEL), BF16),
            jax.ShapeDtypeStruct((D_MODEL, LANES), BF16),
        ],
        compiler_params=_cparams(("arbitrary",)),
        name="weight_prep",
    )(w_t, w_t)


IN_TM = 1024
IN_TN = 1024
QKV_STEPS = QKV_COLS // IN_TN


CONV_STEP0 = QKV_STEPS + S_MQK * (D_MODEL // IN_TN)
CONV_STEPS = D_MODEL // IN_TN
CONV_HALO = 8


def _in_proj_kernel(x_ref, g_ref, w_ref, wg_ref, cw_ref, cb_ref, qkv_ref, p_ref, gt_ref,
                    hn_ref, halo_ref, acc_ref, *, tiles_per_seq):
    i = pl.program_id(0)
    n = pl.program_id(1)

    @pl.when((i == 0) & (n == 0))
    def _():
        halo_ref[...] = jnp.zeros_like(halo_ref)

    @pl.when(n == 0)
    def _():
        x = x_ref[...]
        ms = jnp.mean(x * x, axis=-1, keepdims=True)
        hn = (x * lax.rsqrt(ms + NORM_EPS) * g_ref[...]).astype(BF16)
        hn_ref[...] = hn
        gates = jnp.dot(hn, wg_ref[...], preferred_element_type=F32)
        gates_t = gates.T
        for c in range(IN_TM // CHUNK):
            gt_ref[c] = gates_t[:2 * MLSTM_HEADS, c * CHUNK:(c + 1) * CHUNK]

    def slab_acc(j):
        return lax.dot_general(hn_ref[...], w_ref[j * SLAB:(j + 1) * SLAB, :], (((1,), (1,)), ((), ())),
                               preferred_element_type=F32)

    @pl.when(n < QKV_STEPS)
    def _():
        scale = jnp.where(n < D_MODEL // IN_TN, ATTN_HEAD_DIM ** -0.5 * LOG2E, 1.0).astype(F32)
        for j in range(IN_TN // SLAB):
            acc = slab_acc(j) * scale
            qkv_ref[2 * j] = acc[:, :ATTN_HEAD_DIM]
            qkv_ref[2 * j + 1] = acc[:, ATTN_HEAD_DIM:]

    is_conv = (n >= CONV_STEP0) & (n < CONV_STEP0 + CONV_STEPS)

    @pl.when((n >= QKV_STEPS) & jnp.logical_not(is_conv))
    def _():
        for j in range(IN_TN // SLAB):
            p_ref[j] = slab_acc(j).astype(BF16)

    @pl.when(is_conv)
    def _():
        step = n - CONV_STEP0
        seq_start = (i % tiles_per_seq) == 0
        post = jnp.where(step == 1, MLSTM_QK_DIM ** -0.5, 1.0).astype(F32)

        def conv_slab(j):
            cols = slice(j * SLAB, (j + 1) * SLAB)
            slot = (n + j) % 2
            acc_ref[slot, 0:CONV_HALO, :] = jnp.where(seq_start, 0.0, halo_ref[step, :, cols])
            halo_ref[step, :, cols] = acc_ref[slot, IN_TM:IN_TM + CONV_HALO, :]
            y = cb_ref[:, cols]
            for back in range(CONV_WIDTH):
                tap = acc_ref[slot, CONV_HALO - back:CONV_HALO - back + IN_TM, :]
                y = y + cw_ref[CONV_WIDTH - 1 - back:CONV_WIDTH - back, cols] * tap
            p_ref[j] = (_silu(y) * post).astype(BF16)

        n_slabs = IN_TN // SLAB
        for j in range(n_slabs):
            acc_ref[(n + j) % 2, CONV_HALO:, :] = slab_acc(j)
            if j > 0:
                conv_slab(j - 1)
        conv_slab(n_slabs - 1)


def _in_proj(x2, norm_g, w_big, w_gate, conv_w, conv_b, seq):
    t = x2.shape[0]
    n_cols = w_big.shape[0]
    grid = (t // IN_TM, n_cols // IN_TN)
    assert seq % IN_TM == 0 and CONV_STEPS * IN_TN == conv_w.shape[1]

    def conv_step(i, n):
        return (0, jnp.clip(n - CONV_STEP0, 0, CONV_STEPS - 1))

    return pl.pallas_call(
        functools.partial(_in_proj_kernel, tiles_per_seq=seq // IN_TM),
        grid=grid,
        in_specs=[
            pl.BlockSpec((IN_TM, D_MODEL), lambda i, n: (i, 0)),
            pl.BlockSpec((1, D_MODEL), lambda i, n: (0, 0)),
            pl.BlockSpec((IN_TN, D_MODEL), lambda i, n: (n, 0)),
            pl.BlockSpec((D_MODEL, GATE_LANES), lambda i, n: (0, 0)),
            pl.BlockSpec((CONV_WIDTH, IN_TN), conv_step),
            pl.BlockSpec((1, IN_TN), conv_step),
        ],
        out_specs=[
            pl.BlockSpec((IN_TN // ATTN_HEAD_DIM, IN_TM, ATTN_HEAD_DIM),
                         lambda i, n: (jnp.minimum(n, QKV_STEPS - 1), i, 0)),
            pl.BlockSpec((IN_TN // SLAB, IN_TM, SLAB),
                         lambda i, n: (jnp.maximum(n - QKV_STEPS, 0), i, 0)),
            pl.BlockSpec((IN_TM // CHUNK, 2 * MLSTM_HEADS, CHUNK), lambda i, n: (i, 0, 0)),
        ],
        out_shape=[
            jax.ShapeDtypeStruct((QKV_COLS // ATTN_HEAD_DIM, t, ATTN_HEAD_DIM), F32),
            jax.ShapeDtypeStruct(((n_cols - QKV_COLS) // SLAB, t, SLAB), BF16),
            jax.ShapeDtypeStruct((t // CHUNK, 2 * MLSTM_HEADS, CHUNK), F32),
        ],
        scratch_shapes=[pltpu.VMEM((IN_TM, D_MODEL), BF16),
                        pltpu.VMEM((CONV_STEPS, CONV_HALO, IN_TN), F32),
                        pltpu.VMEM((2, CONV_HALO + IN_TM, SLAB), F32)],
        compiler_params=_cparams(("arbitrary", "arbitrary")),
        name="in_proj",
    )(x2, norm_g, w_big, w_gate, conv_w, conv_b)


def _band_bias(slope, dilation, first):
    qi = lax.broadcasted_iota(jnp.int32, (BAND, 2 * BAND), 0)
    ki = lax.broadcasted_iota(jnp.int32, (BAND, 2 * BAND), 1)
    dist = BAND + qi - ki
    valid = (dist >= 0) & (dist <= BAND)
    if first:
        valid = valid & (ki >= BAND)
    return jnp.where(valid, -slope * (dist * dilation).astype(F32), MASKED)


def _dense_bias(slope, first):
    qi = lax.broadcasted_iota(jnp.int32, (BAND, 2 * BAND), 0)
    ki = lax.broadcasted_iota(jnp.int32, (BAND, 2 * BAND), 1)
    pq, jq = qi // 32, qi % 32
    half, pk, jk = ki // BAND, (ki % BAND) // 32, ki % 32
    dist = 4 * (jq - jk + 32 * (1 - half)) + pq - pk
    valid = (dist >= 0) & (dist <= BAND)
    if first:
        valid = valid & (half == 1)
    return jnp.where(valid, -slope * dist.astype(F32), MASKED)


def _wide_bias(slope):
    qi = lax.broadcasted_iota(jnp.int32, (2 * BAND, 2 * BAND), 0)
    ki = lax.broadcasted_iota(jnp.int32, (2 * BAND, 2 * BAND), 1)
    dist = qi - ki
    valid = (dist >= 0) & (dist <= BAND)
    return jnp.where(valid, -slope * (dist * 16).astype(F32), MASKED)


def _softmax_block(q, k, v, bias):
    s = lax.dot_general(q, k, (((1,), (1,)), ((), ())), preferred_element_type=F32) + bias
    m = jnp.max(s, axis=-1, keepdims=True)
    p = jnp.exp2(s - m).astype(BF16)
    v_ext = jnp.concatenate([v, jnp.ones_like(v)], axis=1)
    r = jnp.dot(p, v_ext, preferred_element_type=F32)
    hd = ATTN_HEAD_DIM
    return r[:, :hd], r[:, hd:], jnp.broadcast_to(m, (q.shape[0], hd))


def _attn_kernel(q_ref, k_ref, v_ref, o_ref,
                 x4_ref, x16_ref, tmp_ref, num_ref, den_ref, mx_ref, bias_ref, bias16_ref, onat_ref):
    seq = q_ref.shape[0]
    n_u4, n_u16 = seq // 4, seq // 16
    slope = jnp.exp2(jnp.full((1, 1), -8.0 / ATTN_HEADS, F32) * (pl.program_id(1) + 1)) * LOG2E
    bias_ref[0, 0] = _dense_bias(slope, False)
    bias_ref[0, 1] = _dense_bias(slope, True)
    bias_ref[1, 0] = _band_bias(slope, 4, False)
    bias_ref[1, 1] = _band_bias(slope, 4, True)
    bias16_ref[...] = _wide_bias(slope)

    for ti, src in enumerate((q_ref, k_ref, v_ref)):
        def to_phase4(piece, carry, ti=ti, src=src):
            dst = pl.ds(pl.multiple_of(piece * BAND, BAND), BAND)
            for p4 in range(4):
                rows = src[pl.ds(p4 + 4 * BAND * piece, BAND, stride=4), :]
                tmp_ref[p4, dst, :] = rows
                x4_ref[ti, p4, dst, :] = rows.astype(BF16)
            return carry

        lax.fori_loop(0, n_u4 // BAND, to_phase4, 0)

        def to_phase16(piece, carry, ti=ti):
            dst = pl.ds(pl.multiple_of(piece * BAND, BAND), BAND)
            for p16 in range(16):
                rows = tmp_ref[p16 % 4, pl.ds(p16 // 4 + 4 * BAND * piece, BAND, stride=4), :]
                x16_ref[ti, p16, dst, :] = rows.astype(BF16)
            return carry

        lax.fori_loop(0, n_u16 // BAND, to_phase16, 0)

    def store(pattern, p, rows, result):
        num, den, mx = result
        num_ref[pattern, p, rows, :] = num
        den_ref[pattern, p, rows, :] = den
        mx_ref[pattern, p, rows, :] = mx

    def dense_block(n):
        cur, prev = slice(n * 32, (n + 1) * 32), slice(max(n - 1, 0) * 32, max(n - 1, 0) * 32 + 32)

        def rows(ti, sl):
            return [x4_ref[ti, p, sl, :] for p in range(4)]

        q = jnp.concatenate(rows(0, cur), axis=0)
        k = jnp.concatenate(rows(1, prev) + rows(1, cur), axis=0)
        v = jnp.concatenate(rows(2, prev) + rows(2, cur), axis=0)
        num, den, mx = _softmax_block(q, k, v, bias_ref[0, int(n == 0)])
        for p in range(4):
            part = slice(p * 32, (p + 1) * 32)
            store(0, p, cur, (num[part], den[part], mx[part]))

    def band_block(p, n):
        cur = slice(n * BAND, (n + 1) * BAND)
        prev = slice(max(n - 1, 0) * BAND, max(n - 1, 0) * BAND + BAND)
        q = x4_ref[0, p, cur, :]
        k = jnp.concatenate([x4_ref[1, p, prev, :], x4_ref[1, p, cur, :]], axis=0)
        v = jnp.concatenate([x4_ref[2, p, prev, :], x4_ref[2, p, cur, :]], axis=0)
        store(1, p, cur, _softmax_block(q, k, v, bias_ref[1, int(n == 0)]))

    def wide_block(p16, half):
        part = slice(half * BAND, (half + 1) * BAND)
        result = _softmax_block(x16_ref[0, p16, part, :], x16_ref[1, p16], x16_ref[2, p16], bias16_ref[part, :])
        store(2, p16 % 4, pl.ds(p16 // 4 + 4 * BAND * half, BAND, stride=4), result)

    def merge_block(p, n):
        rows = slice(n * BAND, (n + 1) * BAND)
        m0, m1, m2 = mx_ref[0, p, rows, :], mx_ref[1, p, rows, :], mx_ref[2, p, rows, :]
        m_all = jnp.maximum(jnp.maximum(m0, m1), m2)
        w0, w1, w2 = jnp.exp2(m0 - m_all), jnp.exp2(m1 - m_all), jnp.exp2(m2 - m_all)
        num = w0 * num_ref[0, p, rows, :] + w1 * num_ref[1, p, rows, :] + w2 * num_ref[2, p, rows, :]
        den = w0 * den_ref[0, p, rows, :] + w1 * den_ref[1, p, rows, :] + w2 * den_ref[2, p, rows, :]
        onat_ref[pl.ds(p + 4 * BAND * n, BAND, stride=4), :] = num / den

    halves = 2
    dense_per_half = n_u4 // 32 // halves
    band_per_half = n_u4 // BAND // halves
    for half in range(halves):
        for n in range(half * dense_per_half, (half + 1) * dense_per_half):
            dense_block(n)
        for n in range(half * band_per_half, (half + 1) * band_per_half):
            for p in range(4):
                band_block(p, n)
                if half > 0:
                    merge_block(p, n - band_per_half)
        for p16 in range(16):
            wide_block(p16, half)
    for n in range(band_per_half, 2 * band_per_half):
        for p in range(4):
            merge_block(p, n)
    o_ref[...] = onat_ref[...].astype(o_ref.dtype)


def _attention(qkv, batch, seq):
    assert seq // 16 == 2 * BAND
    hd = ATTN_HEAD_DIM
    qkv4 = qkv.reshape(3 * ATTN_HEADS, batch, seq, hd)

    def spec(which):
        return pl.BlockSpec((None, None, seq, hd), lambda b, h: (which * ATTN_HEADS + h, b, 0, 0))

    out = pl.pallas_call(
        _attn_kernel,
        grid=(batch, ATTN_HEADS),
        in_specs=[spec(0), spec(1), spec(2)],
        out_specs=pl.BlockSpec((None, None, seq, hd), lambda b, h: (h, b, 0, 0)),
        out_shape=jax.ShapeDtypeStruct((ATTN_HEADS, batch, seq, hd), BF16),
        scratch_shapes=[
            pltpu.VMEM((3, 4, seq // 4, hd), BF16),
            pltpu.VMEM((3, 16, seq // 16, hd), BF16),
            pltpu.VMEM((4, seq // 4, hd), F32),
            pltpu.VMEM((3, 4, seq // 4, hd), F32),
            pltpu.VMEM((3, 4, seq // 4, hd), F32),
            pltpu.VMEM((3, 4, seq // 4, hd), F32),
            pltpu.VMEM((2, 2, BAND, 2 * BAND), F32),
            pltpu.VMEM((2 * BAND, 2 * BAND), F32),
            pltpu.VMEM((seq, hd), F32),
        ],
        compiler_params=_cparams(("parallel", "parallel")),
        name="dilated_attention",
    )(qkv4, qkv4, qkv4)
    return out.reshape(ATTN_HEADS, batch * seq, hd)


MLSTM_TS = 512
ONES_LANES = 128


def _mlstm_kernel(bias_ref, q_ref, k_ref, v_ref, gt_ref, out_ref, c_ref, m_ref, kt_ref):
    s = pl.program_id(1)
    dk, dv = MLSTM_QK_DIM, MLSTM_V_DIM
    L = CHUNK

    @pl.when(s == 0)
    def _():
        c_ref[...] = jnp.zeros_like(c_ref)
        m_ref[...] = jnp.zeros_like(m_ref)

    ti = lax.broadcasted_iota(jnp.int32, (CHUNK, CHUNK), 0)
    si = lax.broadcasted_iota(jnp.int32, (CHUNK, CHUNK), 1)
    causal = ti >= si
    csum = jnp.where(ti <= si, 1.0, 0.0).astype(F32)

    lane = lax.broadcasted_iota(jnp.int32, (MLSTM_HEADS, L), 1)

    for h in range(MLSTM_HEADS):
        for c in range(MLSTM_TS // L):
            k_hc = k_ref[h // 2, c * L:(c + 1) * L, (h % 2) * dk:(h % 2 + 1) * dk]
            kt_ref[h, c] = k_hc.astype(F32).T.astype(BF16)

    def chunk_body(c, carry):
        start = pl.multiple_of(c * L, L)
        gates = gt_ref[c] + bias_ref[...]
        ig_all = gates[:MLSTM_HEADS]
        lf_all = _log_sigmoid(gates[MLSTM_HEADS:])
        b_all = jnp.dot(lf_all, csum, preferred_element_type=F32,
                        precision=lax.Precision.HIGHEST)
        c_all = ig_all - b_all
        cm_all = c_all
        shift = 1
        while shift < L:
            cm_all = jnp.maximum(cm_all, jnp.where(lane >= shift, pltpu.roll(cm_all, shift, axis=1), MASKED))
            shift *= 2
        m_prev = m_ref[...]
        b_last = b_all[:, L - 1:L]
        top = jnp.maximum(m_prev, cm_all)
        m_new = b_last + top[:, L - 1:L]
        w_old = jnp.exp(b_last + m_prev - m_new)
        w_key = jnp.exp(b_last + c_all - m_new)
        cols = jnp.concatenate([
            -top,
            jnp.exp(m_prev - top),
            jnp.exp(-(b_all + top)),
        ], axis=0).T
        m_ref[...] = m_new

        def column(group, h):
            j = group * MLSTM_HEADS + h
            return jnp.broadcast_to(cols[:, j:j + 1], (L, L))

        def q_of(h):
            return q_ref[h // 2, pl.ds(start, L), (h % 2) * dk:(h % 2 + 1) * dk]

        def v_ext_of(h):
            return jnp.concatenate([v_ref[h, pl.ds(start, L), :], jnp.ones((L, ONES_LANES), BF16)], axis=1)

        heads = range(MLSTM_HEADS)
        scores = [jnp.dot(q_of(h), kt_ref[h, c], preferred_element_type=F32) for h in heads]
        incs = [jnp.dot((kt_ref[h, c].astype(F32) * w_key[h:h + 1]).astype(BF16), v_ext_of(h),
                        preferred_element_type=F32) for h in heads]
        nds = []
        for h in heads:
            w_intra = jnp.exp(jnp.where(causal, column(0, h) + c_all[h:h + 1], MASKED))
            lhs = jnp.concatenate([(scores[h] * w_intra).astype(BF16),
                                   (q_of(h).astype(F32) * column(1, h)).astype(BF16)], axis=1)
            rhs = jnp.concatenate([v_ext_of(h), c_ref[h].astype(BF16)], axis=0)
            nds.append(jnp.dot(lhs, rhs, preferred_element_type=F32))
        for h in heads:
            nd = nds[h]
            inv = 1.0 / jnp.maximum(jnp.abs(nd[:, dv:]), column(2, h))
            for half in range(dv // L):
                lanes = slice(half * L, (half + 1) * L)
                out_ref[h, pl.ds(start, L), lanes] = (nd[:, lanes] * inv).astype(out_ref.dtype)
            c_ref[h] = w_old[h:h + 1] * c_ref[h] + incs[h]
        return carry

    lax.fori_loop(0, MLSTM_TS // CHUNK, chunk_body, 0, unroll=2)


def _mlstm(slabs, gates_t, b_if, batch, seq):
    n_slabs = slabs.shape[0]
    s4d = slabs.reshape(n_slabs, batch, seq, SLAB)
    half = MLSTM_HEADS // 2
    tiles = seq // MLSTM_TS
    tile_chunks = MLSTM_TS // CHUNK

    def tile_spec(first_slab, n):
        return pl.BlockSpec((n, None, MLSTM_TS, SLAB), lambda b, s: (first_slab // n, b, s, 0))

    out = pl.pallas_call(
        _mlstm_kernel,
        grid=(batch, tiles),
        in_specs=[
            pl.BlockSpec((2 * MLSTM_HEADS, 1), lambda b, s: (0, 0)),
            tile_spec(S_MQK * GROUP_SLABS, half), tile_spec(S_MQK * GROUP_SLABS + half, half),
            tile_spec(S_MV * GROUP_SLABS, MLSTM_HEADS),
            pl.BlockSpec((tile_chunks, 2 * MLSTM_HEADS, CHUNK), lambda b, s: (b * tiles + s, 0, 0)),
        ],
        out_specs=pl.BlockSpec((MLSTM_HEADS, None, MLSTM_TS, SLAB), lambda b, s: (0, b, s, 0)),
        out_shape=jax.ShapeDtypeStruct((MLSTM_HEADS, batch, seq, SLAB), BF16),
        scratch_shapes=[
            pltpu.VMEM((MLSTM_HEADS, MLSTM_QK_DIM, MLSTM_V_DIM + ONES_LANES), F32),
            pltpu.VMEM((MLSTM_HEADS, 1), F32),
            pltpu.VMEM((MLSTM_HEADS, tile_chunks, MLSTM_QK_DIM, CHUNK), BF16),
        ],
        compiler_params=_cparams(("parallel", "arbitrary")),
        name="mlstm",
    )(b_if.reshape(-1, 1), s4d, s4d, s4d, gates_t)
    return out.reshape(MLSTM_HEADS, batch * seq, SLAB)


MERGE_TM = 512
OUT_TM = 512


def _weight_spec():
    return pl.BlockSpec((D_MODEL, D_MODEL), lambda i: (0, 0), pipeline_mode=pl.Buffered(1))


def _branch_merge_kernel(attn_ref, az_ref, cell_ref, mo_ref, mz_ref, ga_ref, gm_ref,
                         wa_ref, wm_ref, ng_ref, merged_ref):
    n = GROUP_SLABS
    gated = []
    for c in range(n):
        attn = jnp.concatenate([attn_ref[2 * c].astype(F32), attn_ref[2 * c + 1].astype(F32)], axis=1)
        gated.append(attn * _silu(az_ref[c].astype(F32)))
    gated = jnp.concatenate(gated, axis=1).astype(BF16)
    y_a, mem = [], []
    for h in range(MLSTM_HEADS):
        cols = slice(h * SLAB, (h + 1) * SLAB)
        y_a.append(jnp.dot(gated, wa_ref[:, cols], preferred_element_type=F32))
        cell = _sigmoid(mo_ref[h].astype(F32)) * cell_ref[h].astype(F32)
        cell = cell * lax.rsqrt(jnp.mean(cell * cell, axis=-1, keepdims=True) + NORM_EPS)
        cell = cell * ng_ref[:, cols]
        mem.append(cell * _silu(mz_ref[h].astype(F32)))
    mem = jnp.concatenate(mem, axis=1).astype(BF16)
    for c in range(n):
        cols = slice(c * SLAB, (c + 1) * SLAB)
        y_m = jnp.dot(mem, wm_ref[:, cols], preferred_element_type=F32)
        merged = _sigmoid(ga_ref[c].astype(F32)) * y_a[c] + _sigmoid(gm_ref[c].astype(F32)) * y_m
        merged_ref[:, cols] = merged.astype(merged_ref.dtype)


def _branch_merge(attn, slabs, cell, w_a, w_m, norm_g):
    t = attn.shape[1]
    assert MLSTM_V_DIM == SLAB and MLSTM_HEADS == GROUP_SLABS

    def slab_spec(group):
        return pl.BlockSpec((GROUP_SLABS, MERGE_TM, SLAB), lambda i: (group, i, 0))

    return pl.pallas_call(
        _branch_merge_kernel,
        grid=(t // MERGE_TM,),
        in_specs=[
            pl.BlockSpec((ATTN_HEADS, MERGE_TM, ATTN_HEAD_DIM), lambda i: (0, i, 0)),
            slab_spec(S_AZ), slab_spec(0), slab_spec(S_MO), slab_spec(S_MZ),
            slab_spec(S_GA), slab_spec(S_GM),
            _weight_spec(), _weight_spec(),
            pl.BlockSpec((1, D_MODEL), lambda i: (0, 0)),
        ],
        out_specs=pl.BlockSpec((MERGE_TM, D_MODEL), lambda i: (i, 0)),
        out_shape=jax.ShapeDtypeStruct((t, D_MODEL), BF16),
        compiler_params=_cparams(("parallel",)),
        name="branch_merge",
    )(attn, slabs, cell, slabs, slabs, slabs, slabs, w_a, w_m, norm_g)


def _out_proj_kernel(merged_ref, x_ref, wo_ref, fg_ref, out_ref):
    y = x_ref[...] + jnp.dot(merged_ref[...], wo_ref[...], preferred_element_type=F32)
    y = y * lax.rsqrt(jnp.mean(y * y, axis=-1, keepdims=True) + NORM_EPS)
    out_ref[...] = y * fg_ref[...]


def _out_proj(merged, x2, w_o, final_g):
    t = x2.shape[0]
    return pl.pallas_call(
        _out_proj_kernel,
        grid=(t // OUT_TM,),
        in_specs=[
            pl.BlockSpec((OUT_TM, D_MODEL), lambda i: (i, 0)),
            pl.BlockSpec((OUT_TM, D_MODEL), lambda i: (i, 0)),
            _weight_spec(),
            pl.BlockSpec((1, D_MODEL), lambda i: (0, 0)),
        ],
        out_specs=pl.BlockSpec((OUT_TM, D_MODEL), lambda i: (i, 0)),
        out_shape=jax.ShapeDtypeStruct((t, D_MODEL), F32),
        compiler_params=_cparams(("parallel",)),
        name="out_proj",
    )(merged, x2, w_o, final_g)


def kernel(x, norm_g, w_in, b_if, conv_w, conv_b, mlstm_norm_g, w_attn_branch, w_mlstm_branch,
           w_out, final_norm_g):
    batch, seq, d = x.shape
    assert d == D_MODEL and seq % (16 * BAND) == 0 and (batch * seq) % IN_TM == 0
    t = batch * seq
    x2 = x.reshape(t, d)

    w_big, w_gate = _weight_prep(w_in.T)
    assert w_big.shape[0] == QKV_COLS + N_SLAB_GROUPS * D_MODEL

    qkv, slabs, gates_t = _in_proj(x2, norm_g.reshape(1, d), w_big, w_gate, conv_w,
                                   conv_b.reshape(1, -1), seq)
    attn = _attention(qkv, batch, seq)
    cell = _mlstm(slabs, gates_t, b_if, batch, seq)
    merged = _branch_merge(attn, slabs, cell, w_attn_branch.astype(BF16), w_mlstm_branch.astype(BF16),
                           mlstm_norm_g.reshape(1, d))
    out = _out_proj(merged, x2, w_out.astype(BF16), final_norm_g.reshape(1, d))
    return out.reshape(batch, seq, d)
```

```python
import functools

import jax
import jax.numpy as jnp
from jax import lax
from jax.experimental import pallas as pl
from jax.experimental.pallas import tpu as pltpu

F32 = jnp.float32
BF16 = jnp.bfloat16

D_MODEL = 2048
ATTN_HEADS = 16
ATTN_HEAD_DIM = 128
MLSTM_HEADS = 8
MLSTM_QK_DIM = 128
MLSTM_V_DIM = 256
CONV_WIDTH = 4
NORM_EPS = 1e-6
BAND = 128
SLAB = 256
GROUP_SLABS = D_MODEL // SLAB
QKV_COLS = 3 * D_MODEL
S_AZ, S_MQK, S_MV, S_MO, S_MZ, S_GA, S_GM = range(7)
N_SLAB_GROUPS = 7
GATE_LANES = 128
MASKED = -1e30
CHUNK = 128
VMEM_LIMIT = 56 * 1024 * 1024


def _cparams(sem):
    return pltpu.CompilerParams(dimension_semantics=sem, vmem_limit_bytes=VMEM_LIMIT)


LOG2E = 1.4426950408889634
NEG_LOG2E = -LOG2E


def _sigmoid(x):
    return 1.0 / (1.0 + jnp.exp2(x * NEG_LOG2E))


def _silu(x):
    half = 0.5 * x
    return half + half * jnp.tanh(half)


def _log_sigmoid(x):
    return jnp.minimum(x, 0.0) - jnp.log(1.0 + jnp.exp(-jnp.abs(x)))


PREP_TN = 1024
GATE_COLS = 2 * MLSTM_HEADS
GATE_COL0 = 4 * D_MODEL + 2 * MLSTM_HEADS * MLSTM_QK_DIM + MLSTM_HEADS * MLSTM_V_DIM
GATE_STEP = GATE_COL0 // PREP_TN
LANES = 128


def _weight_prep_kernel(a_ref, b_ref, w_ref, wg_ref):
    n = pl.program_id(0)

    @pl.when(n < GATE_STEP)
    def _():
        w_ref[...] = a_ref[...].astype(BF16)

    @pl.when(n >= GATE_STEP)
    def _():
        w_ref[...] = jnp.concatenate([a_ref[GATE_COLS:, :], b_ref[...]], axis=0).astype(BF16)

    @pl.when(n == GATE_STEP)
    def _():
        lane = lax.broadcasted_iota(jnp.int32, (D_MODEL, LANES), 1)
        wg_ref[...] = jnp.where(lane < GATE_COLS, a_ref[:LANES, :].T, 0.0).astype(BF16)


def _weight_prep(w_t):
    in_width, d = w_t.shape
    n_rows = in_width - GATE_COLS
    assert d == D_MODEL and GATE_COL0 % PREP_TN == 0 and n_rows % PREP_TN == 0 and GATE_LANES == LANES
    return pl.pallas_call(
        _weight_prep_kernel,
        grid=(n_rows // PREP_TN,),
        in_specs=[
            pl.BlockSpec((PREP_TN, D_MODEL), lambda n: (n, 0)),
            pl.BlockSpec((GATE_COLS, D_MODEL), lambda n: ((n + 1) * (PREP_TN // GATE_COLS), 0)),
        ],
        out_specs=[
            pl.BlockSpec((PREP_TN, D_MODEL), lambda n: (n, 0)),
            pl.BlockSpec((D_MODEL, LANES), lambda n: (0, 0)),
        ],
        out_shape=[
            jax.ShapeDtypeStruct((n_rows, D_MODEL), BF16),
            jax.ShapeDtypeStruct((D_MODEL, LANES), BF16),
        ],
        compiler_params=_cparams(("arbitrary",)),
        name="weight_prep",
    )(w_t, w_t)


IN_TM = 1024
IN_TN = 1024
QKV_STEPS = QKV_COLS // IN_TN


CONV_STEP0 = QKV_STEPS + S_MQK * (D_MODEL // IN_TN)
CONV_STEPS = D_MODEL // IN_TN
CONV_HALO = 8


def _in_proj_kernel(x_ref, g_ref, w_ref, wg_ref, cw_ref, cb_ref, qkv_ref, p_ref, gt_ref,
                    hn_ref, halo_ref, acc_ref, *, tiles_per_seq):
    i = pl.program_id(0)
    n = pl.program_id(1)

    @pl.when((i == 0) & (n == 0))
    def _():
        halo_ref[...] = jnp.zeros_like(halo_ref)

    @pl.when(n == 0)
    def _():
        x = x_ref[...]
        ms = jnp.mean(x * x, axis=-1, keepdims=True)
        hn = (x * lax.rsqrt(ms + NORM_EPS) * g_ref[...]).astype(BF16)
        hn_ref[...] = hn
        gates = jnp.dot(hn, wg_ref[...], preferred_element_type=F32)
        gates_t = gates.T
        for c in range(IN_TM // CHUNK):
            gt_ref[c] = gates_t[:2 * MLSTM_HEADS, c * CHUNK:(c + 1) * CHUNK]

    def slab_acc(j):
        return lax.dot_general(hn_ref[...], w_ref[j * SLAB:(j + 1) * SLAB, :], (((1,), (1,)), ((), ())),
                               preferred_element_type=F32)

    @pl.when(n < QKV_STEPS)
    def _():
        scale = jnp.where(n < D_MODEL // IN_TN, ATTN_HEAD_DIM ** -0.5 * LOG2E, 1.0).astype(F32)
        for j in range(IN_TN // SLAB):
            acc = slab_acc(j) * scale
            qkv_ref[2 * j] = acc[:, :ATTN_HEAD_DIM]
            qkv_ref[2 * j + 1] = acc[:, ATTN_HEAD_DIM:]

    is_conv = (n >= CONV_STEP0) & (n < CONV_STEP0 + CONV_STEPS)

    @pl.when((n >= QKV_STEPS) & jnp.logical_not(is_conv))
    def _():
        for j in range(IN_TN // SLAB):
            p_ref[j] = slab_acc(j).astype(BF16)

    @pl.when(is_conv)
    def _():
        step = n - CONV_STEP0
        seq_start = (i % tiles_per_seq) == 0
        post = jnp.where(step == 1, MLSTM_QK_DIM ** -0.5, 1.0).astype(F32)

        def conv_slab(j):
            cols = slice(j * SLAB, (j + 1) * SLAB)
            slot = (n + j) % 2
            acc_ref[slot, 0:CONV_HALO, :] = jnp.where(seq_start, 0.0, halo_ref[step, :, cols])
            halo_ref[step, :, cols] = acc_ref[slot, IN_TM:IN_TM + CONV_HALO, :]
            y = cb_ref[:, cols]
            for back in range(CONV_WIDTH):
                tap = acc_ref[slot, CONV_HALO - back:CONV_HALO - back + IN_TM, :]
                y = y + cw_ref[CONV_WIDTH - 1 - back:CONV_WIDTH - back, cols] * tap
            p_ref[j] = (_silu(y) * post).astype(BF16)

        n_slabs = IN_TN // SLAB
        for j in range(n_slabs):
            acc_ref[(n + j) % 2, CONV_HALO:, :] = slab_acc(j)
            if j > 0:
                conv_slab(j - 1)
        conv_slab(n_slabs - 1)


def _in_proj(x2, norm_g, w_big, w_gate, conv_w, conv_b, seq):
    t = x2.shape[0]
    n_cols = w_big.shape[0]
    grid = (t // IN_TM, n_cols // IN_TN)
    assert seq % IN_TM == 0 and CONV_STEPS * IN_TN == conv_w.shape[1]

    def conv_step(i, n):
        return (0, jnp.clip(n - CONV_STEP0, 0, CONV_STEPS - 1))

    return pl.pallas_call(
        functools.partial(_in_proj_kernel, tiles_per_seq=seq // IN_TM),
        grid=grid,
        in_specs=[
            pl.BlockSpec((IN_TM, D_MODEL), lambda i, n: (i, 0)),
            pl.BlockSpec((1, D_MODEL), lambda i, n: (0, 0)),
            pl.BlockSpec((IN_TN, D_MODEL), lambda i, n: (n, 0)),
            pl.BlockSpec((D_MODEL, GATE_LANES), lambda i, n: (0, 0)),
            pl.BlockSpec((CONV_WIDTH, IN_TN), conv_step),
            pl.BlockSpec((1, IN_TN), conv_step),
        ],
        out_specs=[
            pl.BlockSpec((IN_TN // ATTN_HEAD_DIM, IN_TM, ATTN_HEAD_DIM),
                         lambda i, n: (jnp.minimum(n, QKV_STEPS - 1), i, 0)),
            pl.BlockSpec((IN_TN // SLAB, IN_TM, SLAB),
                         lambda i, n: (jnp.maximum(n - QKV_STEPS, 0), i, 0)),
            pl.BlockSpec((IN_TM // CHUNK, 2 * MLSTM_HEADS, CHUNK), lambda i, n: (i, 0, 0)),
        ],
        out_shape=[
            jax.ShapeDtypeStruct((QKV_COLS // ATTN_HEAD_DIM, t, ATTN_HEAD_DIM), F32),
            jax.ShapeDtypeStruct(((n_cols - QKV_COLS) // SLAB, t, SLAB), BF16),
            jax.ShapeDtypeStruct((t // CHUNK, 2 * MLSTM_HEADS, CHUNK), F32),
        ],
        scratch_shapes=[pltpu.VMEM((IN_TM, D_MODEL), BF16),
                        pltpu.VMEM((CONV_STEPS, CONV_HALO, IN_TN), F32),
                        pltpu.VMEM((2, CONV_HALO + IN_TM, SLAB), F32)],
        compiler_params=_cparams(("arbitrary", "arbitrary")),
        name="in_proj",
    )(x2, norm_g, w_big, w_gate, conv_w, conv_b)


def _band_bias(slope, dilation, first):
    qi = lax.broadcasted_iota(jnp.int32, (BAND, 2 * BAND), 0)
    ki = lax.broadcasted_iota(jnp.int32, (BAND, 2 * BAND), 1)
    dist = BAND + qi - ki
    valid = (dist >= 0) & (dist <= BAND)
    if first:
        valid = valid & (ki >= BAND)
    return jnp.where(valid, -slope * (dist * dilation).astype(F32), MASKED)


def _dense_bias(slope, first):
    qi = lax.broadcasted_iota(jnp.int32, (BAND, 2 * BAND), 0)
    ki = lax.broadcasted_iota(jnp.int32, (BAND, 2 * BAND), 1)
    pq, jq = qi // 32, qi % 32
    half, pk, jk = ki // BAND, (ki % BAND) // 32, ki % 32
    dist = 4 * (jq - jk + 32 * (1 - half)) + pq - pk
    valid = (dist >= 0) & (dist <= BAND)
    if first:
        valid = valid & (half == 1)
    return jnp.where(valid, -slope * dist.astype(F32), MASKED)


def _wide_bias(slope):
    qi = lax.broadcasted_iota(jnp.int32, (2 * BAND, 2 * BAND), 0)
    ki = lax.broadcasted_iota(jnp.int32, (2 * BAND, 2 * BAND), 1)
    dist = qi - ki
    valid = (dist >= 0) & (dist <= BAND)
    return jnp.where(valid, -slope * (dist * 16).astype(F32), MASKED)


def _softmax_block(q, k, v, bias):
    s = lax.dot_general(q, k, (((1,), (1,)), ((), ())), preferred_element_type=F32) + bias
    m = jnp.max(s, axis=-1, keepdims=True)
    p = jnp.exp2(s - m).astype(BF16)
    v_ext = jnp.concatenate([v, jnp.ones_like(v)], axis=1)
    r = jnp.dot(p, v_ext, preferred_element_type=F32)
    hd = ATTN_HEAD_DIM
    return r[:, :hd], r[:, hd:], jnp.broadcast_to(m, (q.shape[0], hd))


def _attn_kernel(q_ref, k_ref, v_ref, o_ref,
                 x4_ref, x16_ref, tmp_ref, num_ref, den_ref, mx_ref, bias_ref, bias16_ref, onat_ref):
    seq = q_ref.shape[0]
    n_u4, n_u16 = seq // 4, seq // 16
    slope = jnp.exp2(jnp.full((1, 1), -8.0 / ATTN_HEADS, F32) * (pl.program_id(1) + 1)) * LOG2E
    bias_ref[0, 0] = _dense_bias(slope, False)
    bias_ref[0, 1] = _dense_bias(slope, True)
    bias_ref[1, 0] = _band_bias(slope, 4, False)
    bias_ref[1, 1] = _band_bias(slope, 4, True)
    bias16_ref[...] = _wide_bias(slope)

    for ti, src in enumerate((q_ref, k_ref, v_ref)):
        def to_phase4(piece, carry, ti=ti, src=src):
            dst = pl.ds(pl.multiple_of(piece * BAND, BAND), BAND)
            for p4 in range(4):
                rows = src[pl.ds(p4 + 4 * BAND * piece, BAND, stride=4), :]
                tmp_ref[p4, dst, :] = rows
                x4_ref[ti, p4, dst, :] = rows.astype(BF16)
            return carry

        lax.fori_loop(0, n_u4 // BAND, to_phase4, 0)

        def to_phase16(piece, carry, ti=ti):
            dst = pl.ds(pl.multiple_of(piece * BAND, BAND), BAND)
            for p16 in range(16):
                rows = tmp_ref[p16 % 4, pl.ds(p16 // 4 + 4 * BAND * piece, BAND, stride=4), :]
                x16_ref[ti, p16, dst, :] = rows.astype(BF16)
            return carry

        lax.fori_loop(0, n_u16 // BAND, to_phase16, 0)

    def store(pattern, p, rows, result):
        num, den, mx = result
        num_ref[pattern, p, rows, :] = num
        den_ref[pattern, p, rows, :] = den
        mx_ref[pattern, p, rows, :] = mx

    def dense_block(n):
        cur, prev = slice(n * 32, (n + 1) * 32), slice(max(n - 1, 0) * 32, max(n - 1, 0) * 32 + 32)

        def rows(ti, sl):
            return [x4_ref[ti, p, sl, :] for p in range(4)]

        q = jnp.concatenate(rows(0, cur), axis=0)
        k = jnp.concatenate(rows(1, prev) + rows(1, cur), axis=0)
        v = jnp.concatenate(rows(2, prev) + rows(2, cur), axis=0)
        num, den, mx = _softmax_block(q, k, v, bias_ref[0, int(n == 0)])
        for p in range(4):
            part = slice(p * 32, (p + 1) * 32)
            store(0, p, cur, (num[part], den[part], mx[part]))

    def band_block(p, n):
        cur = slice(n * BAND, (n + 1) * BAND)
        prev = slice(max(n - 1, 0) * BAND, max(n - 1, 0) * BAND + BAND)
        q = x4_ref[0, p, cur, :]
        k = jnp.concatenate([x4_ref[1, p, prev, :], x4_ref[1, p, cur, :]], axis=0)
        v = jnp.concatenate([x4_ref[2, p, prev, :], x4_ref[2, p, cur, :]], axis=0)
        store(1, p, cur, _softmax_block(q, k, v, bias_ref[1, int(n == 0)]))

    def wide_block(p16, half):
        part = slice(half * BAND, (half + 1) * BAND)
        result = _softmax_block(x16_ref[0, p16, part, :], x16_ref[1, p16], x16_ref[2, p16], bias16_ref[part, :])
        store(2, p16 % 4, pl.ds(p16 // 4 + 4 * BAND * half, BAND, stride=4), result)

    def merge_block(p, n):
        rows = slice(n * BAND, (n + 1) * BAND)
        m0, m1, m2 = mx_ref[0, p, rows, :], mx_ref[1, p, rows, :], mx_ref[2, p, rows, :]
        m_all = jnp.maximum(jnp.maximum(m0, m1), m2)
        w0, w1, w2 = jnp.exp2(m0 - m_all), jnp.exp2(m1 - m_all), jnp.exp2(m2 - m_all)
        num = w0 * num_ref[0, p, rows, :] + w1 * num_ref[1, p, rows, :] + w2 * num_ref[2, p, rows, :]
        den = w0 * den_ref[0, p, rows, :] + w1 * den_ref[1, p, rows, :] + w2 * den_ref[2, p, rows, :]
        onat_ref[pl.ds(p + 4 * BAND * n, BAND, stride=4), :] = num / den

    halves = 2
    dense_per_half = n_u4 // 32 // halves
    band_per_half = n_u4 // BAND // halves
    for half in range(halves):
        for n in range(half * dense_per_half, (half + 1) * dense_per_half):
            dense_block(n)
        for n in range(half * band_per_half, (half + 1) * band_per_half):
            for p in range(4):
                band_block(p, n)
                if half > 0:
                    merge_block(p, n - band_per_half)
        for p16 in range(16):
            wide_block(p16, half)
    for n in range(band_per_half, 2 * band_per_half):
        for p in range(4):
            merge_block(p, n)
    o_ref[...] = onat_ref[...].astype(o_ref.dtype)


def _attention(qkv, batch, seq):
    assert seq // 16 == 2 * BAND
    hd = ATTN_HEAD_DIM
    qkv4 = qkv.reshape(3 * ATTN_HEADS, batch, seq, hd)

    def spec(which):
        return pl.BlockSpec((None, None, seq, hd), lambda b, h: (which * ATTN_HEADS + h, b, 0, 0))

    out = pl.pallas_call(
        _attn_kernel,
        grid=(batch, ATTN_HEADS),
        in_specs=[spec(0), spec(1), spec(2)],
        out_specs=pl.BlockSpec((None, None, seq, hd), lambda b, h: (h, b, 0, 0)),
        out_shape=jax.ShapeDtypeStruct((ATTN_HEADS, batch, seq, hd), BF16),
        scratch_shapes=[
            pltpu.VMEM((3, 4, seq // 4, hd), BF16),
            pltpu.VMEM((3, 16, seq // 16, hd), BF16),
            pltpu.VMEM((4, seq // 4, hd), F32),
            pltpu.VMEM((3, 4, seq // 4, hd), F32),
            pltpu.VMEM((3, 4, seq // 4, hd), F32),
            pltpu.VMEM((3, 4, seq // 4, hd), F32),
            pltpu.VMEM((2, 2, BAND, 2 * BAND), F32),
            pltpu.VMEM((2 * BAND, 2 * BAND), F32),
            pltpu.VMEM((seq, hd), F32),
        ],
        compiler_params=_cparams(("parallel", "parallel")),
        name="dilated_attention",
    )(qkv4, qkv4, qkv4)
    return out.reshape(ATTN_HEADS, batch * seq, hd)


MLSTM_TS = 512
ONES_LANES = 128


def _mlstm_kernel(bias_ref, q_ref, k_ref, v_ref, gt_ref, out_ref, c_ref, m_ref, kt_ref,
                  lhs_ref, kw_ref, floor_ref):
    s = pl.program_id(1)
    dk, dv = MLSTM_QK_DIM, MLSTM_V_DIM
    L = CHUNK

    @pl.when(s == 0)
    def _():
        c_ref[...] = jnp.zeros_like(c_ref)
        m_ref[...] = jnp.zeros_like(m_ref)

    ti = lax.broadcasted_iota(jnp.int32, (CHUNK, CHUNK), 0)
    si = lax.broadcasted_iota(jnp.int32, (CHUNK, CHUNK), 1)
    causal = ti >= si
    csum = jnp.where(ti <= si, 1.0, 0.0).astype(F32)

    lane = lax.broadcasted_iota(jnp.int32, (MLSTM_HEADS, L), 1)

    for h in range(MLSTM_HEADS):
        for c in range(MLSTM_TS // L):
            k_hc = k_ref[h // 2, c * L:(c + 1) * L, (h % 2) * dk:(h % 2 + 1) * dk]
            kt_ref[h, c] = k_hc.astype(F32).T.astype(BF16)

    n_chunks = MLSTM_TS // L
    heads = range(MLSTM_HEADS)

    def q_of(h, c):
        return q_ref[h // 2, c * L:(c + 1) * L, (h % 2) * dk:(h % 2 + 1) * dk]

    def v_ext_of(h, c):
        return jnp.concatenate([v_ref[h, c * L:(c + 1) * L, :], jnp.ones((L, ONES_LANES), BF16)], axis=1)

    w_olds = []
    for c in range(n_chunks):
        gates = gt_ref[c] + bias_ref[...]
        ig_all = gates[:MLSTM_HEADS]
        lf_all = _log_sigmoid(gates[MLSTM_HEADS:])
        b_all = jnp.dot(lf_all, csum, preferred_element_type=F32,
                        precision=lax.Precision.HIGHEST)
        c_all = ig_all - b_all
        cm_all = c_all
        shift = 1
        while shift < L:
            cm_all = jnp.maximum(cm_all, jnp.where(lane >= shift, pltpu.roll(cm_all, shift, axis=1), MASKED))
            shift *= 2
        m_prev = m_ref[...]
        b_last = b_all[:, L - 1:L]
        top = jnp.maximum(m_prev, cm_all)
        m_new = b_last + top[:, L - 1:L]
        w_old = jnp.exp(b_last + m_prev - m_new)
        w_key = jnp.exp(b_last + c_all - m_new)
        cols = jnp.concatenate([
            -top,
            jnp.exp(m_prev - top),
            jnp.exp(-(b_all + top)),
        ], axis=0).T
        m_ref[...] = m_new
        w_olds.append(w_old)

        def column(group, h):
            j = group * MLSTM_HEADS + h
            return jnp.broadcast_to(cols[:, j:j + 1], (L, L))

        scores = [jnp.dot(q_of(h, c), kt_ref[h, c], preferred_element_type=F32) for h in heads]
        for h in heads:
            w_intra = jnp.exp(jnp.where(causal, column(0, h) + c_all[h:h + 1], MASKED))
            lhs_ref[c, h] = jnp.concatenate([(scores[h] * w_intra).astype(BF16),
                                             (q_of(h, c).astype(F32) * column(1, h)).astype(BF16)], axis=1)
            kw_ref[c, h] = (kt_ref[h, c].astype(F32) * w_key[h:h + 1]).astype(BF16)
            floor_ref[c, h] = column(2, h)

    for c in range(n_chunks):
        incs = [jnp.dot(kw_ref[c, h], v_ext_of(h, c), preferred_element_type=F32) for h in heads]
        nds = [jnp.dot(lhs_ref[c, h], jnp.concatenate([v_ext_of(h, c), c_ref[h].astype(BF16)], axis=0),
                       preferred_element_type=F32) for h in heads]
        for h in heads:
            nd = nds[h]
            inv = 1.0 / jnp.maximum(jnp.abs(nd[:, dv:]), floor_ref[c, h])
            for half in range(dv // L):
                lanes = slice(half * L, (half + 1) * L)
                out_ref[h, c * L:(c + 1) * L, lanes] = (nd[:, lanes] * inv).astype(out_ref.dtype)
            c_ref[h] = w_olds[c][h:h + 1] * c_ref[h] + incs[h]


def _mlstm(slabs, gates_t, b_if, batch, seq):
    n_slabs = slabs.shape[0]
    s4d = slabs.reshape(n_slabs, batch, seq, SLAB)
    half = MLSTM_HEADS // 2
    tiles = seq // MLSTM_TS
    tile_chunks = MLSTM_TS // CHUNK

    def tile_spec(first_slab, n):
        return pl.BlockSpec((n, None, MLSTM_TS, SLAB), lambda b, s: (first_slab // n, b, s, 0))

    out = pl.pallas_call(
        _mlstm_kernel,
        grid=(batch, tiles),
        in_specs=[
            pl.BlockSpec((2 * MLSTM_HEADS, 1), lambda b, s: (0, 0)),
            tile_spec(S_MQK * GROUP_SLABS, half), tile_spec(S_MQK * GROUP_SLABS + half, half),
            tile_spec(S_MV * GROUP_SLABS, MLSTM_HEADS),
            pl.BlockSpec((tile_chunks, 2 * MLSTM_HEADS, CHUNK), lambda b, s: (b * tiles + s, 0, 0)),
        ],
        out_specs=pl.BlockSpec((MLSTM_HEADS, None, MLSTM_TS, SLAB), lambda b, s: (0, b, s, 0)),
        out_shape=jax.ShapeDtypeStruct((MLSTM_HEADS, batch, seq, SLAB), BF16),
        scratch_shapes=[
            pltpu.VMEM((MLSTM_HEADS, MLSTM_QK_DIM, MLSTM_V_DIM + ONES_LANES), F32),
            pltpu.VMEM((MLSTM_HEADS, 1), F32),
            pltpu.VMEM((MLSTM_HEADS, tile_chunks, MLSTM_QK_DIM, CHUNK), BF16),
            pltpu.VMEM((tile_chunks, MLSTM_HEADS, CHUNK, CHUNK + MLSTM_QK_DIM), BF16),
            pltpu.VMEM((tile_chunks, MLSTM_HEADS, MLSTM_QK_DIM, CHUNK), BF16),
            pltpu.VMEM((tile_chunks, MLSTM_HEADS, CHUNK, CHUNK), F32),
        ],
        compiler_params=_cparams(("parallel", "arbitrary")),
        name="mlstm",
    )(b_if.reshape(-1, 1), s4d, s4d, s4d, gates_t)
    return out.reshape(MLSTM_HEADS, batch * seq, SLAB)


MERGE_TM = 512
OUT_TM = 512


def _weight_spec():
    return pl.BlockSpec((D_MODEL, D_MODEL), lambda i: (0, 0), pipeline_mode=pl.Buffered(1))


def _branch_merge_kernel(attn_ref, az_ref, cell_ref, mo_ref, mz_ref, ga_ref, gm_ref,
                         wa_ref, wm_ref, ng_ref, merged_ref):
    n = GROUP_SLABS
    gated = []
    for c in range(n):
        attn = jnp.concatenate([attn_ref[2 * c].astype(F32), attn_ref[2 * c + 1].astype(F32)], axis=1)
        gated.append(attn * _silu(az_ref[c].astype(F32)))
    gated = jnp.concatenate(gated, axis=1).astype(BF16)
    y_a, mem = [], []
    for h in range(MLSTM_HEADS):
        cols = slice(h * SLAB, (h + 1) * SLAB)
        y_a.append(jnp.dot(gated, wa_ref[:, cols], preferred_element_type=F32))
        cell = _sigmoid(mo_ref[h].astype(F32)) * cell_ref[h].astype(F32)
        cell = cell * lax.rsqrt(jnp.mean(cell * cell, axis=-1, keepdims=True) + NORM_EPS)
        cell = cell * ng_ref[:, cols]
        mem.append(cell * _silu(mz_ref[h].astype(F32)))
    mem = jnp.concatenate(mem, axis=1).astype(BF16)
    for c in range(n):
        cols = slice(c * SLAB, (c + 1) * SLAB)
        y_m = jnp.dot(mem, wm_ref[:, cols], preferred_element_type=F32)
        merged = _sigmoid(ga_ref[c].astype(F32)) * y_a[c] + _sigmoid(gm_ref[c].astype(F32)) * y_m
        merged_ref[:, cols] = merged.astype(merged_ref.dtype)


def _branch_merge(attn, slabs, cell, w_a, w_m, norm_g):
    t = attn.shape[1]
    assert MLSTM_V_DIM == SLAB and MLSTM_HEADS == GROUP_SLABS

    def slab_spec(group):
        return pl.BlockSpec((GROUP_SLABS, MERGE_TM, SLAB), lambda i: (group, i, 0))

    return pl.pallas_call(
        _branch_merge_kernel,
        grid=(t // MERGE_TM,),
        in_specs=[
            pl.BlockSpec((ATTN_HEADS, MERGE_TM, ATTN_HEAD_DIM), lambda i: (0, i, 0)),
            slab_spec(S_AZ), slab_spec(0), slab_spec(S_MO), slab_spec(S_MZ),
            slab_spec(S_GA), slab_spec(S_GM),
            _weight_spec(), _weight_spec(),
            pl.BlockSpec((1, D_MODEL), lambda i: (0, 0)),
        ],
        out_specs=pl.BlockSpec((MERGE_TM, D_MODEL), lambda i: (i, 0)),
        out_shape=jax.ShapeDtypeStruct((t, D_MODEL), BF16),
        compiler_params=_cparams(("parallel",)),
        name="branch_merge",
    )(attn, slabs, cell, slabs, slabs, slabs, slabs, w_a, w_m, norm_g)


def _out_proj_kernel(merged_ref, x_ref, wo_ref, fg_ref, out_ref):
    y = x_ref[...] + jnp.dot(merged_ref[...], wo_ref[...], preferred_element_type=F32)
    y = y * lax.rsqrt(jnp.mean(y * y, axis=-1, keepdims=True) + NORM_EPS)
    out_ref[...] = y * fg_ref[...]


def _out_proj(merged, x2, w_o, final_g):
    t = x2.shape[0]
    return pl.pallas_call(
        _out_proj_kernel,
        grid=(t // OUT_TM,),
        in_specs=[
            pl.BlockSpec((OUT_TM, D_MODEL), lambda i: (i, 0)),
            pl.BlockSpec((OUT_TM, D_MODEL), lambda i: (i, 0)),
            _weight_spec(),
            pl.BlockSpec((1, D_MODEL), lambda i: (0, 0)),
        ],
        out_specs=pl.BlockSpec((OUT_TM, D_MODEL), lambda i: (i, 0)),
        out_shape=jax.ShapeDtypeStruct((t, D_MODEL), F32),
        compiler_params=_cparams(("parallel",)),
        name="out_proj",
    )(merged, x2, w_o, final_g)


def kernel(x, norm_g, w_in, b_if, conv_w, conv_b, mlstm_norm_g, w_attn_branch, w_mlstm_branch,
           w_out, final_norm_g):
    batch, seq, d = x.shape
    assert d == D_MODEL and seq % (16 * BAND) == 0 and (batch * seq) % IN_TM == 0
    t = batch * seq
    x2 = x.reshape(t, d)

    w_big, w_gate = _weight_prep(w_in.T)
    assert w_big.shape[0] == QKV_COLS + N_SLAB_GROUPS * D_MODEL

    qkv, slabs, gates_t = _in_proj(x2, norm_g.reshape(1, d), w_big, w_gate, conv_w,
                                   conv_b.reshape(1, -1), seq)
    attn = _attention(qkv, batch, seq)
    cell = _mlstm(slabs, gates_t, b_if, batch, seq)
    merged = _branch_merge(attn, slabs, cell, w_attn_branch.astype(BF16), w_mlstm_branch.astype(BF16),
                           mlstm_norm_g.reshape(1, d))
    out = _out_proj(merged, x2, w_out.astype(BF16), final_norm_g.reshape(1, d))
    return out.reshape(batch, seq, d)
```

```python
import functools

import jax
import jax.numpy as jnp
from jax import lax
from jax.experimental import pallas as pl
from jax.experimental.pallas import tpu as pltpu

F32 = jnp.float32
BF16 = jnp.bfloat16

D_MODEL = 2048
ATTN_HEADS = 16
ATTN_HEAD_DIM = 128
MLSTM_HEADS = 8
MLSTM_QK_DIM = 128
MLSTM_V_DIM = 256
CONV_WIDTH = 4
NORM_EPS = 1e-6
BAND = 128
ATTN_PARTS = 2
SLAB = 256
GROUP_SLABS = D_MODEL // SLAB
QKV_COLS = 3 * D_MODEL
S_AZ, S_MQK, S_MV, S_MO, S_MZ, S_GA, S_GM = range(7)
N_SLAB_GROUPS = 7
GATE_LANES = 128
MASKED = -1e30
CHUNK = 128
VMEM_LIMIT = 56 * 1024 * 1024


def _cparams(sem):
    return pltpu.CompilerParams(dimension_semantics=sem, vmem_limit_bytes=VMEM_LIMIT)


LOG2E = 1.4426950408889634
NEG_LOG2E = -LOG2E


def _sigmoid(x):
    return 1.0 / (1.0 + jnp.exp2(x * NEG_LOG2E))


def _silu(x):
    half = 0.5 * x
    return half + half * jnp.tanh(half)


def _log_sigmoid(x):
    return jnp.minimum(x, 0.0) - jnp.log(1.0 + jnp.exp(-jnp.abs(x)))


PREP_TN = 1024
GATE_COLS = 2 * MLSTM_HEADS
GATE_COL0 = 4 * D_MODEL + 2 * MLSTM_HEADS * MLSTM_QK_DIM + MLSTM_HEADS * MLSTM_V_DIM
GATE_STEP = GATE_COL0 // PREP_TN
LANES = 128


def _weight_prep_kernel(a_ref, b_ref, w_ref, wg_ref):
    n = pl.program_id(0)

    @pl.when(n < GATE_STEP)
    def _():
        w_ref[...] = a_ref[...].astype(BF16)

    @pl.when(n >= GATE_STEP)
    def _():
        w_ref[...] = jnp.concatenate([a_ref[GATE_COLS:, :], b_ref[...]], axis=0).astype(BF16)

    @pl.when(n == GATE_STEP)
    def _():
        lane = lax.broadcasted_iota(jnp.int32, (D_MODEL, LANES), 1)
        wg_ref[...] = jnp.where(lane < GATE_COLS, a_ref[:LANES, :].T, 0.0).astype(BF16)


def _weight_prep(w_t):
    in_width, d = w_t.shape
    n_rows = in_width - GATE_COLS
    assert d == D_MODEL and GATE_COL0 % PREP_TN == 0 and n_rows % PREP_TN == 0 and GATE_LANES == LANES
    return pl.pallas_call(
        _weight_prep_kernel,
        grid=(n_rows // PREP_TN,),
        in_specs=[
            pl.BlockSpec((PREP_TN, D_MODEL), lambda n: (n, 0)),
            pl.BlockSpec((GATE_COLS, D_MODEL), lambda n: ((n + 1) * (PREP_TN // GATE_COLS), 0)),
        ],
        out_specs=[
            pl.BlockSpec((PREP_TN, D_MODEL), lambda n: (n, 0)),
            pl.BlockSpec((D_MODEL, LANES), lambda n: (0, 0)),
        ],
        out_shape=[
            jax.ShapeDtypeStruct((n_rows, D_MODEL), BF16),
            jax.ShapeDtypeStruct((D_MODEL, LANES), BF16),
        ],
        compiler_params=_cparams(("arbitrary",)),
        name="weight_prep",
    )(w_t, w_t)


IN_TM = 1024
IN_TN = 1024
QKV_STEPS = QKV_COLS // IN_TN


CONV_STEP0 = QKV_STEPS + S_MQK * (D_MODEL // IN_TN)
CONV_STEPS = D_MODEL // IN_TN
CONV_HALO = 8


def _in_proj_kernel(x_ref, g_ref, w_ref, wg_ref, cw_ref, cb_ref, qkv_ref, p_ref, gt_ref,
                    hn_ref, halo_ref, acc_ref, stage_ref, *, tiles_per_seq):
    i = pl.program_id(0)
    n = pl.program_id(1)

    @pl.when((i == 0) & (n == 0))
    def _():
        halo_ref[...] = jnp.zeros_like(halo_ref)

    @pl.when(n == 0)
    def _():
        x = x_ref[...]
        ms = jnp.mean(x * x, axis=-1, keepdims=True)
        hn = (x * lax.rsqrt(ms + NORM_EPS) * g_ref[...]).astype(BF16)
        hn_ref[...] = hn
        gates = jnp.dot(hn, wg_ref[...], preferred_element_type=F32)
        gates_t = gates.T
        for c in range(IN_TM // CHUNK):
            gt_ref[c] = gates_t[:2 * MLSTM_HEADS, c * CHUNK:(c + 1) * CHUNK]

    def slab_acc(j):
        return lax.dot_general(hn_ref[...], w_ref[j * SLAB:(j + 1) * SLAB, :], (((1,), (1,)), ((), ())),
                               preferred_element_type=F32)

    @pl.when(n < QKV_STEPS)
    def _():
        scale = jnp.where(n < D_MODEL // IN_TN, ATTN_HEAD_DIM ** -0.5 * LOG2E, 1.0).astype(F32)
        quarter = IN_TM // 4
        for j in range(IN_TN // SLAB):
            acc = slab_acc(j) * scale
            for hh in range(SLAB // ATTN_HEAD_DIM):
                slot = (2 * j + hh) % 2
                stage_ref[slot] = acc[:, hh * ATTN_HEAD_DIM:(hh + 1) * ATTN_HEAD_DIM]
                for p in range(4):
                    qkv_ref[2 * j + hh, p * quarter:(p + 1) * quarter, :] = (
                        stage_ref[slot, pl.ds(p, quarter, stride=4), :])

    is_conv = (n >= CONV_STEP0) & (n < CONV_STEP0 + CONV_STEPS)

    @pl.when((n >= QKV_STEPS) & jnp.logical_not(is_conv))
    def _():
        for j in range(IN_TN // SLAB):
            p_ref[j] = slab_acc(j).astype(BF16)

    @pl.when(is_conv)
    def _():
        step = n - CONV_STEP0
        seq_start = (i % tiles_per_seq) == 0
        post = jnp.where(step == 1, MLSTM_QK_DIM ** -0.5, 1.0).astype(F32)

        def conv_slab(j):
            cols = slice(j * SLAB, (j + 1) * SLAB)
            slot = (n + j) % 2
            acc_ref[slot, 0:CONV_HALO, :] = jnp.where(seq_start, 0.0, halo_ref[step, :, cols])
            halo_ref[step, :, cols] = acc_ref[slot, IN_TM:IN_TM + CONV_HALO, :]
            y = cb_ref[:, cols]
            for back in range(CONV_WIDTH):
                tap = acc_ref[slot, CONV_HALO - back:CONV_HALO - back + IN_TM, :]
                y = y + cw_ref[CONV_WIDTH - 1 - back:CONV_WIDTH - back, cols] * tap
            p_ref[j] = (_silu(y) * post).astype(BF16)

        n_slabs = IN_TN // SLAB
        for j in range(n_slabs):
            acc_ref[(n + j) % 2, CONV_HALO:, :] = slab_acc(j)
            if j > 0:
                conv_slab(j - 1)
        conv_slab(n_slabs - 1)


def _in_proj(x2, norm_g, w_big, w_gate, conv_w, conv_b, seq):
    t = x2.shape[0]
    n_cols = w_big.shape[0]
    grid = (t // IN_TM, n_cols // IN_TN)
    assert seq % IN_TM == 0 and CONV_STEPS * IN_TN == conv_w.shape[1]

    def conv_step(i, n):
        return (0, jnp.clip(n - CONV_STEP0, 0, CONV_STEPS - 1))

    return pl.pallas_call(
        functools.partial(_in_proj_kernel, tiles_per_seq=seq // IN_TM),
        grid=grid,
        in_specs=[
            pl.BlockSpec((IN_TM, D_MODEL), lambda i, n: (i, 0)),
            pl.BlockSpec((1, D_MODEL), lambda i, n: (0, 0)),
            pl.BlockSpec((IN_TN, D_MODEL), lambda i, n: (n, 0)),
            pl.BlockSpec((D_MODEL, GATE_LANES), lambda i, n: (0, 0)),
            pl.BlockSpec((CONV_WIDTH, IN_TN), conv_step),
            pl.BlockSpec((1, IN_TN), conv_step),
        ],
        out_specs=[
            pl.BlockSpec((IN_TN // ATTN_HEAD_DIM, IN_TM, ATTN_HEAD_DIM),
                         lambda i, n: (jnp.minimum(n, QKV_STEPS - 1), i, 0)),
            pl.BlockSpec((IN_TN // SLAB, IN_TM, SLAB),
                         lambda i, n: (jnp.maximum(n - QKV_STEPS, 0), i, 0)),
            pl.BlockSpec((IN_TM // CHUNK, 2 * MLSTM_HEADS, CHUNK), lambda i, n: (i, 0, 0)),
        ],
        out_shape=[
            jax.ShapeDtypeStruct((QKV_COLS // ATTN_HEAD_DIM, t, ATTN_HEAD_DIM), F32),
            jax.ShapeDtypeStruct(((n_cols - QKV_COLS) // SLAB, t, SLAB), BF16),
            jax.ShapeDtypeStruct((t // CHUNK, 2 * MLSTM_HEADS, CHUNK), F32),
        ],
        scratch_shapes=[pltpu.VMEM((IN_TM, D_MODEL), BF16),
                        pltpu.VMEM((CONV_STEPS, CONV_HALO, IN_TN), F32),
                        pltpu.VMEM((2, CONV_HALO + IN_TM, SLAB), F32),
                        pltpu.VMEM((2, IN_TM, ATTN_HEAD_DIM), F32)],
        compiler_params=_cparams(("arbitrary", "arbitrary")),
        name="in_proj",
    )(x2, norm_g, w_big, w_gate, conv_w, conv_b)


def _band_bias(slope, dilation, first):
    qi = lax.broadcasted_iota(jnp.int32, (BAND, 2 * BAND), 0)
    ki = lax.broadcasted_iota(jnp.int32, (BAND, 2 * BAND), 1)
    dist = BAND + qi - ki
    valid = (dist >= 0) & (dist <= BAND)
    if first:
        valid = valid & (ki >= BAND)
    return jnp.where(valid, -slope * (dist * dilation).astype(F32), MASKED)


def _dense_bias(slope, first):
    qi = lax.broadcasted_iota(jnp.int32, (BAND, 2 * BAND), 0)
    ki = lax.broadcasted_iota(jnp.int32, (BAND, 2 * BAND), 1)
    pq, jq = qi // 32, qi % 32
    half, pk, jk = ki // BAND, (ki % BAND) // 32, ki % 32
    dist = 4 * (jq - jk + 32 * (1 - half)) + pq - pk
    valid = (dist >= 0) & (dist <= BAND)
    if first:
        valid = valid & (half == 1)
    return jnp.where(valid, -slope * dist.astype(F32), MASKED)


def _wide_bias(slope):
    qi = lax.broadcasted_iota(jnp.int32, (2 * BAND, 2 * BAND), 0)
    ki = lax.broadcasted_iota(jnp.int32, (2 * BAND, 2 * BAND), 1)
    dist = qi - ki
    valid = (dist >= 0) & (dist <= BAND)
    return jnp.where(valid, -slope * (dist * 16).astype(F32), MASKED)


def _softmax_block(q, k, v, bias):
    s = lax.dot_general(q, k, (((1,), (1,)), ((), ())), preferred_element_type=F32) + bias
    m = jnp.max(s, axis=-1, keepdims=True)
    p = jnp.exp2(s - m).astype(BF16)
    v_ext = jnp.concatenate([v, jnp.ones_like(v)], axis=1)
    r = jnp.dot(p, v_ext, preferred_element_type=F32)
    hd = ATTN_HEAD_DIM
    return r[:, :hd], r[:, hd:], jnp.broadcast_to(m, (q.shape[0], hd))


def _attn_kernel(q_ref, k_ref, v_ref, o_ref,
                 x4_ref, x16_ref, num_ref, den_ref, mx_ref, bias_ref, bias16_ref, onat_ref):
    seq = q_ref.shape[0]
    n_u4, n_u16 = seq // 4, seq // 16
    slope = jnp.exp2(jnp.full((1, 1), -8.0 / ATTN_HEADS, F32) * (pl.program_id(1) + 1)) * LOG2E
    bias_ref[0, 0] = _dense_bias(slope, False)
    bias_ref[0, 1] = _dense_bias(slope, True)
    bias_ref[1, 0] = _band_bias(slope, 4, False)
    bias_ref[1, 1] = _band_bias(slope, 4, True)
    bias16_ref[...] = _wide_bias(slope)

    quarter = IN_TM // 4
    sixteenth = IN_TM // 16
    for ti, src in enumerate((q_ref, k_ref, v_ref)):
        def regroup(tile, carry, ti=ti, src=src):
            base = pl.multiple_of(tile * IN_TM, IN_TM)
            for p4 in range(4):
                for piece in range(quarter // BAND):
                    rows = src[pl.ds(base + p4 * quarter + piece * BAND, BAND), :]
                    dst = pl.ds(pl.multiple_of(tile * quarter + piece * BAND, BAND), BAND)
                    x4_ref[ti, p4, dst, :] = rows.astype(BF16)
            for p16 in range(16):
                rows = src[pl.ds(base + (p16 % 4) * quarter + p16 // 4, sixteenth, stride=4), :]
                x16_ref[ti, p16, pl.ds(pl.multiple_of(tile * sixteenth, sixteenth), sixteenth), :] = rows.astype(BF16)
            return carry

        lax.fori_loop(0, seq // IN_TM, regroup, 0)

    def store(pattern, p, rows, result):
        num, den, mx = result
        num_ref[pattern, p, rows, :] = num
        den_ref[pattern, p, rows, :] = den
        mx_ref[pattern, p, rows, :] = mx

    def dense_block(n):
        cur, prev = slice(n * 32, (n + 1) * 32), slice(max(n - 1, 0) * 32, max(n - 1, 0) * 32 + 32)

        def rows(ti, sl):
            return [x4_ref[ti, p, sl, :] for p in range(4)]

        q = jnp.concatenate(rows(0, cur), axis=0)
        k = jnp.concatenate(rows(1, prev) + rows(1, cur), axis=0)
        v = jnp.concatenate(rows(2, prev) + rows(2, cur), axis=0)
        num, den, mx = _softmax_block(q, k, v, bias_ref[0, int(n == 0)])
        for p in range(4):
            part = slice(p * 32, (p + 1) * 32)
            store(0, p, cur, (num[part], den[part], mx[part]))

    def band_block(p, n):
        cur = slice(n * BAND, (n + 1) * BAND)
        prev = slice(max(n - 1, 0) * BAND, max(n - 1, 0) * BAND + BAND)
        q = x4_ref[0, p, cur, :]
        k = jnp.concatenate([x4_ref[1, p, prev, :], x4_ref[1, p, cur, :]], axis=0)
        v = jnp.concatenate([x4_ref[2, p, prev, :], x4_ref[2, p, cur, :]], axis=0)
        store(1, p, cur, _softmax_block(q, k, v, bias_ref[1, int(n == 0)]))

    wide_rows = n_u16 // ATTN_PARTS

    def wide_block(p16, part):
        rows = slice(part * wide_rows, (part + 1) * wide_rows)
        result = _softmax_block(x16_ref[0, p16, rows, :], x16_ref[1, p16], x16_ref[2, p16], bias16_ref[rows, :])
        store(2, p16 % 4, pl.ds(p16 // 4 + 4 * wide_rows * part, wide_rows, stride=4), result)

    def merge_block(p, n):
        rows = slice(n * BAND, (n + 1) * BAND)
        m0, m1, m2 = mx_ref[0, p, rows, :], mx_ref[1, p, rows, :], mx_ref[2, p, rows, :]
        m_all = jnp.maximum(jnp.maximum(m0, m1), m2)
        w0, w1, w2 = jnp.exp2(m0 - m_all), jnp.exp2(m1 - m_all), jnp.exp2(m2 - m_all)
        num = w0 * num_ref[0, p, rows, :] + w1 * num_ref[1, p, rows, :] + w2 * num_ref[2, p, rows, :]
        den = w0 * den_ref[0, p, rows, :] + w1 * den_ref[1, p, rows, :] + w2 * den_ref[2, p, rows, :]
        onat_ref[pl.ds(p + 4 * BAND * n, BAND, stride=4), :] = num / den

    dense_per_part = n_u4 // 32 // ATTN_PARTS
    band_per_part = n_u4 // BAND // ATTN_PARTS
    for part in range(ATTN_PARTS):
        for n in range(part * dense_per_part, (part + 1) * dense_per_part):
            dense_block(n)
        for n in range(part * band_per_part, (part + 1) * band_per_part):
            for p in range(4):
                band_block(p, n)
                if part > 0:
                    merge_block(p, n - band_per_part)
        for p16 in range(16):
            wide_block(p16, part)
    for n in range((ATTN_PARTS - 1) * band_per_part, ATTN_PARTS * band_per_part):
        for p in range(4):
            merge_block(p, n)
    o_ref[...] = onat_ref[...].astype(o_ref.dtype)


def _attention(qkv, batch, seq):
    assert seq // 16 == 2 * BAND
    hd = ATTN_HEAD_DIM
    qkv4 = qkv.reshape(3 * ATTN_HEADS, batch, seq, hd)

    def spec(which):
        return pl.BlockSpec((None, None, seq, hd), lambda b, h: (which * ATTN_HEADS + h, b, 0, 0))

    out = pl.pallas_call(
        _attn_kernel,
        grid=(batch, ATTN_HEADS),
        in_specs=[spec(0), spec(1), spec(2)],
        out_specs=pl.BlockSpec((None, None, seq, hd), lambda b, h: (h, b, 0, 0)),
        out_shape=jax.ShapeDtypeStruct((ATTN_HEADS, batch, seq, hd), BF16),
        scratch_shapes=[
            pltpu.VMEM((3, 4, seq // 4, hd), BF16),
            pltpu.VMEM((3, 16, seq // 16, hd), BF16),
            pltpu.VMEM((3, 4, seq // 4, hd), F32),
            pltpu.VMEM((3, 4, seq // 4, hd), F32),
            pltpu.VMEM((3, 4, seq // 4, hd), F32),
            pltpu.VMEM((2, 2, BAND, 2 * BAND), F32),
            pltpu.VMEM((2 * BAND, 2 * BAND), F32),
            pltpu.VMEM((seq, hd), F32),
        ],
        compiler_params=_cparams(("parallel", "parallel")),
        name="dilated_attention",
    )(qkv4, qkv4, qkv4)
    return out.reshape(ATTN_HEADS, batch * seq, hd)


MLSTM_TS = 512
ONES_LANES = 128


def _mlstm_kernel(bias_ref, q_ref, k_ref, v_ref, gt_ref, out_ref, c_ref, m_ref, kt_ref,
                  lhs_ref, kw_ref, floor_ref):
    s = pl.program_id(1)
    dk, dv = MLSTM_QK_DIM, MLSTM_V_DIM
    L = CHUNK

    @pl.when(s == 0)
    def _():
        c_ref[...] = jnp.zeros_like(c_ref)
        m_ref[...] = jnp.zeros_like(m_ref)

    ti = lax.broadcasted_iota(jnp.int32, (CHUNK, CHUNK), 0)
    si = lax.broadcasted_iota(jnp.int32, (CHUNK, CHUNK), 1)
    causal = ti >= si
    csum = jnp.where(ti <= si, 1.0, 0.0).astype(F32)

    lane = lax.broadcasted_iota(jnp.int32, (MLSTM_HEADS, L), 1)

    for h in range(MLSTM_HEADS):
        for c in range(MLSTM_TS // L):
            k_hc = k_ref[h // 2, c * L:(c + 1) * L, (h % 2) * dk:(h % 2 + 1) * dk]
            kt_ref[h, c] = k_hc.astype(F32).T.astype(BF16)

    n_chunks = MLSTM_TS // L
    heads = range(MLSTM_HEADS)

    def q_of(h, c):
        return q_ref[h // 2, c * L:(c + 1) * L, (h % 2) * dk:(h % 2 + 1) * dk]

    def v_ext_of(h, c):
        return jnp.concatenate([v_ref[h, c * L:(c + 1) * L, :], jnp.ones((L, ONES_LANES), BF16)], axis=1)

    w_olds = []
    for c in range(n_chunks):
        gates = gt_ref[c] + bias_ref[...]
        ig_all = gates[:MLSTM_HEADS]
        lf_all = _log_sigmoid(gates[MLSTM_HEADS:])
        b_all = jnp.dot(lf_all, csum, preferred_element_type=F32,
                        precision=lax.Precision.HIGHEST)
        c_all = ig_all - b_all
        cm_all = c_all
        shift = 1
        while shift < L:
            cm_all = jnp.maximum(cm_all, jnp.where(lane >= shift, pltpu.roll(cm_all, shift, axis=1), MASKED))
            shift *= 2
        m_prev = m_ref[...]
        b_last = b_all[:, L - 1:L]
        top = jnp.maximum(m_prev, cm_all)
        m_new = b_last + top[:, L - 1:L]
        w_old = jnp.exp(b_last + m_prev - m_new)
        w_key = jnp.exp(b_last + c_all - m_new)
        cols = jnp.concatenate([
            -top,
            jnp.exp(m_prev - top),
            jnp.exp(-(b_all + top)),
        ], axis=0).T
        m_ref[...] = m_new
        w_olds.append(w_old)

        def column(group, h):
            j = group * MLSTM_HEADS + h
            return jnp.broadcast_to(cols[:, j:j + 1], (L, L))

        scores = [jnp.dot(q_of(h, c), kt_ref[h, c], preferred_element_type=F32) for h in heads]
        for h in heads:
            w_intra = jnp.exp(jnp.where(causal, column(0, h) + c_all[h:h + 1], MASKED))
            lhs_ref[c, h] = jnp.concatenate([(scores[h] * w_intra).astype(BF16),
                                             (q_of(h, c).astype(F32) * column(1, h)).astype(BF16)], axis=1)
            kw_ref[c, h] = (kt_ref[h, c].astype(F32) * w_key[h:h + 1]).astype(BF16)
            floor_ref[c, h] = column(2, h)

    for c in range(n_chunks):
        incs = [jnp.dot(kw_ref[c, h], v_ext_of(h, c), preferred_element_type=F32) for h in heads]
        nds = [jnp.dot(lhs_ref[c, h], jnp.concatenate([v_ext_of(h, c), c_ref[h].astype(BF16)], axis=0),
                       preferred_element_type=F32) for h in heads]
        for h in heads:
            nd = nds[h]
            inv = 1.0 / jnp.maximum(jnp.abs(nd[:, dv:]), floor_ref[c, h])
            for half in range(dv // L):
                lanes = slice(half * L, (half + 1) * L)
                out_ref[h, c * L:(c + 1) * L, lanes] = (nd[:, lanes] * inv).astype(out_ref.dtype)
            c_ref[h] = w_olds[c][h:h + 1] * c_ref[h] + incs[h]


def _mlstm(slabs, gates_t, b_if, batch, seq):
    n_slabs = slabs.shape[0]
    s4d = slabs.reshape(n_slabs, batch, seq, SLAB)
    half = MLSTM_HEADS // 2
    tiles = seq // MLSTM_TS
    tile_chunks = MLSTM_TS // CHUNK

    def tile_spec(first_slab, n):
        return pl.BlockSpec((n, None, MLSTM_TS, SLAB), lambda b, s: (first_slab // n, b, s, 0))

    out = pl.pallas_call(
        _mlstm_kernel,
        grid=(batch, tiles),
        in_specs=[
            pl.BlockSpec((2 * MLSTM_HEADS, 1), lambda b, s: (0, 0)),
            tile_spec(S_MQK * GROUP_SLABS, half), tile_spec(S_MQK * GROUP_SLABS + half, half),
            tile_spec(S_MV * GROUP_SLABS, MLSTM_HEADS),
            pl.BlockSpec((tile_chunks, 2 * MLSTM_HEADS, CHUNK), lambda b, s: (b * tiles + s, 0, 0)),
        ],
        out_specs=pl.BlockSpec((MLSTM_HEADS, None, MLSTM_TS, SLAB), lambda b, s: (0, b, s, 0)),
        out_shape=jax.ShapeDtypeStruct((MLSTM_HEADS, batch, seq, SLAB), BF16),
        scratch_shapes=[
            pltpu.VMEM((MLSTM_HEADS, MLSTM_QK_DIM, MLSTM_V_DIM + ONES_LANES), F32),
            pltpu.VMEM((MLSTM_HEADS, 1), F32),
            pltpu.VMEM((MLSTM_HEADS, tile_chunks, MLSTM_QK_DIM, CHUNK), BF16),
            pltpu.VMEM((tile_chunks, MLSTM_HEADS, CHUNK, CHUNK + MLSTM_QK_DIM), BF16),
            pltpu.VMEM((tile_chunks, MLSTM_HEADS, MLSTM_QK_DIM, CHUNK), BF16),
            pltpu.VMEM((tile_chunks, MLSTM_HEADS, CHUNK, CHUNK), F32),
        ],
        compiler_params=_cparams(("parallel", "arbitrary")),
        name="mlstm",
    )(b_if.reshape(-1, 1), s4d, s4d, s4d, gates_t)
    return out.reshape(MLSTM_HEADS, batch * seq, SLAB)


MERGE_TM = 512
OUT_TM = 512


def _weight_spec():
    return pl.BlockSpec((D_MODEL, D_MODEL), lambda i: (0, 0), pipeline_mode=pl.Buffered(1))


def _branch_merge_kernel(attn_ref, az_ref, cell_ref, mo_ref, mz_ref, ga_ref, gm_ref,
                         wa_ref, wm_ref, ng_ref, merged_ref):
    n = GROUP_SLABS
    gated = []
    for c in range(n):
        attn = jnp.concatenate([attn_ref[2 * c].astype(F32), attn_ref[2 * c + 1].astype(F32)], axis=1)
        gated.append(attn * _silu(az_ref[c].astype(F32)))
    gated = jnp.concatenate(gated, axis=1).astype(BF16)
    y_a, mem = [], []
    for h in range(MLSTM_HEADS):
        cols = slice(h * SLAB, (h + 1) * SLAB)
        y_a.append(jnp.dot(gated, wa_ref[:, cols], preferred_element_type=F32))
        cell = _sigmoid(mo_ref[h].astype(F32)) * cell_ref[h].astype(F32)
        cell = cell * lax.rsqrt(jnp.mean(cell * cell, axis=-1, keepdims=True) + NORM_EPS)
        cell = cell * ng_ref[:, cols]
        mem.append(cell * _silu(mz_ref[h].astype(F32)))
    mem = jnp.concatenate(mem, axis=1).astype(BF16)
    for c in range(n):
        cols = slice(c * SLAB, (c + 1) * SLAB)
        y_m = jnp.dot(mem, wm_ref[:, cols], preferred_element_type=F32)
        merged = _sigmoid(ga_ref[c].astype(F32)) * y_a[c] + _sigmoid(gm_ref[c].astype(F32)) * y_m
        merged_ref[:, cols] = merged.astype(merged_ref.dtype)


def _branch_merge(attn, slabs, cell, w_a, w_m, norm_g):
    t = attn.shape[1]
    assert MLSTM_V_DIM == SLAB and MLSTM_HEADS == GROUP_SLABS

    def slab_spec(group):
        return pl.BlockSpec((GROUP_SLABS, MERGE_TM, SLAB), lambda i: (group, i, 0))

    return pl.pallas_call(
        _branch_merge_kernel,
        grid=(t // MERGE_TM,),
        in_specs=[
            pl.BlockSpec((ATTN_HEADS, MERGE_TM, ATTN_HEAD_DIM), lambda i: (0, i, 0)),
            slab_spec(S_AZ), slab_spec(0), slab_spec(S_MO), slab_spec(S_MZ),
            slab_spec(S_GA), slab_spec(S_GM),
            _weight_spec(), _weight_spec(),
            pl.BlockSpec((1, D_MODEL), lambda i: (0, 0)),
        ],
        out_specs=pl.BlockSpec((MERGE_TM, D_MODEL), lambda i: (i, 0)),
        out_shape=jax.ShapeDtypeStruct((t, D_MODEL), BF16),
        compiler_params=_cparams(("parallel",)),
        name="branch_merge",
    )(attn, slabs, cell, slabs, slabs, slabs, slabs, w_a, w_m, norm_g)


def _out_proj_kernel(merged_ref, x_ref, wo_ref, fg_ref, out_ref):
    y = x_ref[...] + jnp.dot(merged_ref[...], wo_ref[...], preferred_element_type=F32)
    y = y * lax.rsqrt(jnp.mean(y * y, axis=-1, keepdims=True) + NORM_EPS)
    out_ref[...] = y * fg_ref[...]


def _out_proj(merged, x2, w_o, final_g):
    t = x2.shape[0]
    return pl.pallas_call(
        _out_proj_kernel,
        grid=(t // OUT_TM,),
        in_specs=[
            pl.BlockSpec((OUT_TM, D_MODEL), lambda i: (i, 0)),
            pl.BlockSpec((OUT_TM, D_MODEL), lambda i: (i, 0)),
            _weight_spec(),
            pl.BlockSpec((1, D_MODEL), lambda i: (0, 0)),
        ],
        out_specs=pl.BlockSpec((OUT_TM, D_MODEL), lambda i: (i, 0)),
        out_shape=jax.ShapeDtypeStruct((t, D_MODEL), F32),
        compiler_params=_cparams(("parallel",)),
        name="out_proj",
    )(merged, x2, w_o, final_g)


def kernel(x, norm_g, w_in, b_if, conv_w, conv_b, mlstm_norm_g, w_attn_branch, w_mlstm_branch,
           w_out, final_norm_g):
    batch, seq, d = x.shape
    assert d == D_MODEL and seq % (16 * BAND) == 0 and (batch * seq) % IN_TM == 0
    t = batch * seq
    x2 = x.reshape(t, d)

    w_big, w_gate = _weight_prep(w_in.T)
    assert w_big.shape[0] == QKV_COLS + N_SLAB_GROUPS * D_MODEL

    qkv, slabs, gates_t = _in_proj(x2, norm_g.reshape(1, d), w_big, w_gate, conv_w,
                                   conv_b.reshape(1, -1), seq)
    attn = _attention(qkv, batch, seq)
    cell = _mlstm(slabs, gates_t, b_if, batch, seq)
    merged = _branch_merge(attn, slabs, cell, w_attn_branch.astype(BF16), w_mlstm_branch.astype(BF16),
                           mlstm_norm_g.reshape(1, d))
    out = _out_proj(merged, x2, w_out.astype(BF16), final_norm_g.reshape(1, d))
    return out.reshape(batch, seq, d)
```

```python
import functools

import jax
import jax.numpy as jnp
from jax import lax
from jax.experimental import pallas as pl
from jax.experimental.pallas import tpu as pltpu

F32 = jnp.float32
BF16 = jnp.bfloat16

D_MODEL = 2048
ATTN_HEADS = 16
ATTN_HEAD_DIM = 128
MLSTM_HEADS = 8
MLSTM_QK_DIM = 128
MLSTM_V_DIM = 256
CONV_WIDTH = 4
NORM_EPS = 1e-6
BAND = 128
ATTN_PARTS = 2
SLAB = 256
GROUP_SLABS = D_MODEL // SLAB
QKV_COLS = 3 * D_MODEL
S_AZ, S_MQK, S_MV, S_MO, S_MZ, S_GA, S_GM = range(7)
N_SLAB_GROUPS = 7
GATE_LANES = 128
MASKED = -1e30
CHUNK = 128
VMEM_LIMIT = 56 * 1024 * 1024


def _cparams(sem):
    return pltpu.CompilerParams(dimension_semantics=sem, vmem_limit_bytes=VMEM_LIMIT)


LOG2E = 1.4426950408889634
NEG_LOG2E = -LOG2E


def _sigmoid(x):
    return 1.0 / (1.0 + jnp.exp2(x * NEG_LOG2E))


def _silu(x):
    half = 0.5 * x
    return half + half * jnp.tanh(half)


def _log_sigmoid(x):
    return jnp.minimum(x, 0.0) - jnp.log(1.0 + jnp.exp(-jnp.abs(x)))


PREP_TN = 1024
GATE_COLS = 2 * MLSTM_HEADS
GATE_COL0 = 4 * D_MODEL + 2 * MLSTM_HEADS * MLSTM_QK_DIM + MLSTM_HEADS * MLSTM_V_DIM
GATE_STEP = GATE_COL0 // PREP_TN
LANES = 128


def _weight_prep_kernel(a_ref, b_ref, w_ref, wg_ref):
    n = pl.program_id(0)

    @pl.when(n < GATE_STEP)
    def _():
        w_ref[...] = a_ref[...].astype(BF16)

    @pl.when(n >= GATE_STEP)
    def _():
        w_ref[...] = jnp.concatenate([a_ref[GATE_COLS:, :], b_ref[...]], axis=0).astype(BF16)

    @pl.when(n == GATE_STEP)
    def _():
        lane = lax.broadcasted_iota(jnp.int32, (D_MODEL, LANES), 1)
        wg_ref[...] = jnp.where(lane < GATE_COLS, a_ref[:LANES, :].T, 0.0).astype(BF16)


def _weight_prep(w_t):
    in_width, d = w_t.shape
    n_rows = in_width - GATE_COLS
    assert d == D_MODEL and GATE_COL0 % PREP_TN == 0 and n_rows % PREP_TN == 0 and GATE_LANES == LANES
    return pl.pallas_call(
        _weight_prep_kernel,
        grid=(n_rows // PREP_TN,),
        in_specs=[
            pl.BlockSpec((PREP_TN, D_MODEL), lambda n: (n, 0)),
            pl.BlockSpec((GATE_COLS, D_MODEL), lambda n: ((n + 1) * (PREP_TN // GATE_COLS), 0)),
        ],
        out_specs=[
            pl.BlockSpec((PREP_TN, D_MODEL), lambda n: (n, 0)),
            pl.BlockSpec((D_MODEL, LANES), lambda n: (0, 0)),
        ],
        out_shape=[
            jax.ShapeDtypeStruct((n_rows, D_MODEL), BF16),
            jax.ShapeDtypeStruct((D_MODEL, LANES), BF16),
        ],
        compiler_params=_cparams(("arbitrary",)),
        name="weight_prep",
    )(w_t, w_t)


IN_TM = 1024
IN_TN = 1024
QKV_STEPS = QKV_COLS // IN_TN


CONV_STEP0 = QKV_STEPS + S_MQK * (D_MODEL // IN_TN)
CONV_STEPS = D_MODEL // IN_TN
CONV_HALO = 8


def _in_proj_kernel(x_ref, g_ref, w_ref, wg_ref, cw_ref, cb_ref, qkv_ref, p_ref, gt_ref,
                    hn_ref, halo_ref, acc_ref, stage_ref, *, tiles_per_seq):
    i = pl.program_id(0)
    n = pl.program_id(1)

    @pl.when((i == 0) & (n == 0))
    def _():
        halo_ref[...] = jnp.zeros_like(halo_ref)

    @pl.when(n == 0)
    def _():
        x = x_ref[...]
        ms = jnp.mean(x * x, axis=-1, keepdims=True)
        hn = (x * lax.rsqrt(ms + NORM_EPS) * g_ref[...]).astype(BF16)
        hn_ref[...] = hn
        gates = jnp.dot(hn, wg_ref[...], preferred_element_type=F32)
        gates_t = gates.T
        for c in range(IN_TM // CHUNK):
            gt_ref[c] = gates_t[:2 * MLSTM_HEADS, c * CHUNK:(c + 1) * CHUNK]

    def slab_acc(j):
        return lax.dot_general(hn_ref[...], w_ref[j * SLAB:(j + 1) * SLAB, :], (((1,), (1,)), ((), ())),
                               preferred_element_type=F32)

    @pl.when(n < QKV_STEPS)
    def _():
        scale = jnp.where(n < D_MODEL // IN_TN, ATTN_HEAD_DIM ** -0.5 * LOG2E, 1.0).astype(F32)
        quarter = IN_TM // 4
        for j in range(IN_TN // SLAB):
            acc = slab_acc(j) * scale
            for hh in range(SLAB // ATTN_HEAD_DIM):
                slot = (2 * j + hh) % 2
                stage_ref[slot] = acc[:, hh * ATTN_HEAD_DIM:(hh + 1) * ATTN_HEAD_DIM]
                for p in range(4):
                    qkv_ref[2 * j + hh, p * quarter:(p + 1) * quarter, :] = (
                        stage_ref[slot, pl.ds(p, quarter, stride=4), :])

    is_conv = (n >= CONV_STEP0) & (n < CONV_STEP0 + CONV_STEPS)

    @pl.when((n >= QKV_STEPS) & jnp.logical_not(is_conv))
    def _():
        for j in range(IN_TN // SLAB):
            p_ref[j] = slab_acc(j).astype(BF16)

    @pl.when(is_conv)
    def _():
        step = n - CONV_STEP0
        seq_start = (i % tiles_per_seq) == 0
        post = jnp.where(step == 1, MLSTM_QK_DIM ** -0.5, 1.0).astype(F32)

        def conv_slab(j):
            cols = slice(j * SLAB, (j + 1) * SLAB)
            slot = (n + j) % 2
            acc_ref[slot, 0:CONV_HALO, :] = jnp.where(seq_start, 0.0, halo_ref[step, :, cols])
            halo_ref[step, :, cols] = acc_ref[slot, IN_TM:IN_TM + CONV_HALO, :]
            y = cb_ref[:, cols]
            for back in range(CONV_WIDTH):
                tap = acc_ref[slot, CONV_HALO - back:CONV_HALO - back + IN_TM, :]
                y = y + cw_ref[CONV_WIDTH - 1 - back:CONV_WIDTH - back, cols] * tap
            p_ref[j] = (_silu(y) * post).astype(BF16)

        n_slabs = IN_TN // SLAB
        for j in range(n_slabs):
            acc_ref[(n + j) % 2, CONV_HALO:, :] = slab_acc(j)
            if j > 0:
                conv_slab(j - 1)
        conv_slab(n_slabs - 1)


def _in_proj(x2, norm_g, w_big, w_gate, conv_w, conv_b, seq):
    t = x2.shape[0]
    n_cols = w_big.shape[0]
    grid = (t // IN_TM, n_cols // IN_TN)
    assert seq % IN_TM == 0 and CONV_STEPS * IN_TN == conv_w.shape[1]

    def conv_step(i, n):
        return (0, jnp.clip(n - CONV_STEP0, 0, CONV_STEPS - 1))

    return pl.pallas_call(
        functools.partial(_in_proj_kernel, tiles_per_seq=seq // IN_TM),
        grid=grid,
        in_specs=[
            pl.BlockSpec((IN_TM, D_MODEL), lambda i, n: (i, 0)),
            pl.BlockSpec((1, D_MODEL), lambda i, n: (0, 0)),
            pl.BlockSpec((IN_TN, D_MODEL), lambda i, n: (n, 0)),
            pl.BlockSpec((D_MODEL, GATE_LANES), lambda i, n: (0, 0)),
            pl.BlockSpec((CONV_WIDTH, IN_TN), conv_step),
            pl.BlockSpec((1, IN_TN), conv_step),
        ],
        out_specs=[
            pl.BlockSpec((IN_TN // ATTN_HEAD_DIM, IN_TM, ATTN_HEAD_DIM),
                         lambda i, n: (jnp.minimum(n, QKV_STEPS - 1), i, 0)),
            pl.BlockSpec((IN_TN // SLAB, IN_TM, SLAB),
                         lambda i, n: (jnp.maximum(n - QKV_STEPS, 0), i, 0)),
            pl.BlockSpec((IN_TM // CHUNK, 2 * MLSTM_HEADS, CHUNK), lambda i, n: (i, 0, 0)),
        ],
        out_shape=[
            jax.ShapeDtypeStruct((QKV_COLS // ATTN_HEAD_DIM, t, ATTN_HEAD_DIM), F32),
            jax.ShapeDtypeStruct(((n_cols - QKV_COLS) // SLAB, t, SLAB), BF16),
            jax.ShapeDtypeStruct((t // CHUNK, 2 * MLSTM_HEADS, CHUNK), F32),
        ],
        scratch_shapes=[pltpu.VMEM((IN_TM, D_MODEL), BF16),
                        pltpu.VMEM((CONV_STEPS, CONV_HALO, IN_TN), F32),
                        pltpu.VMEM((2, CONV_HALO + IN_TM, SLAB), F32),
                        pltpu.VMEM((2, IN_TM, ATTN_HEAD_DIM), F32)],
        compiler_params=_cparams(("arbitrary", "arbitrary")),
        name="in_proj",
    )(x2, norm_g, w_big, w_gate, conv_w, conv_b)


def _band_bias(slope, dilation, first):
    qi = lax.broadcasted_iota(jnp.int32, (BAND, 2 * BAND), 0)
    ki = lax.broadcasted_iota(jnp.int32, (BAND, 2 * BAND), 1)
    dist = BAND + qi - ki
    valid = (dist >= 0) & (dist <= BAND)
    if first:
        valid = valid & (ki >= BAND)
    return jnp.where(valid, -slope * (dist * dilation).astype(F32), MASKED)


def _dense_bias(slope, first):
    qi = lax.broadcasted_iota(jnp.int32, (BAND, 2 * BAND), 0)
    ki = lax.broadcasted_iota(jnp.int32, (BAND, 2 * BAND), 1)
    pq, jq = qi // 32, qi % 32
    half, pk, jk = ki // BAND, (ki % BAND) // 32, ki % 32
    dist = 4 * (jq - jk + 32 * (1 - half)) + pq - pk
    valid = (dist >= 0) & (dist <= BAND)
    if first:
        valid = valid & (half == 1)
    return jnp.where(valid, -slope * dist.astype(F32), MASKED)


def _wide_bias(slope):
    qi = lax.broadcasted_iota(jnp.int32, (2 * BAND, 2 * BAND), 0)
    ki = lax.broadcasted_iota(jnp.int32, (2 * BAND, 2 * BAND), 1)
    dist = qi - ki
    valid = (dist >= 0) & (dist <= BAND)
    return jnp.where(valid, -slope * (dist * 16).astype(F32), MASKED)


def _softmax_block(q, k, v, bias):
    s = lax.dot_general(q, k, (((1,), (1,)), ((), ())), preferred_element_type=F32) + bias
    m = jnp.max(s, axis=-1, keepdims=True)
    p = jnp.exp2(s - m).astype(BF16)
    v_ext = jnp.concatenate([v, jnp.ones_like(v)], axis=1)
    r = jnp.dot(p, v_ext, preferred_element_type=F32)
    hd = ATTN_HEAD_DIM
    return r[:, :hd], r[:, hd:], jnp.broadcast_to(m, (q.shape[0], hd))


def _attn_kernel(q_ref, k_ref, v_ref, o_ref,
                 x4_ref, x16_ref, num_ref, den_ref, mx_ref, bias_ref, bias16_ref, unit_ref, unit16_ref,
                 onat_ref):
    seq = q_ref.shape[0]
    n_u4, n_u16 = seq // 4, seq // 16
    slope = jnp.exp2(jnp.full((1, 1), -8.0 / ATTN_HEADS, F32) * (pl.program_id(1) + 1)) * LOG2E

    @pl.when((pl.program_id(0) == 0) & (pl.program_id(1) == 0))
    def _():
        one = jnp.ones((1, 1), F32)
        unit_ref[0, 0] = _dense_bias(one, False)
        unit_ref[0, 1] = _dense_bias(one, True)
        unit_ref[1, 0] = _band_bias(one, 4, False)
        unit_ref[1, 1] = _band_bias(one, 4, True)
        unit16_ref[...] = _wide_bias(one)

    bias_ref[...] = unit_ref[...] * slope
    bias16_ref[...] = unit16_ref[...] * slope

    quarter = IN_TM // 4
    sixteenth = IN_TM // 16
    for ti, src in enumerate((q_ref, k_ref, v_ref)):
        def regroup(tile, carry, ti=ti, src=src):
            base = pl.multiple_of(tile * IN_TM, IN_TM)
            for p4 in range(4):
                for piece in range(quarter // BAND):
                    rows = src[pl.ds(base + p4 * quarter + piece * BAND, BAND), :]
                    dst = pl.ds(pl.multiple_of(tile * quarter + piece * BAND, BAND), BAND)
                    x4_ref[ti, p4, dst, :] = rows.astype(BF16)
            for p16 in range(16):
                rows = src[pl.ds(base + (p16 % 4) * quarter + p16 // 4, sixteenth, stride=4), :]
                x16_ref[ti, p16, pl.ds(pl.multiple_of(tile * sixteenth, sixteenth), sixteenth), :] = rows.astype(BF16)
            return carry

        lax.fori_loop(0, seq // IN_TM, regroup, 0)

    def store(pattern, p, rows, result):
        num, den, mx = result
        num_ref[pattern, p, rows, :] = num
        den_ref[pattern, p, rows, :] = den
        mx_ref[pattern, p, rows, :] = mx

    def dense_block(n):
        cur, prev = slice(n * 32, (n + 1) * 32), slice(max(n - 1, 0) * 32, max(n - 1, 0) * 32 + 32)

        def rows(ti, sl):
            return [x4_ref[ti, p, sl, :] for p in range(4)]

        q = jnp.concatenate(rows(0, cur), axis=0)
        k = jnp.concatenate(rows(1, prev) + rows(1, cur), axis=0)
        v = jnp.concatenate(rows(2, prev) + rows(2, cur), axis=0)
        num, den, mx = _softmax_block(q, k, v, bias_ref[0, int(n == 0)])
        for p in range(4):
            part = slice(p * 32, (p + 1) * 32)
            store(0, p, cur, (num[part], den[part], mx[part]))

    def band_block(p, n):
        cur = slice(n * BAND, (n + 1) * BAND)
        prev = slice(max(n - 1, 0) * BAND, max(n - 1, 0) * BAND + BAND)
        q = x4_ref[0, p, cur, :]
        k = jnp.concatenate([x4_ref[1, p, prev, :], x4_ref[1, p, cur, :]], axis=0)
        v = jnp.concatenate([x4_ref[2, p, prev, :], x4_ref[2, p, cur, :]], axis=0)
        store(1, p, cur, _softmax_block(q, k, v, bias_ref[1, int(n == 0)]))

    wide_rows = n_u16 // ATTN_PARTS

    def wide_block(p16, part):
        rows = slice(part * wide_rows, (part + 1) * wide_rows)
        result = _softmax_block(x16_ref[0, p16, rows, :], x16_ref[1, p16], x16_ref[2, p16], bias16_ref[rows, :])
        store(2, p16 % 4, pl.ds(p16 // 4 + 4 * wide_rows * part, wide_rows, stride=4), result)

    def merge_block(p, n):
        rows = slice(n * BAND, (n + 1) * BAND)
        m0, m1, m2 = mx_ref[0, p, rows, :], mx_ref[1, p, rows, :], mx_ref[2, p, rows, :]
        m_all = jnp.maximum(jnp.maximum(m0, m1), m2)
        w0, w1, w2 = jnp.exp2(m0 - m_all), jnp.exp2(m1 - m_all), jnp.exp2(m2 - m_all)
        num = w0 * num_ref[0, p, rows, :] + w1 * num_ref[1, p, rows, :] + w2 * num_ref[2, p, rows, :]
        den = w0 * den_ref[0, p, rows, :] + w1 * den_ref[1, p, rows, :] + w2 * den_ref[2, p, rows, :]
        onat_ref[pl.ds(p + 4 * BAND * n, BAND, stride=4), :] = num / den

    dense_per_part = n_u4 // 32 // ATTN_PARTS
    band_per_part = n_u4 // BAND // ATTN_PARTS
    for part in range(ATTN_PARTS):
        for n in range(part * dense_per_part, (part + 1) * dense_per_part):
            dense_block(n)
        for n in range(part * band_per_part, (part + 1) * band_per_part):
            for p in range(4):
                band_block(p, n)
                if part > 0:
                    merge_block(p, n - band_per_part)
        for p16 in range(16):
            wide_block(p16, part)
    for n in range((ATTN_PARTS - 1) * band_per_part, ATTN_PARTS * band_per_part):
        for p in range(4):
            merge_block(p, n)
    o_ref[...] = onat_ref[...].astype(o_ref.dtype)


def _attention(qkv, batch, seq):
    assert seq // 16 == 2 * BAND
    hd = ATTN_HEAD_DIM
    qkv4 = qkv.reshape(3 * ATTN_HEADS, batch, seq, hd)

    def spec(which):
        return pl.BlockSpec((None, None, seq, hd), lambda b, h: (which * ATTN_HEADS + h, b, 0, 0))

    out = pl.pallas_call(
        _attn_kernel,
        grid=(batch, ATTN_HEADS),
        in_specs=[spec(0), spec(1), spec(2)],
        out_specs=pl.BlockSpec((None, None, seq, hd), lambda b, h: (h, b, 0, 0)),
        out_shape=jax.ShapeDtypeStruct((ATTN_HEADS, batch, seq, hd), BF16),
        scratch_shapes=[
            pltpu.VMEM((3, 4, seq // 4, hd), BF16),
            pltpu.VMEM((3, 16, seq // 16, hd), BF16),
            pltpu.VMEM((3, 4, seq // 4, hd), F32),
            pltpu.VMEM((3, 4, seq // 4, hd), F32),
            pltpu.VMEM((3, 4, seq // 4, hd), F32),
            pltpu.VMEM((2, 2, BAND, 2 * BAND), F32),
            pltpu.VMEM((2 * BAND, 2 * BAND), F32),
            pltpu.VMEM((2, 2, BAND, 2 * BAND), F32),
            pltpu.VMEM((2 * BAND, 2 * BAND), F32),
            pltpu.VMEM((seq, hd), F32),
        ],
        compiler_params=_cparams(("arbitrary", "arbitrary")),
        name="dilated_attention",
    )(qkv4, qkv4, qkv4)
    return out.reshape(ATTN_HEADS, batch * seq, hd)


MLSTM_TS = 512
ONES_LANES = 128


def _mlstm_kernel(bias_ref, q_ref, k_ref, v_ref, gt_ref, out_ref, c_ref, m_ref, kt_ref,
                  lhs_ref, kw_ref, floor_ref):
    s = pl.program_id(1)
    dk, dv = MLSTM_QK_DIM, MLSTM_V_DIM
    L = CHUNK

    @pl.when(s == 0)
    def _():
        c_ref[...] = jnp.zeros_like(c_ref)
        m_ref[...] = jnp.zeros_like(m_ref)

    ti = lax.broadcasted_iota(jnp.int32, (CHUNK, CHUNK), 0)
    si = lax.broadcasted_iota(jnp.int32, (CHUNK, CHUNK), 1)
    causal = ti >= si
    csum = jnp.where(ti <= si, 1.0, 0.0).astype(F32)

    lane = lax.broadcasted_iota(jnp.int32, (MLSTM_HEADS, L), 1)

    for h in range(MLSTM_HEADS):
        for c in range(MLSTM_TS // L):
            k_hc = k_ref[h // 2, c * L:(c + 1) * L, (h % 2) * dk:(h % 2 + 1) * dk]
            kt_ref[h, c] = k_hc.astype(F32).T.astype(BF16)

    n_chunks = MLSTM_TS // L
    heads = range(MLSTM_HEADS)

    def q_of(h, c):
        return q_ref[h // 2, c * L:(c + 1) * L, (h % 2) * dk:(h % 2 + 1) * dk]

    def v_ext_of(h, c):
        return jnp.concatenate([v_ref[h, c * L:(c + 1) * L, :], jnp.ones((L, ONES_LANES), BF16)], axis=1)

    w_olds = []
    for c in range(n_chunks):
        gates = gt_ref[c] + bias_ref[...]
        ig_all = gates[:MLSTM_HEADS]
        lf_all = _log_sigmoid(gates[MLSTM_HEADS:])
        b_all = jnp.dot(lf_all, csum, preferred_element_type=F32,
                        precision=lax.Precision.HIGHEST)
        c_all = ig_all - b_all
        cm_all = c_all
        shift = 1
        while shift < L:
            cm_all = jnp.maximum(cm_all, jnp.where(lane >= shift, pltpu.roll(cm_all, shift, axis=1), MASKED))
            shift *= 2
        m_prev = m_ref[...]
        b_last = b_all[:, L - 1:L]
        top = jnp.maximum(m_prev, cm_all)
        m_new = b_last + top[:, L - 1:L]
        w_old = jnp.exp(b_last + m_prev - m_new)
        w_key = jnp.exp(b_last + c_all - m_new)
        cols = jnp.concatenate([
            -top,
            jnp.exp(m_prev - top),
            jnp.exp(-(b_all + top)),
        ], axis=0).T
        m_ref[...] = m_new
        w_olds.append(w_old)

        def column(group, h):
            j = group * MLSTM_HEADS + h
            return jnp.broadcast_to(cols[:, j:j + 1], (L, L))

        scores = [jnp.dot(q_of(h, c), kt_ref[h, c], preferred_element_type=F32) for h in heads]
        for h in heads:
            w_intra = jnp.exp(jnp.where(causal, column(0, h) + c_all[h:h + 1], MASKED))
            lhs_ref[c, h] = jnp.concatenate([(scores[h] * w_intra).astype(BF16),
                                             (q_of(h, c).astype(F32) * column(1, h)).astype(BF16)], axis=1)
            kw_ref[c, h] = (kt_ref[h, c].astype(F32) * w_key[h:h + 1]).astype(BF16)
            floor_ref[c, h] = column(2, h)

    for c in range(n_chunks):
        incs = [jnp.dot(kw_ref[c, h], v_ext_of(h, c), preferred_element_type=F32) for h in heads]
        nds = [jnp.dot(lhs_ref[c, h], jnp.concatenate([v_ext_of(h, c), c_ref[h].astype(BF16)], axis=0),
                       preferred_element_type=F32) for h in heads]
        for h in heads:
            nd = nds[h]
            inv = 1.0 / jnp.maximum(jnp.abs(nd[:, dv:]), floor_ref[c, h])
            for half in range(dv // L):
                lanes = slice(half * L, (half + 1) * L)
                out_ref[h, c * L:(c + 1) * L, lanes] = (nd[:, lanes] * inv).astype(out_ref.dtype)
            c_ref[h] = w_olds[c][h:h + 1] * c_ref[h] + incs[h]


def _mlstm(slabs, gates_t, b_if, batch, seq):
    n_slabs = slabs.shape[0]
    s4d = slabs.reshape(n_slabs, batch, seq, SLAB)
    half = MLSTM_HEADS // 2
    tiles = seq // MLSTM_TS
    tile_chunks = MLSTM_TS // CHUNK

    def tile_spec(first_slab, n):
        return pl.BlockSpec((n, None, MLSTM_TS, SLAB), lambda b, s: (first_slab // n, b, s, 0))

    out = pl.pallas_call(
        _mlstm_kernel,
        grid=(batch, tiles),
        in_specs=[
            pl.BlockSpec((2 * MLSTM_HEADS, 1), lambda b, s: (0, 0)),
            tile_spec(S_MQK * GROUP_SLABS, half), tile_spec(S_MQK * GROUP_SLABS + half, half),
            tile_spec(S_MV * GROUP_SLABS, MLSTM_HEADS),
            pl.BlockSpec((tile_chunks, 2 * MLSTM_HEADS, CHUNK), lambda b, s: (b * tiles + s, 0, 0)),
        ],
        out_specs=pl.BlockSpec((MLSTM_HEADS, None, MLSTM_TS, SLAB), lambda b, s: (0, b, s, 0)),
        out_shape=jax.ShapeDtypeStruct((MLSTM_HEADS, batch, seq, SLAB), BF16),
        scratch_shapes=[
            pltpu.VMEM((MLSTM_HEADS, MLSTM_QK_DIM, MLSTM_V_DIM + ONES_LANES), F32),
            pltpu.VMEM((MLSTM_HEADS, 1), F32),
            pltpu.VMEM((MLSTM_HEADS, tile_chunks, MLSTM_QK_DIM, CHUNK), BF16),
            pltpu.VMEM((tile_chunks, MLSTM_HEADS, CHUNK, CHUNK + MLSTM_QK_DIM), BF16),
            pltpu.VMEM((tile_chunks, MLSTM_HEADS, MLSTM_QK_DIM, CHUNK), BF16),
            pltpu.VMEM((tile_chunks, MLSTM_HEADS, CHUNK, CHUNK), F32),
        ],
        compiler_params=_cparams(("parallel", "arbitrary")),
        name="mlstm",
    )(b_if.reshape(-1, 1), s4d, s4d, s4d, gates_t)
    return out.reshape(MLSTM_HEADS, batch * seq, SLAB)


MERGE_TM = 512
OUT_TM = 512


def _weight_spec():
    return pl.BlockSpec((D_MODEL, D_MODEL), lambda i: (0, 0), pipeline_mode=pl.Buffered(1))


def _branch_merge_kernel(attn_ref, az_ref, cell_ref, mo_ref, mz_ref, ga_ref, gm_ref,
                         wa_ref, wm_ref, ng_ref, merged_ref):
    n = GROUP_SLABS
    gated = []
    for c in range(n):
        attn = jnp.concatenate([attn_ref[2 * c].astype(F32), attn_ref[2 * c + 1].astype(F32)], axis=1)
        gated.append(attn * _silu(az_ref[c].astype(F32)))
    gated = jnp.concatenate(gated, axis=1).astype(BF16)
    y_a, mem = [], []
    for h in range(MLSTM_HEADS):
        cols = slice(h * SLAB, (h + 1) * SLAB)
        y_a.append(jnp.dot(gated, wa_ref[:, cols], preferred_element_type=F32))
        cell = _sigmoid(mo_ref[h].astype(F32)) * cell_ref[h].astype(F32)
        cell = cell * lax.rsqrt(jnp.mean(cell * cell, axis=-1, keepdims=True) + NORM_EPS)
        cell = cell * ng_ref[:, cols]
        mem.append(cell * _silu(mz_ref[h].astype(F32)))
    mem = jnp.concatenate(mem, axis=1).astype(BF16)
    for c in range(n):
        cols = slice(c * SLAB, (c + 1) * SLAB)
        y_m = jnp.dot(mem, wm_ref[:, cols], preferred_element_type=F32)
        merged = _sigmoid(ga_ref[c].astype(F32)) * y_a[c] + _sigmoid(gm_ref[c].astype(F32)) * y_m
        merged_ref[:, cols] = merged.astype(merged_ref.dtype)


def _branch_merge(attn, slabs, cell, w_a, w_m, norm_g):
    t = attn.shape[1]
    assert MLSTM_V_DIM == SLAB and MLSTM_HEADS == GROUP_SLABS

    def slab_spec(group):
        return pl.BlockSpec((GROUP_SLABS, MERGE_TM, SLAB), lambda i: (group, i, 0))

    return pl.pallas_call(
        _branch_merge_kernel,
        grid=(t // MERGE_TM,),
        in_specs=[
            pl.BlockSpec((ATTN_HEADS, MERGE_TM, ATTN_HEAD_DIM), lambda i: (0, i, 0)),
            slab_spec(S_AZ), slab_spec(0), slab_spec(S_MO), slab_spec(S_MZ),
            slab_spec(S_GA), slab_spec(S_GM),
            _weight_spec(), _weight_spec(),
            pl.BlockSpec((1, D_MODEL), lambda i: (0, 0)),
        ],
        out_specs=pl.BlockSpec((MERGE_TM, D_MODEL), lambda i: (i, 0)),
        out_shape=jax.ShapeDtypeStruct((t, D_MODEL), BF16),
        compiler_params=_cparams(("parallel",)),
        name="branch_merge",
    )(attn, slabs, cell, slabs, slabs, slabs, slabs, w_a, w_m, norm_g)


def _out_proj_kernel(merged_ref, x_ref, wo_ref, fg_ref, out_ref):
    y = x_ref[...] + jnp.dot(merged_ref[...], wo_ref[...], preferred_element_type=F32)
    y = y * lax.rsqrt(jnp.mean(y * y, axis=-1, keepdims=True) + NORM_EPS)
    out_ref[...] = y * fg_ref[...]


def _out_proj(merged, x2, w_o, final_g):
    t = x2.shape[0]
    return pl.pallas_call(
        _out_proj_kernel,
        grid=(t // OUT_TM,),
        in_specs=[
            pl.BlockSpec((OUT_TM, D_MODEL), lambda i: (i, 0)),
            pl.BlockSpec((OUT_TM, D_MODEL), lambda i: (i, 0)),
            _weight_spec(),
            pl.BlockSpec((1, D_MODEL), lambda i: (0, 0)),
        ],
        out_specs=pl.BlockSpec((OUT_TM, D_MODEL), lambda i: (i, 0)),
        out_shape=jax.ShapeDtypeStruct((t, D_MODEL), F32),
        compiler_params=_cparams(("parallel",)),
        name="out_proj",
    )(merged, x2, w_o, final_g)


def kernel(x, norm_g, w_in, b_if, conv_w, conv_b, mlstm_norm_g, w_attn_branch, w_mlstm_branch,
           w_out, final_norm_g):
    batch, seq, d = x.shape
    assert d == D_MODEL and seq % (16 * BAND) == 0 and (batch * seq) % IN_TM == 0
    t = batch * seq
    x2 = x.reshape(t, d)

    w_big, w_gate = _weight_prep(w_in.T)
    assert w_big.shape[0] == QKV_COLS + N_SLAB_GROUPS * D_MODEL

    qkv, slabs, gates_t = _in_proj(x2, norm_g.reshape(1, d), w_big, w_gate, conv_w,
                                   conv_b.reshape(1, -1), seq)
    attn = _attention(qkv, batch, seq)
    cell = _mlstm(slabs, gates_t, b_if, batch, seq)
    merged = _branch_merge(attn, slabs, cell, w_attn_branch.astype(BF16), w_mlstm_branch.astype(BF16),
                           mlstm_norm_g.reshape(1, d))
    out = _out_proj(merged, x2, w_out.astype(BF16), final_norm_g.reshape(1, d))
    return out.reshape(batch, seq, d)
```

```python
import functools

import jax
import jax.numpy as jnp
from jax import lax
from jax.experimental import pallas as pl
from jax.experimental.pallas import tpu as pltpu

F32 = jnp.float32
BF16 = jnp.bfloat16

D_MODEL = 2048
ATTN_HEADS = 16
ATTN_HEAD_DIM = 128
MLSTM_HEADS = 8
MLSTM_QK_DIM = 128
MLSTM_V_DIM = 256
CONV_WIDTH = 4
NORM_EPS = 1e-6
BAND = 128
ATTN_PARTS = 2
SLAB = 256
GROUP_SLABS = D_MODEL // SLAB
QKV_COLS = 3 * D_MODEL
S_AZ, S_MQK, S_MV, S_MO, S_MZ, S_GA, S_GM = range(7)
N_SLAB_GROUPS = 7
GATE_LANES = 128
MASKED = -1e30
CHUNK = 128
VMEM_LIMIT = 56 * 1024 * 1024


def _cparams(sem):
    return pltpu.CompilerParams(dimension_semantics=sem, vmem_limit_bytes=VMEM_LIMIT)


LOG2E = 1.4426950408889634


def _sigmoid(x):
    return 0.5 + 0.5 * jnp.tanh(0.5 * x)


def _silu(x):
    half = 0.5 * x
    return half + half * jnp.tanh(half)


def _log_sigmoid(x):
    return jnp.minimum(x, 0.0) - jnp.log(1.0 + jnp.exp(-jnp.abs(x)))


PREP_TN = 1024
GATE_COLS = 2 * MLSTM_HEADS
GATE_COL0 = 4 * D_MODEL + 2 * MLSTM_HEADS * MLSTM_QK_DIM + MLSTM_HEADS * MLSTM_V_DIM
GATE_STEP = GATE_COL0 // PREP_TN
LANES = 128


def _weight_prep_kernel(a_ref, b_ref, w_ref, wg_ref):
    n = pl.program_id(0)

    @pl.when(n < GATE_STEP)
    def _():
        w_ref[...] = a_ref[...].astype(BF16)

    @pl.when(n >= GATE_STEP)
    def _():
        w_ref[...] = jnp.concatenate([a_ref[GATE_COLS:, :], b_ref[...]], axis=0).astype(BF16)

    @pl.when(n == GATE_STEP)
    def _():
        lane = lax.broadcasted_iota(jnp.int32, (D_MODEL, LANES), 1)
        wg_ref[...] = jnp.where(lane < GATE_COLS, a_ref[:LANES, :].T, 0.0).astype(BF16)


def _weight_prep(w_t):
    in_width, d = w_t.shape
    n_rows = in_width - GATE_COLS
    assert d == D_MODEL and GATE_COL0 % PREP_TN == 0 and n_rows % PREP_TN == 0 and GATE_LANES == LANES
    return pl.pallas_call(
        _weight_prep_kernel,
        grid=(n_rows // PREP_TN,),
        in_specs=[
            pl.BlockSpec((PREP_TN, D_MODEL), lambda n: (n, 0)),
            pl.BlockSpec((GATE_COLS, D_MODEL), lambda n: ((n + 1) * (PREP_TN // GATE_COLS), 0)),
        ],
        out_specs=[
            pl.BlockSpec((PREP_TN, D_MODEL), lambda n: (n, 0)),
            pl.BlockSpec((D_MODEL, LANES), lambda n: (0, 0)),
        ],
        out_shape=[
            jax.ShapeDtypeStruct((n_rows, D_MODEL), BF16),
            jax.ShapeDtypeStruct((D_MODEL, LANES), BF16),
        ],
        compiler_params=_cparams(("arbitrary",)),
        name="weight_prep",
    )(w_t, w_t)


IN_TM = 1024
IN_TN = 1024
QKV_STEPS = QKV_COLS // IN_TN


CONV_STEP0 = QKV_STEPS + S_MQK * (D_MODEL // IN_TN)
CONV_STEPS = D_MODEL // IN_TN
CONV_HALO = 8


def _in_proj_kernel(x_ref, g_ref, w_ref, wg_ref, cw_ref, cb_ref, qkv_ref, p_ref, gt_ref,
                    hn_ref, halo_ref, acc_ref, stage_ref, *, tiles_per_seq):
    i = pl.program_id(0)
    n = pl.program_id(1)

    @pl.when((i == 0) & (n == 0))
    def _():
        halo_ref[...] = jnp.zeros_like(halo_ref)

    @pl.when(n == 0)
    def _():
        x = x_ref[...]
        ms = jnp.mean(x * x, axis=-1, keepdims=True)
        hn = (x * lax.rsqrt(ms + NORM_EPS) * g_ref[...]).astype(BF16)
        hn_ref[...] = hn
        gates = jnp.dot(hn, wg_ref[...], preferred_element_type=F32)
        gates_t = gates.T
        for c in range(IN_TM // CHUNK):
            gt_ref[c] = gates_t[:2 * MLSTM_HEADS, c * CHUNK:(c + 1) * CHUNK]

    def slab_acc(j):
        return lax.dot_general(hn_ref[...], w_ref[j * SLAB:(j + 1) * SLAB, :], (((1,), (1,)), ((), ())),
                               preferred_element_type=F32)

    @pl.when(n < QKV_STEPS)
    def _():
        scale = jnp.where(n < D_MODEL // IN_TN, ATTN_HEAD_DIM ** -0.5 * LOG2E, 1.0).astype(F32)
        quarter = IN_TM // 4
        for j in range(IN_TN // SLAB):
            acc = slab_acc(j) * scale
            for hh in range(SLAB // ATTN_HEAD_DIM):
                slot = (2 * j + hh) % 2
                stage_ref[slot] = acc[:, hh * ATTN_HEAD_DIM:(hh + 1) * ATTN_HEAD_DIM]
                for p in range(4):
                    qkv_ref[2 * j + hh, p * quarter:(p + 1) * quarter, :] = (
                        stage_ref[slot, pl.ds(p, quarter, stride=4), :])

    is_conv = (n >= CONV_STEP0) & (n < CONV_STEP0 + CONV_STEPS)

    group = (n - QKV_STEPS) // (D_MODEL // IN_TN)
    is_silu = (n >= QKV_STEPS) & ((group == S_AZ) | (group == S_MZ))
    is_sigmoid = (group == S_MO) | (group == S_GA) | (group == S_GM)

    @pl.when(group == S_MV)
    def _():
        for j in range(IN_TN // SLAB):
            p_ref[j] = slab_acc(j).astype(BF16)

    @pl.when(is_silu)
    def _():
        for j in range(IN_TN // SLAB):
            p_ref[j] = _silu(slab_acc(j)).astype(BF16)

    @pl.when(is_sigmoid)
    def _():
        for j in range(IN_TN // SLAB):
            p_ref[j] = _sigmoid(slab_acc(j)).astype(BF16)

    @pl.when(is_conv)
    def _():
        step = n - CONV_STEP0
        seq_start = (i % tiles_per_seq) == 0
        post = jnp.where(step == 1, MLSTM_QK_DIM ** -0.5, 1.0).astype(F32)

        def conv_slab(j):
            cols = slice(j * SLAB, (j + 1) * SLAB)
            slot = (n + j) % 2
            acc_ref[slot, 0:CONV_HALO, :] = jnp.where(seq_start, 0.0, halo_ref[step, :, cols])
            halo_ref[step, :, cols] = acc_ref[slot, IN_TM:IN_TM + CONV_HALO, :]
            y = cb_ref[:, cols]
            for back in range(CONV_WIDTH):
                tap = acc_ref[slot, CONV_HALO - back:CONV_HALO - back + IN_TM, :]
                y = y + cw_ref[CONV_WIDTH - 1 - back:CONV_WIDTH - back, cols] * tap
            p_ref[j] = (_silu(y) * post).astype(BF16)

        n_slabs = IN_TN // SLAB
        for j in range(n_slabs):
            acc_ref[(n + j) % 2, CONV_HALO:, :] = slab_acc(j)
            if j > 0:
                conv_slab(j - 1)
        conv_slab(n_slabs - 1)


def _in_proj(x2, norm_g, w_big, w_gate, conv_w, conv_b, seq):
    t = x2.shape[0]
    n_cols = w_big.shape[0]
    grid = (t // IN_TM, n_cols // IN_TN)
    assert seq % IN_TM == 0 and CONV_STEPS * IN_TN == conv_w.shape[1]

    def conv_step(i, n):
        return (0, jnp.clip(n - CONV_STEP0, 0, CONV_STEPS - 1))

    return pl.pallas_call(
        functools.partial(_in_proj_kernel, tiles_per_seq=seq // IN_TM),
        grid=grid,
        in_specs=[
            pl.BlockSpec((IN_TM, D_MODEL), lambda i, n: (i, 0)),
            pl.BlockSpec((1, D_MODEL), lambda i, n: (0, 0)),
            pl.BlockSpec((IN_TN, D_MODEL), lambda i, n: (n, 0)),
            pl.BlockSpec((D_MODEL, GATE_LANES), lambda i, n: (0, 0)),
            pl.BlockSpec((CONV_WIDTH, IN_TN), conv_step),
            pl.BlockSpec((1, IN_TN), conv_step),
        ],
        out_specs=[
            pl.BlockSpec((IN_TN // ATTN_HEAD_DIM, IN_TM, ATTN_HEAD_DIM),
                         lambda i, n: (jnp.minimum(n, QKV_STEPS - 1), i, 0)),
            pl.BlockSpec((IN_TN // SLAB, IN_TM, SLAB),
                         lambda i, n: (jnp.maximum(n - QKV_STEPS, 0), i, 0)),
            pl.BlockSpec((IN_TM // CHUNK, 2 * MLSTM_HEADS, CHUNK), lambda i, n: (i, 0, 0)),
        ],
        out_shape=[
            jax.ShapeDtypeStruct((QKV_COLS // ATTN_HEAD_DIM, t, ATTN_HEAD_DIM), F32),
            jax.ShapeDtypeStruct(((n_cols - QKV_COLS) // SLAB, t, SLAB), BF16),
            jax.ShapeDtypeStruct((t // CHUNK, 2 * MLSTM_HEADS, CHUNK), F32),
        ],
        scratch_shapes=[pltpu.VMEM((IN_TM, D_MODEL), BF16),
                        pltpu.VMEM((CONV_STEPS, CONV_HALO, IN_TN), F32),
                        pltpu.VMEM((2, CONV_HALO + IN_TM, SLAB), F32),
                        pltpu.VMEM((2, IN_TM, ATTN_HEAD_DIM), F32)],
        compiler_params=_cparams(("arbitrary", "arbitrary")),
        name="in_proj",
    )(x2, norm_g, w_big, w_gate, conv_w, conv_b)


def _band_bias(slope, dilation, first):
    qi = lax.broadcasted_iota(jnp.int32, (BAND, 2 * BAND), 0)
    ki = lax.broadcasted_iota(jnp.int32, (BAND, 2 * BAND), 1)
    dist = BAND + qi - ki
    valid = (dist >= 0) & (dist <= BAND)
    if first:
        valid = valid & (ki >= BAND)
    return jnp.where(valid, -slope * (dist * dilation).astype(F32), MASKED)


def _dense_bias(slope, first):
    qi = lax.broadcasted_iota(jnp.int32, (BAND, 2 * BAND), 0)
    ki = lax.broadcasted_iota(jnp.int32, (BAND, 2 * BAND), 1)
    pq, jq = qi // 32, qi % 32
    half, pk, jk = ki // BAND, (ki % BAND) // 32, ki % 32
    dist = 4 * (jq - jk + 32 * (1 - half)) + pq - pk
    valid = (dist >= 0) & (dist <= BAND)
    if first:
        valid = valid & (half == 1)
    return jnp.where(valid, -slope * dist.astype(F32), MASKED)


def _wide_bias(slope):
    qi = lax.broadcasted_iota(jnp.int32, (2 * BAND, 2 * BAND), 0)
    ki = lax.broadcasted_iota(jnp.int32, (2 * BAND, 2 * BAND), 1)
    dist = qi - ki
    valid = (dist >= 0) & (dist <= BAND)
    return jnp.where(valid, -slope * (dist * 16).astype(F32), MASKED)


def _softmax_block(q, k, v, bias):
    s = lax.dot_general(q, k, (((1,), (1,)), ((), ())), preferred_element_type=F32) + bias
    m = jnp.max(s, axis=-1, keepdims=True)
    p = jnp.exp2(s - m).astype(BF16)
    v_ext = jnp.concatenate([v, jnp.ones_like(v)], axis=1)
    r = jnp.dot(p, v_ext, preferred_element_type=F32)
    hd = ATTN_HEAD_DIM
    return r[:, :hd], r[:, hd:], jnp.broadcast_to(m, (q.shape[0], hd))


def _attn_kernel(q_ref, k_ref, v_ref, o_ref,
                 x4_ref, x16_ref, num_ref, den_ref, mx_ref, bias_ref, bias16_ref, unit_ref, unit16_ref,
                 onat_ref):
    seq = q_ref.shape[0]
    n_u4, n_u16 = seq // 4, seq // 16
    slope = jnp.exp2(jnp.full((1, 1), -8.0 / ATTN_HEADS, F32) * (pl.program_id(1) + 1)) * LOG2E

    @pl.when((pl.program_id(0) == 0) & (pl.program_id(1) == 0))
    def _():
        one = jnp.ones((1, 1), F32)
        unit_ref[0, 0] = _dense_bias(one, False)
        unit_ref[0, 1] = _dense_bias(one, True)
        unit_ref[1, 0] = _band_bias(one, 4, False)
        unit_ref[1, 1] = _band_bias(one, 4, True)
        unit16_ref[...] = _wide_bias(one)

    bias_ref[...] = unit_ref[...] * slope
    bias16_ref[...] = unit16_ref[...] * slope

    quarter = IN_TM // 4
    sixteenth = IN_TM // 16
    for ti, src in enumerate((q_ref, k_ref, v_ref)):
        def regroup(tile, carry, ti=ti, src=src):
            base = pl.multiple_of(tile * IN_TM, IN_TM)
            for p4 in range(4):
                for piece in range(quarter // BAND):
                    rows = src[pl.ds(base + p4 * quarter + piece * BAND, BAND), :]
                    dst = pl.ds(pl.multiple_of(tile * quarter + piece * BAND, BAND), BAND)
                    x4_ref[ti, p4, dst, :] = rows.astype(BF16)
            for p16 in range(16):
                rows = src[pl.ds(base + (p16 % 4) * quarter + p16 // 4, sixteenth, stride=4), :]
                x16_ref[ti, p16, pl.ds(pl.multiple_of(tile * sixteenth, sixteenth), sixteenth), :] = rows.astype(BF16)
            return carry

        lax.fori_loop(0, seq // IN_TM, regroup, 0)

    def store(pattern, p, rows, result):
        num, den, mx = result
        num_ref[pattern, p, rows, :] = num
        den_ref[pattern, p, rows, :] = den
        mx_ref[pattern, p, rows, :] = mx

    def dense_block(n):
        cur, prev = slice(n * 32, (n + 1) * 32), slice(max(n - 1, 0) * 32, max(n - 1, 0) * 32 + 32)

        def rows(ti, sl):
            return [x4_ref[ti, p, sl, :] for p in range(4)]

        q = jnp.concatenate(rows(0, cur), axis=0)
        k = jnp.concatenate(rows(1, prev) + rows(1, cur), axis=0)
        v = jnp.concatenate(rows(2, prev) + rows(2, cur), axis=0)
        num, den, mx = _softmax_block(q, k, v, bias_ref[0, int(n == 0)])
        for p in range(4):
            part = slice(p * 32, (p + 1) * 32)
            store(0, p, cur, (num[part], den[part], mx[part]))

    def band_block(p, n):
        cur = slice(n * BAND, (n + 1) * BAND)
        prev = slice(max(n - 1, 0) * BAND, max(n - 1, 0) * BAND + BAND)
        q = x4_ref[0, p, cur, :]
        k = jnp.concatenate([x4_ref[1, p, prev, :], x4_ref[1, p, cur, :]], axis=0)
        v = jnp.concatenate([x4_ref[2, p, prev, :], x4_ref[2, p, cur, :]], axis=0)
        store(1, p, cur, _softmax_block(q, k, v, bias_ref[1, int(n == 0)]))

    wide_rows = n_u16 // ATTN_PARTS

    def wide_block(p16, part):
        rows = slice(part * wide_rows, (part + 1) * wide_rows)
        result = _softmax_block(x16_ref[0, p16, rows, :], x16_ref[1, p16], x16_ref[2, p16], bias16_ref[rows, :])
        store(2, p16 % 4, pl.ds(p16 // 4 + 4 * wide_rows * part, wide_rows, stride=4), result)

    def merge_block(p, n):
        rows = slice(n * BAND, (n + 1) * BAND)
        m0, m1, m2 = mx_ref[0, p, rows, :], mx_ref[1, p, rows, :], mx_ref[2, p, rows, :]
        m_all = jnp.maximum(jnp.maximum(m0, m1), m2)
        w0, w1, w2 = jnp.exp2(m0 - m_all), jnp.exp2(m1 - m_all), jnp.exp2(m2 - m_all)
        num = w0 * num_ref[0, p, rows, :] + w1 * num_ref[1, p, rows, :] + w2 * num_ref[2, p, rows, :]
        den = w0 * den_ref[0, p, rows, :] + w1 * den_ref[1, p, rows, :] + w2 * den_ref[2, p, rows, :]
        onat_ref[pl.ds(p + 4 * BAND * n, BAND, stride=4), :] = num / den

    dense_per_part = n_u4 // 32 // ATTN_PARTS
    band_per_part = n_u4 // BAND // ATTN_PARTS
    for part in range(ATTN_PARTS):
        for n in range(part * dense_per_part, (part + 1) * dense_per_part):
            dense_block(n)
        for n in range(part * band_per_part, (part + 1) * band_per_part):
            for p in range(4):
                band_block(p, n)
                if part > 0:
                    merge_block(p, n - band_per_part)
        for p16 in range(16):
            wide_block(p16, part)
    for n in range((ATTN_PARTS - 1) * band_per_part, ATTN_PARTS * band_per_part):
        for p in range(4):
            merge_block(p, n)
    o_ref[...] = onat_ref[...].astype(o_ref.dtype)


def _attention(qkv, batch, seq):
    assert seq // 16 == 2 * BAND
    hd = ATTN_HEAD_DIM
    qkv4 = qkv.reshape(3 * ATTN_HEADS, batch, seq, hd)

    def spec(which):
        return pl.BlockSpec((None, None, seq, hd), lambda b, h: (which * ATTN_HEADS + h, b, 0, 0))

    out = pl.pallas_call(
        _attn_kernel,
        grid=(batch, ATTN_HEADS),
        in_specs=[spec(0), spec(1), spec(2)],
        out_specs=pl.BlockSpec((None, None, seq, hd), lambda b, h: (h, b, 0, 0)),
        out_shape=jax.ShapeDtypeStruct((ATTN_HEADS, batch, seq, hd), BF16),
        scratch_shapes=[
            pltpu.VMEM((3, 4, seq // 4, hd), BF16),
            pltpu.VMEM((3, 16, seq // 16, hd), BF16),
            pltpu.VMEM((3, 4, seq // 4, hd), F32),
            pltpu.VMEM((3, 4, seq // 4, hd), F32),
            pltpu.VMEM((3, 4, seq // 4, hd), F32),
            pltpu.VMEM((2, 2, BAND, 2 * BAND), F32),
            pltpu.VMEM((2 * BAND, 2 * BAND), F32),
            pltpu.VMEM((2, 2, BAND, 2 * BAND), F32),
            pltpu.VMEM((2 * BAND, 2 * BAND), F32),
            pltpu.VMEM((seq, hd), F32),
        ],
        compiler_params=_cparams(("arbitrary", "arbitrary")),
        name="dilated_attention",
    )(qkv4, qkv4, qkv4)
    return out.reshape(ATTN_HEADS, batch * seq, hd)


MLSTM_TS = 512
ONES_LANES = 128


def _mlstm_kernel(bias_ref, q_ref, k_ref, v_ref, gt_ref, out_ref, c_ref, m_ref, kt_ref,
                  lhs_ref, kw_ref, floor_ref):
    s = pl.program_id(1)
    dk, dv = MLSTM_QK_DIM, MLSTM_V_DIM
    L = CHUNK

    @pl.when(s == 0)
    def _():
        c_ref[...] = jnp.zeros_like(c_ref)
        m_ref[...] = jnp.zeros_like(m_ref)

    ti = lax.broadcasted_iota(jnp.int32, (CHUNK, CHUNK), 0)
    si = lax.broadcasted_iota(jnp.int32, (CHUNK, CHUNK), 1)
    causal = ti >= si
    csum = jnp.where(ti <= si, 1.0, 0.0).astype(F32)

    lane = lax.broadcasted_iota(jnp.int32, (MLSTM_HEADS, L), 1)

    for h in range(MLSTM_HEADS):
        for c in range(MLSTM_TS // L):
            k_hc = k_ref[h // 2, c * L:(c + 1) * L, (h % 2) * dk:(h % 2 + 1) * dk]
            kt_ref[h, c] = k_hc.astype(F32).T.astype(BF16)

    n_chunks = MLSTM_TS // L
    heads = range(MLSTM_HEADS)

    def q_of(h, c):
        return q_ref[h // 2, c * L:(c + 1) * L, (h % 2) * dk:(h % 2 + 1) * dk]

    def v_ext_of(h, c):
        return jnp.concatenate([v_ref[h, c * L:(c + 1) * L, :], jnp.ones((L, ONES_LANES), BF16)], axis=1)

    w_olds = []
    for c in range(n_chunks):
        gates = gt_ref[c] + bias_ref[...]
        ig_all = gates[:MLSTM_HEADS]
        lf_all = _log_sigmoid(gates[MLSTM_HEADS:])
        b_all = jnp.dot(lf_all, csum, preferred_element_type=F32,
                        precision=lax.Precision.HIGHEST)
        c_all = ig_all - b_all
        cm_all = c_all
        shift = 1
        while shift < L:
            cm_all = jnp.maximum(cm_all, jnp.where(lane >= shift, pltpu.roll(cm_all, shift, axis=1), MASKED))
            shift *= 2
        m_prev = m_ref[...]
        b_last = b_all[:, L - 1:L]
        top = jnp.maximum(m_prev, cm_all)
        m_new = b_last + top[:, L - 1:L]
        w_old = jnp.exp(b_last + m_prev - m_new)
        w_key = jnp.exp(b_last + c_all - m_new)
        cols = jnp.concatenate([
            -top,
            jnp.exp(m_prev - top),
            jnp.exp(-(b_all + top)),
        ], axis=0).T
        m_ref[...] = m_new
        w_olds.append(w_old)

        def column(group, h):
            j = group * MLSTM_HEADS + h
            return jnp.broadcast_to(cols[:, j:j + 1], (L, L))

        scores = [jnp.dot(q_of(h, c), kt_ref[h, c], preferred_element_type=F32) for h in heads]
        for h in heads:
            w_intra = jnp.exp(jnp.where(causal, column(0, h) + c_all[h:h + 1], MASKED))
            lhs_ref[c, h] = jnp.concatenate([(scores[h] * w_intra).astype(BF16),
                                             (q_of(h, c).astype(F32) * column(1, h)).astype(BF16)], axis=1)
            kw_ref[c, h] = (kt_ref[h, c].astype(F32) * w_key[h:h + 1]).astype(BF16)
            floor_ref[c, h] = column(2, h)

    for c in range(n_chunks):
        incs = [jnp.dot(kw_ref[c, h], v_ext_of(h, c), preferred_element_type=F32) for h in heads]
        nds = [jnp.dot(lhs_ref[c, h], jnp.concatenate([v_ext_of(h, c), c_ref[h].astype(BF16)], axis=0),
                       preferred_element_type=F32) for h in heads]
        for h in heads:
            nd = nds[h]
            inv = 1.0 / jnp.maximum(jnp.abs(nd[:, dv:]), floor_ref[c, h])
            for half in range(dv // L):
                lanes = slice(half * L, (half + 1) * L)
                out_ref[h, c * L:(c + 1) * L, lanes] = (nd[:, lanes] * inv).astype(out_ref.dtype)
            c_ref[h] = w_olds[c][h:h + 1] * c_ref[h] + incs[h]


def _mlstm(slabs, gates_t, b_if, batch, seq):
    n_slabs = slabs.shape[0]
    s4d = slabs.reshape(n_slabs, batch, seq, SLAB)
    half = MLSTM_HEADS // 2
    tiles = seq // MLSTM_TS
    tile_chunks = MLSTM_TS // CHUNK

    def tile_spec(first_slab, n):
        return pl.BlockSpec((n, None, MLSTM_TS, SLAB), lambda b, s: (first_slab // n, b, s, 0))

    out = pl.pallas_call(
        _mlstm_kernel,
        grid=(batch, tiles),
        in_specs=[
            pl.BlockSpec((2 * MLSTM_HEADS, 1), lambda b, s: (0, 0)),
            tile_spec(S_MQK * GROUP_SLABS, half), tile_spec(S_MQK * GROUP_SLABS + half, half),
            tile_spec(S_MV * GROUP_SLABS, MLSTM_HEADS),
            pl.BlockSpec((tile_chunks, 2 * MLSTM_HEADS, CHUNK), lambda b, s: (b * tiles + s, 0, 0)),
        ],
        out_specs=pl.BlockSpec((MLSTM_HEADS, None, MLSTM_TS, SLAB), lambda b, s: (0, b, s, 0)),
        out_shape=jax.ShapeDtypeStruct((MLSTM_HEADS, batch, seq, SLAB), BF16),
        scratch_shapes=[
            pltpu.VMEM((MLSTM_HEADS, MLSTM_QK_DIM, MLSTM_V_DIM + ONES_LANES), F32),
            pltpu.VMEM((MLSTM_HEADS, 1), F32),
            pltpu.VMEM((MLSTM_HEADS, tile_chunks, MLSTM_QK_DIM, CHUNK), BF16),
            pltpu.VMEM((tile_chunks, MLSTM_HEADS, CHUNK, CHUNK + MLSTM_QK_DIM), BF16),
            pltpu.VMEM((tile_chunks, MLSTM_HEADS, MLSTM_QK_DIM, CHUNK), BF16),
            pltpu.VMEM((tile_chunks, MLSTM_HEADS, CHUNK, CHUNK), F32),
        ],
        compiler_params=_cparams(("parallel", "arbitrary")),
        name="mlstm",
    )(b_if.reshape(-1, 1), s4d, s4d, s4d, gates_t)
    return out.reshape(MLSTM_HEADS, batch * seq, SLAB)


MERGE_TM = 512
OUT_TM = 512


def _weight_spec():
    return pl.BlockSpec((D_MODEL, D_MODEL), lambda i: (0, 0), pipeline_mode=pl.Buffered(1))


def _branch_merge_kernel(attn_ref, az_ref, cell_ref, mo_ref, mz_ref, ga_ref, gm_ref,
                         wa_ref, wm_ref, ng_ref, merged_ref):
    n = GROUP_SLABS
    gated = []
    for c in range(n):
        attn = jnp.concatenate([attn_ref[2 * c].astype(F32), attn_ref[2 * c + 1].astype(F32)], axis=1)
        gated.append(attn * az_ref[c].astype(F32))
    gated = jnp.concatenate(gated, axis=1).astype(BF16)
    y_a, mem = [], []
    for h in range(MLSTM_HEADS):
        cols = slice(h * SLAB, (h + 1) * SLAB)
        y_a.append(jnp.dot(gated, wa_ref[:, cols], preferred_element_type=F32))
        cell = mo_ref[h].astype(F32) * cell_ref[h].astype(F32)
        cell = cell * lax.rsqrt(jnp.mean(cell * cell, axis=-1, keepdims=True) + NORM_EPS)
        cell = cell * ng_ref[:, cols]
        mem.append(cell * mz_ref[h].astype(F32))
    mem = jnp.concatenate(mem, axis=1).astype(BF16)
    for c in range(n):
        cols = slice(c * SLAB, (c + 1) * SLAB)
        y_m = jnp.dot(mem, wm_ref[:, cols], preferred_element_type=F32)
        merged = ga_ref[c].astype(F32) * y_a[c] + gm_ref[c].astype(F32) * y_m
        merged_ref[:, cols] = merged.astype(merged_ref.dtype)


def _branch_merge(attn, slabs, cell, w_a, w_m, norm_g):
    t = attn.shape[1]
    assert MLSTM_V_DIM == SLAB and MLSTM_HEADS == GROUP_SLABS

    def slab_spec(group):
        return pl.BlockSpec((GROUP_SLABS, MERGE_TM, SLAB), lambda i: (group, i, 0))

    return pl.pallas_call(
        _branch_merge_kernel,
        grid=(t // MERGE_TM,),
        in_specs=[
            pl.BlockSpec((ATTN_HEADS, MERGE_TM, ATTN_HEAD_DIM), lambda i: (0, i, 0)),
            slab_spec(S_AZ), slab_spec(0), slab_spec(S_MO), slab_spec(S_MZ),
            slab_spec(S_GA), slab_spec(S_GM),
            _weight_spec(), _weight_spec(),
            pl.BlockSpec((1, D_MODEL), lambda i: (0, 0)),
        ],
        out_specs=pl.BlockSpec((MERGE_TM, D_MODEL), lambda i: (i, 0)),
        out_shape=jax.ShapeDtypeStruct((t, D_MODEL), BF16),
        compiler_params=_cparams(("parallel",)),
        name="branch_merge",
    )(attn, slabs, cell, slabs, slabs, slabs, slabs, w_a, w_m, norm_g)


def _out_proj_kernel(merged_ref, x_ref, wo_ref, fg_ref, out_ref):
    y = x_ref[...] + jnp.dot(merged_ref[...], wo_ref[...], preferred_element_type=F32)
    y = y * lax.rsqrt(jnp.mean(y * y, axis=-1, keepdims=True) + NORM_EPS)
    out_ref[...] = y * fg_ref[...]


def _out_proj(merged, x2, w_o, final_g):
    t = x2.shape[0]
    return pl.pallas_call(
        _out_proj_kernel,
        grid=(t // OUT_TM,),
        in_specs=[
            pl.BlockSpec((OUT_TM, D_MODEL), lambda i: (i, 0)),
            pl.BlockSpec((OUT_TM, D_MODEL), lambda i: (i, 0)),
            _weight_spec(),
            pl.BlockSpec((1, D_MODEL), lambda i: (0, 0)),
        ],
        out_specs=pl.BlockSpec((OUT_TM, D_MODEL), lambda i: (i, 0)),
        out_shape=jax.ShapeDtypeStruct((t, D_MODEL), F32),
        compiler_params=_cparams(("parallel",)),
        name="out_proj",
    )(merged, x2, w_o, final_g)


def kernel(x, norm_g, w_in, b_if, conv_w, conv_b, mlstm_norm_g, w_attn_branch, w_mlstm_branch,
           w_out, final_norm_g):
    batch, seq, d = x.shape
    assert d == D_MODEL and seq % (16 * BAND) == 0 and (batch * seq) % IN_TM == 0
    t = batch * seq
    x2 = x.reshape(t, d)

    w_big, w_gate = _weight_prep(w_in.T)
    assert w_big.shape[0] == QKV_COLS + N_SLAB_GROUPS * D_MODEL

    qkv, slabs, gates_t = _in_proj(x2, norm_g.reshape(1, d), w_big, w_gate, conv_w,
                                   conv_b.reshape(1, -1), seq)
    attn = _attention(qkv, batch, seq)
    cell = _mlstm(slabs, gates_t, b_if, batch, seq)
    merged = _branch_merge(attn, slabs, cell, w_attn_branch.astype(BF16), w_mlstm_branch.astype(BF16),
                           mlstm_norm_g.reshape(1, d))
    out = _out_proj(merged, x2, w_out.astype(BF16), final_norm_g.reshape(1, d))
    return out.reshape(batch, seq, d)
```

```python
import functools

import jax
import jax.numpy as jnp
from jax import lax
from jax.experimental import pallas as pl
from jax.experimental.pallas import tpu as pltpu

F32 = jnp.float32
BF16 = jnp.bfloat16

D_MODEL = 2048
ATTN_HEADS = 16
ATTN_HEAD_DIM = 128
MLSTM_HEADS = 8
MLSTM_QK_DIM = 128
MLSTM_V_DIM = 256
CONV_WIDTH = 4
NORM_EPS = 1e-6
BAND = 128
ATTN_PARTS = 2
SLAB = 256
GROUP_SLABS = D_MODEL // SLAB
QKV_COLS = 3 * D_MODEL
S_AZ, S_MQK, S_MV, S_MO, S_MZ, S_GA, S_GM = range(7)
N_SLAB_GROUPS = 7
GATE_LANES = 128
MASKED = -1e30
CHUNK = 128
VMEM_LIMIT = 56 * 1024 * 1024


def _cparams(sem):
    return pltpu.CompilerParams(dimension_semantics=sem, vmem_limit_bytes=VMEM_LIMIT)


LOG2E = 1.4426950408889634


def _sigmoid(x):
    return 0.5 + 0.5 * jnp.tanh(0.5 * x)


def _silu(x):
    half = 0.5 * x
    return half + half * jnp.tanh(half)


def _log_sigmoid(x):
    return jnp.minimum(x, 0.0) - jnp.log(1.0 + jnp.exp(-jnp.abs(x)))


IN_TM = 1024
IN_TN = 1024
QKV_STEPS = QKV_COLS // IN_TN

GATE_COLS = 2 * MLSTM_HEADS
GATE_COL0 = 4 * D_MODEL + 2 * MLSTM_HEADS * MLSTM_QK_DIM + MLSTM_HEADS * MLSTM_V_DIM
GATE_STEP = GATE_COL0 // IN_TN


CONV_STEP0 = QKV_STEPS + S_MQK * (D_MODEL // IN_TN)
CONV_STEPS = D_MODEL // IN_TN
CONV_HALO = 8


def _in_proj_kernel(x_ref, g_ref, w_ref, wtail_ref, wg_ref, cw_ref, cb_ref, qkv_ref, p_ref, gt_ref,
                    hn_ref, halo_ref, acc_ref, stage_ref, *, tiles_per_seq):
    i = pl.program_id(0)
    n = pl.program_id(1)

    @pl.when((i == 0) & (n == 0))
    def _():
        halo_ref[...] = jnp.zeros_like(halo_ref)

    @pl.when(n == 0)
    def _():
        x = x_ref[...]
        ms = jnp.mean(x * x, axis=-1, keepdims=True)
        hn = (x * lax.rsqrt(ms + NORM_EPS) * g_ref[...]).astype(BF16)
        hn_ref[...] = hn
        wg = jnp.concatenate([wg_ref[...], jnp.zeros((GATE_LANES - GATE_COLS, D_MODEL), F32)], axis=0)
        gates = lax.dot_general(hn, wg.astype(BF16), (((1,), (1,)), ((), ())),
                                preferred_element_type=F32)
        gates_t = gates.T
        for c in range(IN_TM // CHUNK):
            gt_ref[c] = gates_t[:2 * MLSTM_HEADS, c * CHUNK:(c + 1) * CHUNK]

    def slab_weights(j, shifted):
        lo = j * SLAB + (GATE_COLS if shifted else 0)
        if lo + SLAB <= IN_TN:
            w = w_ref[lo:lo + SLAB, :]
        else:
            w = jnp.concatenate([w_ref[lo:, :], wtail_ref[...]], axis=0)
        return w.astype(BF16)

    def slab_acc(j, shifted=False):
        return lax.dot_general(hn_ref[...], slab_weights(j, shifted), (((1,), (1,)), ((), ())),
                               preferred_element_type=F32)

    @pl.when(n < QKV_STEPS)
    def _():
        scale = jnp.where(n < D_MODEL // IN_TN, ATTN_HEAD_DIM ** -0.5 * LOG2E, 1.0).astype(F32)
        quarter = IN_TM // 4
        for j in range(IN_TN // SLAB):
            acc = slab_acc(j) * scale
            for hh in range(SLAB // ATTN_HEAD_DIM):
                slot = (2 * j + hh) % 2
                stage_ref[slot] = acc[:, hh * ATTN_HEAD_DIM:(hh + 1) * ATTN_HEAD_DIM]
                for p in range(4):
                    qkv_ref[2 * j + hh, p * quarter:(p + 1) * quarter, :] = (
                        stage_ref[slot, pl.ds(p, quarter, stride=4), :])

    is_conv = (n >= CONV_STEP0) & (n < CONV_STEP0 + CONV_STEPS)

    group = (n - QKV_STEPS) // (D_MODEL // IN_TN)
    is_sigmoid = (group == S_MO) | (group == S_GA) | (group == S_GM)

    @pl.when(group == S_MV)
    def _():
        for j in range(IN_TN // SLAB):
            p_ref[j] = slab_acc(j).astype(BF16)

    @pl.when((n >= QKV_STEPS) & (group == S_AZ))
    def _():
        for j in range(IN_TN // SLAB):
            p_ref[j] = _silu(slab_acc(j)).astype(BF16)

    @pl.when(group == S_MZ)
    def _():
        for j in range(IN_TN // SLAB):
            p_ref[j] = _silu(slab_acc(j, shifted=True)).astype(BF16)

    @pl.when(is_sigmoid)
    def _():
        for j in range(IN_TN // SLAB):
            p_ref[j] = _sigmoid(slab_acc(j, shifted=True)).astype(BF16)

    @pl.when(is_conv)
    def _():
        step = n - CONV_STEP0
        seq_start = (i % tiles_per_seq) == 0
        post = jnp.where(step == 1, MLSTM_QK_DIM ** -0.5, 1.0).astype(F32)

        def conv_slab(j):
            cols = slice(j * SLAB, (j + 1) * SLAB)
            slot = (n + j) % 2
            acc_ref[slot, 0:CONV_HALO, :] = jnp.where(seq_start, 0.0, halo_ref[step, :, cols])
            halo_ref[step, :, cols] = acc_ref[slot, IN_TM:IN_TM + CONV_HALO, :]
            y = cb_ref[:, cols]
            for back in range(CONV_WIDTH):
                tap = acc_ref[slot, CONV_HALO - back:CONV_HALO - back + IN_TM, :]
                y = y + cw_ref[CONV_WIDTH - 1 - back:CONV_WIDTH - back, cols] * tap
            p_ref[j] = (_silu(y) * post).astype(BF16)

        n_slabs = IN_TN // SLAB
        for j in range(n_slabs):
            acc_ref[(n + j) % 2, CONV_HALO:, :] = slab_acc(j)
            if j > 0:
                conv_slab(j - 1)
        conv_slab(n_slabs - 1)


def _in_proj(x2, norm_g, w_t, conv_w, conv_b, seq):
    t = x2.shape[0]
    n_cols = w_t.shape[0] - GATE_COLS
    grid = (t // IN_TM, n_cols // IN_TN)
    assert seq % IN_TM == 0 and CONV_STEPS * IN_TN == conv_w.shape[1]
    assert GATE_COL0 == (QKV_STEPS + S_MO * (D_MODEL // IN_TN)) * IN_TN and n_cols % IN_TN == 0

    def conv_step(i, n):
        return (0, jnp.clip(n - CONV_STEP0, 0, CONV_STEPS - 1))

    return pl.pallas_call(
        functools.partial(_in_proj_kernel, tiles_per_seq=seq // IN_TM),
        grid=grid,
        in_specs=[
            pl.BlockSpec((IN_TM, D_MODEL), lambda i, n: (i, 0)),
            pl.BlockSpec((1, D_MODEL), lambda i, n: (0, 0)),
            pl.BlockSpec((IN_TN, D_MODEL), lambda i, n: (n, 0)),
            pl.BlockSpec((GATE_COLS, D_MODEL), lambda i, n: ((n + 1) * (IN_TN // GATE_COLS), 0)),
            pl.BlockSpec((GATE_COLS, D_MODEL), lambda i, n: (GATE_COL0 // GATE_COLS, 0)),
            pl.BlockSpec((CONV_WIDTH, IN_TN), conv_step),
            pl.BlockSpec((1, IN_TN), conv_step),
        ],
        out_specs=[
            pl.BlockSpec((IN_TN // ATTN_HEAD_DIM, IN_TM, ATTN_HEAD_DIM),
                         lambda i, n: (jnp.minimum(n, QKV_STEPS - 1), i, 0)),
            pl.BlockSpec((IN_TN // SLAB, IN_TM, SLAB),
                         lambda i, n: (jnp.maximum(n - QKV_STEPS, 0), i, 0)),
            pl.BlockSpec((IN_TM // CHUNK, 2 * MLSTM_HEADS, CHUNK), lambda i, n: (i, 0, 0)),
        ],
        out_shape=[
            jax.ShapeDtypeStruct((QKV_COLS // ATTN_HEAD_DIM, t, ATTN_HEAD_DIM), F32),
            jax.ShapeDtypeStruct(((n_cols - QKV_COLS) // SLAB, t, SLAB), BF16),
            jax.ShapeDtypeStruct((t // CHUNK, 2 * MLSTM_HEADS, CHUNK), F32),
        ],
        scratch_shapes=[pltpu.VMEM((IN_TM, D_MODEL), BF16),
                        pltpu.VMEM((CONV_STEPS, CONV_HALO, IN_TN), F32),
                        pltpu.VMEM((2, CONV_HALO + IN_TM, SLAB), F32),
                        pltpu.VMEM((2, IN_TM, ATTN_HEAD_DIM), F32)],
        compiler_params=_cparams(("arbitrary", "arbitrary")),
        name="in_proj",
    )(x2, norm_g, w_t, w_t, w_t, conv_w, conv_b)


def _band_bias(slope, dilation, first):
    qi = lax.broadcasted_iota(jnp.int32, (BAND, 2 * BAND), 0)
    ki = lax.broadcasted_iota(jnp.int32, (BAND, 2 * BAND), 1)
    dist = BAND + qi - ki
    valid = (dist >= 0) & (dist <= BAND)
    if first:
        valid = valid & (ki >= BAND)
    return jnp.where(valid, -slope * (dist * dilation).astype(F32), MASKED)


def _dense_bias(slope, first):
    qi = lax.broadcasted_iota(jnp.int32, (BAND, 2 * BAND), 0)
    ki = lax.broadcasted_iota(jnp.int32, (BAND, 2 * BAND), 1)
    pq, jq = qi // 32, qi % 32
    half, pk, jk = ki // BAND, (ki % BAND) // 32, ki % 32
    dist = 4 * (jq - jk + 32 * (1 - half)) + pq - pk
    valid = (dist >= 0) & (dist <= BAND)
    if first:
        valid = valid & (half == 1)
    return jnp.where(valid, -slope * dist.astype(F32), MASKED)


def _wide_bias(slope):
    qi = lax.broadcasted_iota(jnp.int32, (2 * BAND, 2 * BAND), 0)
    ki = lax.broadcasted_iota(jnp.int32, (2 * BAND, 2 * BAND), 1)
    dist = qi - ki
    valid = (dist >= 0) & (dist <= BAND)
    return jnp.where(valid, -slope * (dist * 16).astype(F32), MASKED)


def _softmax_block(q, k, v, bias):
    s = lax.dot_general(q, k, (((1,), (1,)), ((), ())), preferred_element_type=F32) + bias
    m = jnp.max(s, axis=-1, keepdims=True)
    p = jnp.exp2(s - m).astype(BF16)
    v_ext = jnp.concatenate([v, jnp.ones_like(v)], axis=1)
    r = jnp.dot(p, v_ext, preferred_element_type=F32)
    hd = ATTN_HEAD_DIM
    return r[:, :hd], r[:, hd:], jnp.broadcast_to(m, (q.shape[0], hd))


def _attn_kernel(q_ref, k_ref, v_ref, o_ref,
                 x4_ref, x16_ref, num_ref, den_ref, mx_ref, bias_ref, bias16_ref, unit_ref, unit16_ref,
                 onat_ref):
    seq = q_ref.shape[0]
    n_u4, n_u16 = seq // 4, seq // 16
    slope = jnp.exp2(jnp.full((1, 1), -8.0 / ATTN_HEADS, F32) * (pl.program_id(1) + 1)) * LOG2E

    @pl.when((pl.program_id(0) == 0) & (pl.program_id(1) == 0))
    def _():
        one = jnp.ones((1, 1), F32)
        unit_ref[0, 0] = _dense_bias(one, False)
        unit_ref[0, 1] = _dense_bias(one, True)
        unit_ref[1, 0] = _band_bias(one, 4, False)
        unit_ref[1, 1] = _band_bias(one, 4, True)
        unit16_ref[...] = _wide_bias(one)

    bias_ref[...] = unit_ref[...] * slope
    bias16_ref[...] = unit16_ref[...] * slope

    quarter = IN_TM // 4
    sixteenth = IN_TM // 16
    for ti, src in enumerate((q_ref, k_ref, v_ref)):
        def regroup(tile, carry, ti=ti, src=src):
            base = pl.multiple_of(tile * IN_TM, IN_TM)
            for p4 in range(4):
                for piece in range(quarter // BAND):
                    rows = src[pl.ds(base + p4 * quarter + piece * BAND, BAND), :]
                    dst = pl.ds(pl.multiple_of(tile * quarter + piece * BAND, BAND), BAND)
                    x4_ref[ti, p4, dst, :] = rows.astype(BF16)
            for p16 in range(16):
                rows = src[pl.ds(base + (p16 % 4) * quarter + p16 // 4, sixteenth, stride=4), :]
                x16_ref[ti, p16, pl.ds(pl.multiple_of(tile * sixteenth, sixteenth), sixteenth), :] = rows.astype(BF16)
            return carry

        lax.fori_loop(0, seq // IN_TM, regroup, 0)

    def store(pattern, p, rows, result):
        num, den, mx = result
        num_ref[pattern, p, rows, :] = num
        den_ref[pattern, p, rows, :] = den
        mx_ref[pattern, p, rows, :] = mx

    def dense_block(n):
        cur, prev = slice(n * 32, (n + 1) * 32), slice(max(n - 1, 0) * 32, max(n - 1, 0) * 32 + 32)

        def rows(ti, sl):
            return [x4_ref[ti, p, sl, :] for p in range(4)]

        q = jnp.concatenate(rows(0, cur), axis=0)
        k = jnp.concatenate(rows(1, prev) + rows(1, cur), axis=0)
        v = jnp.concatenate(rows(2, prev) + rows(2, cur), axis=0)
        num, den, mx = _softmax_block(q, k, v, bias_ref[0, int(n == 0)])
        for p in range(4):
            part = slice(p * 32, (p + 1) * 32)
            store(0, p, cur, (num[part], den[part], mx[part]))

    def band_block(p, n):
        cur = slice(n * BAND, (n + 1) * BAND)
        prev = slice(max(n - 1, 0) * BAND, max(n - 1, 0) * BAND + BAND)
        q = x4_ref[0, p, cur, :]
        k = jnp.concatenate([x4_ref[1, p, prev, :], x4_ref[1, p, cur, :]], axis=0)
        v = jnp.concatenate([x4_ref[2, p, prev, :], x4_ref[2, p, cur, :]], axis=0)
        store(1, p, cur, _softmax_block(q, k, v, bias_ref[1, int(n == 0)]))

    wide_rows = n_u16 // ATTN_PARTS

    def wide_block(p16, part):
        rows = slice(part * wide_rows, (part + 1) * wide_rows)
        result = _softmax_block(x16_ref[0, p16, rows, :], x16_ref[1, p16], x16_ref[2, p16], bias16_ref[rows, :])
        store(2, p16 % 4, pl.ds(p16 // 4 + 4 * wide_rows * part, wide_rows, stride=4), result)

    def merge_block(p, n):
        rows = slice(n * BAND, (n + 1) * BAND)
        m0, m1, m2 = mx_ref[0, p, rows, :], mx_ref[1, p, rows, :], mx_ref[2, p, rows, :]
        m_all = jnp.maximum(jnp.maximum(m0, m1), m2)
        w0, w1, w2 = jnp.exp2(m0 - m_all), jnp.exp2(m1 - m_all), jnp.exp2(m2 - m_all)
        num = w0 * num_ref[0, p, rows, :] + w1 * num_ref[1, p, rows, :] + w2 * num_ref[2, p, rows, :]
        den = w0 * den_ref[0, p, rows, :] + w1 * den_ref[1, p, rows, :] + w2 * den_ref[2, p, rows, :]
        onat_ref[pl.ds(p + 4 * BAND * n, BAND, stride=4), :] = num / den

    dense_per_part = n_u4 // 32 // ATTN_PARTS
    band_per_part = n_u4 // BAND // ATTN_PARTS
    for part in range(ATTN_PARTS):
        for n in range(part * dense_per_part, (part + 1) * dense_per_part):
            dense_block(n)
        for n in range(part * band_per_part, (part + 1) * band_per_part):
            for p in range(4):
                band_block(p, n)
                if part > 0:
                    merge_block(p, n - band_per_part)
        for p16 in range(16):
            wide_block(p16, part)
    for n in range((ATTN_PARTS - 1) * band_per_part, ATTN_PARTS * band_per_part):
        for p in range(4):
            merge_block(p, n)
    o_ref[...] = onat_ref[...].astype(o_ref.dtype)


def _attention(qkv, batch, seq):
    assert seq // 16 == 2 * BAND
    hd = ATTN_HEAD_DIM
    qkv4 = qkv.reshape(3 * ATTN_HEADS, batch, seq, hd)

    def spec(which):
        return pl.BlockSpec((None, None, seq, hd), lambda b, h: (which * ATTN_HEADS + h, b, 0, 0))

    out = pl.pallas_call(
        _attn_kernel,
        grid=(batch, ATTN_HEADS),
        in_specs=[spec(0), spec(1), spec(2)],
        out_specs=pl.BlockSpec((None, None, seq, hd), lambda b, h: (h, b, 0, 0)),
        out_shape=jax.ShapeDtypeStruct((ATTN_HEADS, batch, seq, hd), BF16),
        scratch_shapes=[
            pltpu.VMEM((3, 4, seq // 4, hd), BF16),
            pltpu.VMEM((3, 16, seq // 16, hd), BF16),
            pltpu.VMEM((3, 4, seq // 4, hd), F32),
            pltpu.VMEM((3, 4, seq // 4, hd), F32),
            pltpu.VMEM((3, 4, seq // 4, hd), F32),
            pltpu.VMEM((2, 2, BAND, 2 * BAND), F32),
            pltpu.VMEM((2 * BAND, 2 * BAND), F32),
            pltpu.VMEM((2, 2, BAND, 2 * BAND), F32),
            pltpu.VMEM((2 * BAND, 2 * BAND), F32),
            pltpu.VMEM((seq, hd), F32),
        ],
        compiler_params=_cparams(("arbitrary", "arbitrary")),
        name="dilated_attention",
    )(qkv4, qkv4, qkv4)
    return out.reshape(ATTN_HEADS, batch * seq, hd)


MLSTM_TS = 512
ONES_LANES = 128


def _mlstm_kernel(bias_ref, q_ref, k_ref, v_ref, gt_ref, out_ref, c_ref, m_ref, kt_ref,
                  lhs_ref, kw_ref, floor_ref):
    s = pl.program_id(1)
    dk, dv = MLSTM_QK_DIM, MLSTM_V_DIM
    L = CHUNK

    @pl.when(s == 0)
    def _():
        c_ref[...] = jnp.zeros_like(c_ref)
        m_ref[...] = jnp.zeros_like(m_ref)

    ti = lax.broadcasted_iota(jnp.int32, (CHUNK, CHUNK), 0)
    si = lax.broadcasted_iota(jnp.int32, (CHUNK, CHUNK), 1)
    causal = ti >= si
    csum = jnp.where(ti <= si, 1.0, 0.0).astype(F32)

    lane = lax.broadcasted_iota(jnp.int32, (MLSTM_HEADS, L), 1)

    for h in range(MLSTM_HEADS):
        for c in range(MLSTM_TS // L):
            k_hc = k_ref[h // 2, c * L:(c + 1) * L, (h % 2) * dk:(h % 2 + 1) * dk]
            kt_ref[h, c] = k_hc.astype(F32).T.astype(BF16)

    n_chunks = MLSTM_TS // L
    heads = range(MLSTM_HEADS)

    def q_of(h, c):
        return q_ref[h // 2, c * L:(c + 1) * L, (h % 2) * dk:(h % 2 + 1) * dk]

    def v_ext_of(h, c):
        return jnp.concatenate([v_ref[h, c * L:(c + 1) * L, :], jnp.ones((L, ONES_LANES), BF16)], axis=1)

    w_olds = []
    for c in range(n_chunks):
        gates = gt_ref[c] + bias_ref[...]
        ig_all = gates[:MLSTM_HEADS]
        lf_all = _log_sigmoid(gates[MLSTM_HEADS:])
        b_all = jnp.dot(lf_all, csum, preferred_element_type=F32,
                        precision=lax.Precision.HIGHEST)
        c_all = ig_all - b_all
        cm_all = c_all
        shift = 1
        while shift < L:
            cm_all = jnp.maximum(cm_all, jnp.where(lane >= shift, pltpu.roll(cm_all, shift, axis=1), MASKED))
            shift *= 2
        m_prev = m_ref[...]
        b_last = b_all[:, L - 1:L]
        top = jnp.maximum(m_prev, cm_all)
        m_new = b_last + top[:, L - 1:L]
        w_old = jnp.exp(b_last + m_prev - m_new)
        w_key = jnp.exp(b_last + c_all - m_new)
        cols = jnp.concatenate([
            -top,
            jnp.exp(m_prev - top),
            jnp.exp(-(b_all + top)),
        ], axis=0).T
        m_ref[...] = m_new
        w_olds.append(w_old)

        def column(group, h):
            j = group * MLSTM_HEADS + h
            return jnp.broadcast_to(cols[:, j:j + 1], (L, L))

        scores = [jnp.dot(q_of(h, c), kt_ref[h, c], preferred_element_type=F32) for h in heads]
        for h in heads:
            w_intra = jnp.exp(jnp.where(causal, column(0, h) + c_all[h:h + 1], MASKED))
            lhs_ref[c, h] = jnp.concatenate([(scores[h] * w_intra).astype(BF16),
                                             (q_of(h, c).astype(F32) * column(1, h)).astype(BF16)], axis=1)
            kw_ref[c, h] = (kt_ref[h, c].astype(F32) * w_key[h:h + 1]).astype(BF16)
            floor_ref[c, h] = column(2, h)

    for c in range(n_chunks):
        incs = [jnp.dot(kw_ref[c, h], v_ext_of(h, c), preferred_element_type=F32) for h in heads]
        nds = [jnp.dot(lhs_ref[c, h], jnp.concatenate([v_ext_of(h, c), c_ref[h].astype(BF16)], axis=0),
                       preferred_element_type=F32) for h in heads]
        for h in heads:
            nd = nds[h]
            inv = 1.0 / jnp.maximum(jnp.abs(nd[:, dv:]), floor_ref[c, h])
            for half in range(dv // L):
                lanes = slice(half * L, (half + 1) * L)
                out_ref[h, c * L:(c + 1) * L, lanes] = (nd[:, lanes] * inv).astype(out_ref.dtype)
            c_ref[h] = w_olds[c][h:h + 1] * c_ref[h] + incs[h]


def _mlstm(slabs, gates_t, b_if, batch, seq):
    n_slabs = slabs.shape[0]
    s4d = slabs.reshape(n_slabs, batch, seq, SLAB)
    half = MLSTM_HEADS // 2
    tiles = seq // MLSTM_TS
    tile_chunks = MLSTM_TS // CHUNK

    def tile_spec(first_slab, n):
        return pl.BlockSpec((n, None, MLSTM_TS, SLAB), lambda b, s: (first_slab // n, b, s, 0))

    out = pl.pallas_call(
        _mlstm_kernel,
        grid=(batch, tiles),
        in_specs=[
            pl.BlockSpec((2 * MLSTM_HEADS, 1), lambda b, s: (0, 0)),
            tile_spec(S_MQK * GROUP_SLABS, half), tile_spec(S_MQK * GROUP_SLABS + half, half),
            tile_spec(S_MV * GROUP_SLABS, MLSTM_HEADS),
            pl.BlockSpec((tile_chunks, 2 * MLSTM_HEADS, CHUNK), lambda b, s: (b * tiles + s, 0, 0)),
        ],
        out_specs=pl.BlockSpec((MLSTM_HEADS, None, MLSTM_TS, SLAB), lambda b, s: (0, b, s, 0)),
        out_shape=jax.ShapeDtypeStruct((MLSTM_HEADS, batch, seq, SLAB), BF16),
        scratch_shapes=[
            pltpu.VMEM((MLSTM_HEADS, MLSTM_QK_DIM, MLSTM_V_DIM + ONES_LANES), F32),
            pltpu.VMEM((MLSTM_HEADS, 1), F32),
            pltpu.VMEM((MLSTM_HEADS, tile_chunks, MLSTM_QK_DIM, CHUNK), BF16),
            pltpu.VMEM((tile_chunks, MLSTM_HEADS, CHUNK, CHUNK + MLSTM_QK_DIM), BF16),
            pltpu.VMEM((tile_chunks, MLSTM_HEADS, MLSTM_QK_DIM, CHUNK), BF16),
            pltpu.VMEM((tile_chunks, MLSTM_HEADS, CHUNK, CHUNK), F32),
        ],
        compiler_params=_cparams(("parallel", "arbitrary")),
        name="mlstm",
    )(b_if.reshape(-1, 1), s4d, s4d, s4d, gates_t)
    return out.reshape(MLSTM_HEADS, batch * seq, SLAB)


MERGE_TM = 512
OUT_TM = 512


def _weight_spec():
    return pl.BlockSpec((D_MODEL, D_MODEL), lambda i: (0, 0), pipeline_mode=pl.Buffered(1))


def _branch_merge_kernel(attn_ref, az_ref, cell_ref, mo_ref, mz_ref, ga_ref, gm_ref,
                         wa_ref, wm_ref, ng_ref, merged_ref):
    n = GROUP_SLABS
    gated = []
    for c in range(n):
        attn = jnp.concatenate([attn_ref[2 * c].astype(F32), attn_ref[2 * c + 1].astype(F32)], axis=1)
        gated.append(attn * az_ref[c].astype(F32))
    gated = jnp.concatenate(gated, axis=1).astype(BF16)
    y_a, mem = [], []
    for h in range(MLSTM_HEADS):
        cols = slice(h * SLAB, (h + 1) * SLAB)
        y_a.append(jnp.dot(gated, wa_ref[:, cols], preferred_element_type=F32))
        cell = mo_ref[h].astype(F32) * cell_ref[h].astype(F32)
        cell = cell * lax.rsqrt(jnp.mean(cell * cell, axis=-1, keepdims=True) + NORM_EPS)
        cell = cell * ng_ref[:, cols]
        mem.append(cell * mz_ref[h].astype(F32))
    mem = jnp.concatenate(mem, axis=1).astype(BF16)
    for c in range(n):
        cols = slice(c * SLAB, (c + 1) * SLAB)
        y_m = jnp.dot(mem, wm_ref[:, cols], preferred_element_type=F32)
        merged = ga_ref[c].astype(F32) * y_a[c] + gm_ref[c].astype(F32) * y_m
        merged_ref[:, cols] = merged.astype(merged_ref.dtype)


def _branch_merge(attn, slabs, cell, w_a, w_m, norm_g):
    t = attn.shape[1]
    assert MLSTM_V_DIM == SLAB and MLSTM_HEADS == GROUP_SLABS

    def slab_spec(group):
        return pl.BlockSpec((GROUP_SLABS, MERGE_TM, SLAB), lambda i: (group, i, 0))

    return pl.pallas_call(
        _branch_merge_kernel,
        grid=(t // MERGE_TM,),
        in_specs=[
            pl.BlockSpec((ATTN_HEADS, MERGE_TM, ATTN_HEAD_DIM), lambda i: (0, i, 0)),
            slab_spec(S_AZ), slab_spec(0), slab_spec(S_MO), slab_spec(S_MZ),
            slab_spec(S_GA), slab_spec(S_GM),
            _weight_spec(), _weight_spec(),
            pl.BlockSpec((1, D_MODEL), lambda i: (0, 0)),
        ],
        out_specs=pl.BlockSpec((MERGE_TM, D_MODEL), lambda i: (i, 0)),
        out_shape=jax.ShapeDtypeStruct((t, D_MODEL), BF16),
        compiler_params=_cparams(("parallel",)),
        name="branch_merge",
    )(attn, slabs, cell, slabs, slabs, slabs, slabs, w_a, w_m, norm_g)


def _out_proj_kernel(merged_ref, x_ref, wo_ref, fg_ref, out_ref):
    y = x_ref[...] + jnp.dot(merged_ref[...], wo_ref[...], preferred_element_type=F32)
    y = y * lax.rsqrt(jnp.mean(y * y, axis=-1, keepdims=True) + NORM_EPS)
    out_ref[...] = y * fg_ref[...]


def _out_proj(merged, x2, w_o, final_g):
    t = x2.shape[0]
    return pl.pallas_call(
        _out_proj_kernel,
        grid=(t // OUT_TM,),
        in_specs=[
            pl.BlockSpec((OUT_TM, D_MODEL), lambda i: (i, 0)),
            pl.BlockSpec((OUT_TM, D_MODEL), lambda i: (i, 0)),
            _weight_spec(),
            pl.BlockSpec((1, D_MODEL), lambda i: (0, 0)),
        ],
        out_specs=pl.BlockSpec((OUT_TM, D_MODEL), lambda i: (i, 0)),
        out_shape=jax.ShapeDtypeStruct((t, D_MODEL), F32),
        compiler_params=_cparams(("parallel",)),
        name="out_proj",
    )(merged, x2, w_o, final_g)


def kernel(x, norm_g, w_in, b_if, conv_w, conv_b, mlstm_norm_g, w_attn_branch, w_mlstm_branch,
           w_out, final_norm_g):
    batch, seq, d = x.shape
    assert d == D_MODEL and seq % (16 * BAND) == 0 and (batch * seq) % IN_TM == 0
    t = batch * seq
    x2 = x.reshape(t, d)

    assert w_in.shape[1] == QKV_COLS + N_SLAB_GROUPS * D_MODEL + GATE_COLS
    qkv, slabs, gates_t = _in_proj(x2, norm_g.reshape(1, d), w_in.T, conv_w, conv_b.reshape(1, -1), seq)
    attn = _attention(qkv, batch, seq)
    cell = _mlstm(slabs, gates_t, b_if, batch, seq)
    merged = _branch_merge(attn, slabs, cell, w_attn_branch.astype(BF16), w_mlstm_branch.astype(BF16),
                           mlstm_norm_g.reshape(1, d))
    out = _out_proj(merged, x2, w_out.astype(BF16), final_norm_g.reshape(1, d))
    return out.reshape(batch, seq, d)
```

```python
import functools

import jax
import jax.numpy as jnp
from jax import lax
from jax.experimental import pallas as pl
from jax.experimental.pallas import tpu as pltpu

F32 = jnp.float32
BF16 = jnp.bfloat16

D_MODEL = 2048
ATTN_HEADS = 16
ATTN_HEAD_DIM = 128
MLSTM_HEADS = 8
MLSTM_QK_DIM = 128
MLSTM_V_DIM = 256
CONV_WIDTH = 4
NORM_EPS = 1e-6
BAND = 128
ATTN_PARTS = 2
SLAB = 256
GROUP_SLABS = D_MODEL // SLAB
QKV_COLS = 3 * D_MODEL
S_AZ, S_MQK, S_MV, S_MO, S_MZ, S_GA, S_GM = range(7)
N_SLAB_GROUPS = 7
GATE_LANES = 128
MASKED = -1e30
CHUNK = 128
VMEM_LIMIT = 56 * 1024 * 1024


def _cparams(sem):
    return pltpu.CompilerParams(dimension_semantics=sem, vmem_limit_bytes=VMEM_LIMIT)


LOG2E = 1.4426950408889634


def _sigmoid(x):
    return 0.5 + 0.5 * jnp.tanh(0.5 * x)


def _silu(x):
    half = 0.5 * x
    return half + half * jnp.tanh(half)


def _log_sigmoid(x):
    return jnp.minimum(x, 0.0) - jnp.log(1.0 + jnp.exp(-jnp.abs(x)))


IN_TM = 1024
IN_TN = 1024
QKV_STEPS = QKV_COLS // IN_TN

GATE_COLS = 2 * MLSTM_HEADS
GATE_COL0 = 4 * D_MODEL + 2 * MLSTM_HEADS * MLSTM_QK_DIM + MLSTM_HEADS * MLSTM_V_DIM
GATE_STEP = GATE_COL0 // IN_TN

CAST_ROWS = 32
CAST_BLOCKS = D_MODEL // CAST_ROWS


CONV_STEP0 = QKV_STEPS + S_MQK * (D_MODEL // IN_TN)
CONV_STEPS = D_MODEL // IN_TN
CONV_HALO = 8


def _in_proj_kernel(x_ref, g_ref, w_ref, wtail_ref, wg_ref, cw_ref, cb_ref, late0_ref, late1_ref, late2_ref,
                    qkv_ref, p_ref, gt_ref, cast0_ref, cast1_ref, cast2_ref,
                    hn_ref, halo_ref, acc_ref, stage_ref, *, tiles_per_seq):
    i = pl.program_id(0)
    n = pl.program_id(1)

    grid_step = i * pl.num_programs(1) + n
    for k, (src, dst) in enumerate(((late0_ref, cast0_ref), (late1_ref, cast1_ref), (late2_ref, cast2_ref))):
        @pl.when((grid_step >= k * CAST_BLOCKS) & (grid_step < (k + 1) * CAST_BLOCKS))
        def _(src=src, dst=dst):
            dst[...] = src[...].astype(BF16)

    @pl.when((i == 0) & (n == 0))
    def _():
        halo_ref[...] = jnp.zeros_like(halo_ref)

    @pl.when(n == 0)
    def _():
        x = x_ref[...]
        ms = jnp.mean(x * x, axis=-1, keepdims=True)
        hn = (x * lax.rsqrt(ms + NORM_EPS) * g_ref[...]).astype(BF16)
        hn_ref[...] = hn
        wg = jnp.concatenate([wg_ref[...], jnp.zeros((GATE_LANES - GATE_COLS, D_MODEL), F32)], axis=0)
        gates = lax.dot_general(hn, wg.astype(BF16), (((1,), (1,)), ((), ())),
                                preferred_element_type=F32)
        gates_t = gates.T
        for c in range(IN_TM // CHUNK):
            gt_ref[c] = gates_t[:2 * MLSTM_HEADS, c * CHUNK:(c + 1) * CHUNK]

    def slab_weights(j, shifted):
        lo = j * SLAB + (GATE_COLS if shifted else 0)
        if lo + SLAB <= IN_TN:
            w = w_ref[lo:lo + SLAB, :]
        else:
            w = jnp.concatenate([w_ref[lo:, :], wtail_ref[...]], axis=0)
        return w.astype(BF16)

    def slab_acc(j, shifted=False):
        return lax.dot_general(hn_ref[...], slab_weights(j, shifted), (((1,), (1,)), ((), ())),
                               preferred_element_type=F32)

    @pl.when(n < QKV_STEPS)
    def _():
        scale = jnp.where(n < D_MODEL // IN_TN, ATTN_HEAD_DIM ** -0.5 * LOG2E, 1.0).astype(F32)
        quarter = IN_TM // 4
        for j in range(IN_TN // SLAB):
            acc = slab_acc(j) * scale
            for hh in range(SLAB // ATTN_HEAD_DIM):
                slot = (2 * j + hh) % 2
                stage_ref[slot] = acc[:, hh * ATTN_HEAD_DIM:(hh + 1) * ATTN_HEAD_DIM]
                for p in range(4):
                    qkv_ref[2 * j + hh, p * quarter:(p + 1) * quarter, :] = (
                        stage_ref[slot, pl.ds(p, quarter, stride=4), :])

    is_conv = (n >= CONV_STEP0) & (n < CONV_STEP0 + CONV_STEPS)

    group = (n - QKV_STEPS) // (D_MODEL // IN_TN)
    is_sigmoid = (group == S_MO) | (group == S_GA) | (group == S_GM)

    @pl.when(group == S_MV)
    def _():
        for j in range(IN_TN // SLAB):
            p_ref[j] = slab_acc(j).astype(BF16)

    @pl.when((n >= QKV_STEPS) & (group == S_AZ))
    def _():
        for j in range(IN_TN // SLAB):
            p_ref[j] = _silu(slab_acc(j)).astype(BF16)

    @pl.when(group == S_MZ)
    def _():
        for j in range(IN_TN // SLAB):
            p_ref[j] = _silu(slab_acc(j, shifted=True)).astype(BF16)

    @pl.when(is_sigmoid)
    def _():
        for j in range(IN_TN // SLAB):
            p_ref[j] = _sigmoid(slab_acc(j, shifted=True)).astype(BF16)

    @pl.when(is_conv)
    def _():
        step = n - CONV_STEP0
        seq_start = (i % tiles_per_seq) == 0
        post = jnp.where(step == 1, MLSTM_QK_DIM ** -0.5, 1.0).astype(F32)

        def conv_slab(j):
            cols = slice(j * SLAB, (j + 1) * SLAB)
            slot = (n + j) % 2
            acc_ref[slot, 0:CONV_HALO, :] = jnp.where(seq_start, 0.0, halo_ref[step, :, cols])
            halo_ref[step, :, cols] = acc_ref[slot, IN_TM:IN_TM + CONV_HALO, :]
            y = cb_ref[:, cols]
            for back in range(CONV_WIDTH):
                tap = acc_ref[slot, CONV_HALO - back:CONV_HALO - back + IN_TM, :]
                y = y + cw_ref[CONV_WIDTH - 1 - back:CONV_WIDTH - back, cols] * tap
            p_ref[j] = (_silu(y) * post).astype(BF16)

        n_slabs = IN_TN // SLAB
        for j in range(n_slabs):
            acc_ref[(n + j) % 2, CONV_HALO:, :] = slab_acc(j)
            if j > 0:
                conv_slab(j - 1)
        conv_slab(n_slabs - 1)


def _in_proj(x2, norm_g, w_t, conv_w, conv_b, late_weights, seq):
    t = x2.shape[0]
    n_cols = w_t.shape[0] - GATE_COLS
    grid = (t // IN_TM, n_cols // IN_TN)
    assert seq % IN_TM == 0 and CONV_STEPS * IN_TN == conv_w.shape[1]
    assert GATE_COL0 == (QKV_STEPS + S_MO * (D_MODEL // IN_TN)) * IN_TN and n_cols % IN_TN == 0
    assert all(w.shape == (D_MODEL, D_MODEL) for w in late_weights)
    assert grid[0] * grid[1] >= len(late_weights) * CAST_BLOCKS

    def cast_spec(k):
        return pl.BlockSpec((CAST_ROWS, D_MODEL),
                            lambda i, n: (jnp.clip(i * grid[1] + n - k * CAST_BLOCKS, 0, CAST_BLOCKS - 1), 0))

    def conv_step(i, n):
        return (0, jnp.clip(n - CONV_STEP0, 0, CONV_STEPS - 1))

    return pl.pallas_call(
        functools.partial(_in_proj_kernel, tiles_per_seq=seq // IN_TM),
        grid=grid,
        in_specs=[
            pl.BlockSpec((IN_TM, D_MODEL), lambda i, n: (i, 0)),
            pl.BlockSpec((1, D_MODEL), lambda i, n: (0, 0)),
            pl.BlockSpec((IN_TN, D_MODEL), lambda i, n: (n, 0)),
            pl.BlockSpec((GATE_COLS, D_MODEL), lambda i, n: ((n + 1) * (IN_TN // GATE_COLS), 0)),
            pl.BlockSpec((GATE_COLS, D_MODEL), lambda i, n: (GATE_COL0 // GATE_COLS, 0)),
            pl.BlockSpec((CONV_WIDTH, IN_TN), conv_step),
            pl.BlockSpec((1, IN_TN), conv_step),
        ] + [cast_spec(k) for k in range(len(late_weights))],
        out_specs=[
            pl.BlockSpec((IN_TN // ATTN_HEAD_DIM, IN_TM, ATTN_HEAD_DIM),
                         lambda i, n: (jnp.minimum(n, QKV_STEPS - 1), i, 0)),
            pl.BlockSpec((IN_TN // SLAB, IN_TM, SLAB),
                         lambda i, n: (jnp.maximum(n - QKV_STEPS, 0), i, 0)),
            pl.BlockSpec((IN_TM // CHUNK, 2 * MLSTM_HEADS, CHUNK), lambda i, n: (i, 0, 0)),
        ] + [cast_spec(k) for k in range(len(late_weights))],
        out_shape=[
            jax.ShapeDtypeStruct((QKV_COLS // ATTN_HEAD_DIM, t, ATTN_HEAD_DIM), F32),
            jax.ShapeDtypeStruct(((n_cols - QKV_COLS) // SLAB, t, SLAB), BF16),
            jax.ShapeDtypeStruct((t // CHUNK, 2 * MLSTM_HEADS, CHUNK), F32),
        ] + [jax.ShapeDtypeStruct(w.shape, BF16) for w in late_weights],
        scratch_shapes=[pltpu.VMEM((IN_TM, D_MODEL), BF16),
                        pltpu.VMEM((CONV_STEPS, CONV_HALO, IN_TN), F32),
                        pltpu.VMEM((2, CONV_HALO + IN_TM, SLAB), F32),
                        pltpu.VMEM((2, IN_TM, ATTN_HEAD_DIM), F32)],
        compiler_params=_cparams(("arbitrary", "arbitrary")),
        name="in_proj",
    )(x2, norm_g, w_t, w_t, w_t, conv_w, conv_b, *late_weights)


def _band_bias(slope, dilation, first):
    qi = lax.broadcasted_iota(jnp.int32, (BAND, 2 * BAND), 0)
    ki = lax.broadcasted_iota(jnp.int32, (BAND, 2 * BAND), 1)
    dist = BAND + qi - ki
    valid = (dist >= 0) & (dist <= BAND)
    if first:
        valid = valid & (ki >= BAND)
    return jnp.where(valid, -slope * (dist * dilation).astype(F32), MASKED)


def _dense_bias(slope, first):
    qi = lax.broadcasted_iota(jnp.int32, (BAND, 2 * BAND), 0)
    ki = lax.broadcasted_iota(jnp.int32, (BAND, 2 * BAND), 1)
    pq, jq = qi // 32, qi % 32
    half, pk, jk = ki // BAND, (ki % BAND) // 32, ki % 32
    dist = 4 * (jq - jk + 32 * (1 - half)) + pq - pk
    valid = (dist >= 0) & (dist <= BAND)
    if first:
        valid = valid & (half == 1)
    return jnp.where(valid, -slope * dist.astype(F32), MASKED)


def _wide_bias(slope):
    qi = lax.broadcasted_iota(jnp.int32, (2 * BAND, 2 * BAND), 0)
    ki = lax.broadcasted_iota(jnp.int32, (2 * BAND, 2 * BAND), 1)
    dist = qi - ki
    valid = (dist >= 0) & (dist <= BAND)
    return jnp.where(valid, -slope * (dist * 16).astype(F32), MASKED)


def _softmax_block(q, k, v, bias):
    s = lax.dot_general(q, k, (((1,), (1,)), ((), ())), preferred_element_type=F32) + bias
    m = jnp.max(s, axis=-1, keepdims=True)
    p = jnp.exp2(s - m).astype(BF16)
    v_ext = jnp.concatenate([v, jnp.ones_like(v)], axis=1)
    r = jnp.dot(p, v_ext, preferred_element_type=F32)
    hd = ATTN_HEAD_DIM
    return r[:, :hd], r[:, hd:], jnp.broadcast_to(m, (q.shape[0], hd))


def _attn_kernel(q_ref, k_ref, v_ref, o_ref,
                 x4_ref, x16_ref, num_ref, den_ref, mx_ref, bias_ref, bias16_ref, unit_ref, unit16_ref,
                 onat_ref):
    seq = q_ref.shape[0]
    n_u4, n_u16 = seq // 4, seq // 16
    slope = jnp.exp2(jnp.full((1, 1), -8.0 / ATTN_HEADS, F32) * (pl.program_id(1) + 1)) * LOG2E

    @pl.when((pl.program_id(0) == 0) & (pl.program_id(1) == 0))
    def _():
        one = jnp.ones((1, 1), F32)
        unit_ref[0, 0] = _dense_bias(one, False)
        unit_ref[0, 1] = _dense_bias(one, True)
        unit_ref[1, 0] = _band_bias(one, 4, False)
        unit_ref[1, 1] = _band_bias(one, 4, True)
        unit16_ref[...] = _wide_bias(one)

    bias_ref[...] = unit_ref[...] * slope
    bias16_ref[...] = unit16_ref[...] * slope

    quarter = IN_TM // 4
    sixteenth = IN_TM // 16
    for ti, src in enumerate((q_ref, k_ref, v_ref)):
        def regroup(tile, carry, ti=ti, src=src):
            base = pl.multiple_of(tile * IN_TM, IN_TM)
            for p4 in range(4):
                for piece in range(quarter // BAND):
                    rows = src[pl.ds(base + p4 * quarter + piece * BAND, BAND), :]
                    dst = pl.ds(pl.multiple_of(tile * quarter + piece * BAND, BAND), BAND)
                    x4_ref[ti, p4, dst, :] = rows.astype(BF16)
            for p16 in range(16):
                rows = src[pl.ds(base + (p16 % 4) * quarter + p16 // 4, sixteenth, stride=4), :]
                x16_ref[ti, p16, pl.ds(pl.multiple_of(tile * sixteenth, sixteenth), sixteenth), :] = rows.astype(BF16)
            return carry

        lax.fori_loop(0, seq // IN_TM, regroup, 0)

    def store(pattern, p, rows, result):
        num, den, mx = result
        num_ref[pattern, p, rows, :] = num
        den_ref[pattern, p, rows, :] = den
        mx_ref[pattern, p, rows, :] = mx

    def dense_block(n):
        cur, prev = slice(n * 32, (n + 1) * 32), slice(max(n - 1, 0) * 32, max(n - 1, 0) * 32 + 32)

        def rows(ti, sl):
            return [x4_ref[ti, p, sl, :] for p in range(4)]

        q = jnp.concatenate(rows(0, cur), axis=0)
        k = jnp.concatenate(rows(1, prev) + rows(1, cur), axis=0)
        v = jnp.concatenate(rows(2, prev) + rows(2, cur), axis=0)
        num, den, mx = _softmax_block(q, k, v, bias_ref[0, int(n == 0)])
        for p in range(4):
            part = slice(p * 32, (p + 1) * 32)
            store(0, p, cur, (num[part], den[part], mx[part]))

    def band_block(p, n):
        cur = slice(n * BAND, (n + 1) * BAND)
        prev = slice(max(n - 1, 0) * BAND, max(n - 1, 0) * BAND + BAND)
        q = x4_ref[0, p, cur, :]
        k = jnp.concatenate([x4_ref[1, p, prev, :], x4_ref[1, p, cur, :]], axis=0)
        v = jnp.concatenate([x4_ref[2, p, prev, :], x4_ref[2, p, cur, :]], axis=0)
        store(1, p, cur, _softmax_block(q, k, v, bias_ref[1, int(n == 0)]))

    wide_rows = n_u16 // ATTN_PARTS

    def wide_block(p16, part):
        rows = slice(part * wide_rows, (part + 1) * wide_rows)
        result = _softmax_block(x16_ref[0, p16, rows, :], x16_ref[1, p16], x16_ref[2, p16], bias16_ref[rows, :])
        store(2, p16 % 4, pl.ds(p16 // 4 + 4 * wide_rows * part, wide_rows, stride=4), result)

    def merge_block(p, n):
        rows = slice(n * BAND, (n + 1) * BAND)
        m0, m1, m2 = mx_ref[0, p, rows, :], mx_ref[1, p, rows, :], mx_ref[2, p, rows, :]
        m_all = jnp.maximum(jnp.maximum(m0, m1), m2)
        w0, w1, w2 = jnp.exp2(m0 - m_all), jnp.exp2(m1 - m_all), jnp.exp2(m2 - m_all)
        num = w0 * num_ref[0, p, rows, :] + w1 * num_ref[1, p, rows, :] + w2 * num_ref[2, p, rows, :]
        den = w0 * den_ref[0, p, rows, :] + w1 * den_ref[1, p, rows, :] + w2 * den_ref[2, p, rows, :]
        onat_ref[pl.ds(p + 4 * BAND * n, BAND, stride=4), :] = num / den

    dense_per_part = n_u4 // 32 // ATTN_PARTS
    band_per_part = n_u4 // BAND // ATTN_PARTS
    for part in range(ATTN_PARTS):
        for n in range(part * dense_per_part, (part + 1) * dense_per_part):
            dense_block(n)
        for n in range(part * band_per_part, (part + 1) * band_per_part):
            for p in range(4):
                band_block(p, n)
                if part > 0:
                    merge_block(p, n - band_per_part)
        for p16 in range(16):
            wide_block(p16, part)
    for n in range((ATTN_PARTS - 1) * band_per_part, ATTN_PARTS * band_per_part):
        for p in range(4):
            merge_block(p, n)
    o_ref[...] = onat_ref[...].astype(o_ref.dtype)


def _attention(qkv, batch, seq):
    assert seq // 16 == 2 * BAND
    hd = ATTN_HEAD_DIM
    qkv4 = qkv.reshape(3 * ATTN_HEADS, batch, seq, hd)

    def spec(which):
        return pl.BlockSpec((None, None, seq, hd), lambda b, h: (which * ATTN_HEADS + h, b, 0, 0))

    out = pl.pallas_call(
        _attn_kernel,
        grid=(batch, ATTN_HEADS),
        in_specs=[spec(0), spec(1), spec(2)],
        out_specs=pl.BlockSpec((None, None, seq, hd), lambda b, h: (h, b, 0, 0)),
        out_shape=jax.ShapeDtypeStruct((ATTN_HEADS, batch, seq, hd), BF16),
        scratch_shapes=[
            pltpu.VMEM((3, 4, seq // 4, hd), BF16),
            pltpu.VMEM((3, 16, seq // 16, hd), BF16),
            pltpu.VMEM((3, 4, seq // 4, hd), F32),
            pltpu.VMEM((3, 4, seq // 4, hd), F32),
            pltpu.VMEM((3, 4, seq // 4, hd), F32),
            pltpu.VMEM((2, 2, BAND, 2 * BAND), F32),
            pltpu.VMEM((2 * BAND, 2 * BAND), F32),
            pltpu.VMEM((2, 2, BAND, 2 * BAND), F32),
            pltpu.VMEM((2 * BAND, 2 * BAND), F32),
            pltpu.VMEM((seq, hd), F32),
        ],
        compiler_params=_cparams(("arbitrary", "arbitrary")),
        name="dilated_attention",
    )(qkv4, qkv4, qkv4)
    return out.reshape(ATTN_HEADS, batch * seq, hd)


MLSTM_TS = 512
ONES_LANES = 128


def _mlstm_kernel(bias_ref, q_ref, k_ref, v_ref, gt_ref, out_ref, c_ref, m_ref, kt_ref,
                  lhs_ref, kw_ref, floor_ref):
    s = pl.program_id(1)
    dk, dv = MLSTM_QK_DIM, MLSTM_V_DIM
    L = CHUNK

    @pl.when(s == 0)
    def _():
        c_ref[...] = jnp.zeros_like(c_ref)
        m_ref[...] = jnp.zeros_like(m_ref)

    ti = lax.broadcasted_iota(jnp.int32, (CHUNK, CHUNK), 0)
    si = lax.broadcasted_iota(jnp.int32, (CHUNK, CHUNK), 1)
    causal = ti >= si
    csum = jnp.where(ti <= si, 1.0, 0.0).astype(F32)

    lane = lax.broadcasted_iota(jnp.int32, (MLSTM_HEADS, L), 1)

    for h in range(MLSTM_HEADS):
        for c in range(MLSTM_TS // L):
            k_hc = k_ref[h // 2, c * L:(c + 1) * L, (h % 2) * dk:(h % 2 + 1) * dk]
            kt_ref[h, c] = k_hc.astype(F32).T.astype(BF16)

    n_chunks = MLSTM_TS // L
    heads = range(MLSTM_HEADS)

    def q_of(h, c):
        return q_ref[h // 2, c * L:(c + 1) * L, (h % 2) * dk:(h % 2 + 1) * dk]

    def v_ext_of(h, c):
        return jnp.concatenate([v_ref[h, c * L:(c + 1) * L, :], jnp.ones((L, ONES_LANES), BF16)], axis=1)

    w_olds = []
    for c in range(n_chunks):
        gates = gt_ref[c] + bias_ref[...]
        ig_all = gates[:MLSTM_HEADS]
        lf_all = _log_sigmoid(gates[MLSTM_HEADS:])
        b_all = jnp.dot(lf_all, csum, preferred_element_type=F32,
                        precision=lax.Precision.HIGHEST)
        c_all = ig_all - b_all
        cm_all = c_all
        shift = 1
        while shift < L:
            cm_all = jnp.maximum(cm_all, jnp.where(lane >= shift, pltpu.roll(cm_all, shift, axis=1), MASKED))
            shift *= 2
        m_prev = m_ref[...]
        b_last = b_all[:, L - 1:L]
        top = jnp.maximum(m_prev, cm_all)
        m_new = b_last + top[:, L - 1:L]
        w_old = jnp.exp(b_last + m_prev - m_new)
        w_key = jnp.exp(b_last + c_all - m_new)
        cols = jnp.concatenate([
            -top,
            jnp.exp(m_prev - top),
            jnp.exp(-(b_all + top)),
        ], axis=0).T
        m_ref[...] = m_new
        w_olds.append(w_old)

        def column(group, h):
            j = group * MLSTM_HEADS + h
            return jnp.broadcast_to(cols[:, j:j + 1], (L, L))

        scores = [jnp.dot(q_of(h, c), kt_ref[h, c], preferred_element_type=F32) for h in heads]
        for h in heads:
            w_intra = jnp.exp(jnp.where(causal, column(0, h) + c_all[h:h + 1], MASKED))
            lhs_ref[c, h] = jnp.concatenate([(scores[h] * w_intra).astype(BF16),
                                             (q_of(h, c).astype(F32) * column(1, h)).astype(BF16)], axis=1)
            kw_ref[c, h] = (kt_ref[h, c].astype(F32) * w_key[h:h + 1]).astype(BF16)
            floor_ref[c, h] = column(2, h)

    for c in range(n_chunks):
        incs = [jnp.dot(kw_ref[c, h], v_ext_of(h, c), preferred_element_type=F32) for h in heads]
        nds = [jnp.dot(lhs_ref[c, h], jnp.concatenate([v_ext_of(h, c), c_ref[h].astype(BF16)], axis=0),
                       preferred_element_type=F32) for h in heads]
        for h in heads:
            nd = nds[h]
            inv = 1.0 / jnp.maximum(jnp.abs(nd[:, dv:]), floor_ref[c, h])
            for half in range(dv // L):
                lanes = slice(half * L, (half + 1) * L)
                out_ref[h, c * L:(c + 1) * L, lanes] = (nd[:, lanes] * inv).astype(out_ref.dtype)
            c_ref[h] = w_olds[c][h:h + 1] * c_ref[h] + incs[h]


def _mlstm(slabs, gates_t, b_if, batch, seq):
    n_slabs = slabs.shape[0]
    s4d = slabs.reshape(n_slabs, batch, seq, SLAB)
    half = MLSTM_HEADS // 2
    tiles = seq // MLSTM_TS
    tile_chunks = MLSTM_TS // CHUNK

    def tile_spec(first_slab, n):
        return pl.BlockSpec((n, None, MLSTM_TS, SLAB), lambda b, s: (first_slab // n, b, s, 0))

    out = pl.pallas_call(
        _mlstm_kernel,
        grid=(batch, tiles),
        in_specs=[
            pl.BlockSpec((2 * MLSTM_HEADS, 1), lambda b, s: (0, 0)),
            tile_spec(S_MQK * GROUP_SLABS, half), tile_spec(S_MQK * GROUP_SLABS + half, half),
            tile_spec(S_MV * GROUP_SLABS, MLSTM_HEADS),
            pl.BlockSpec((tile_chunks, 2 * MLSTM_HEADS, CHUNK), lambda b, s: (b * tiles + s, 0, 0)),
        ],
        out_specs=pl.BlockSpec((MLSTM_HEADS, None, MLSTM_TS, SLAB), lambda b, s: (0, b, s, 0)),
        out_shape=jax.ShapeDtypeStruct((MLSTM_HEADS, batch, seq, SLAB), BF16),
        scratch_shapes=[
            pltpu.VMEM((MLSTM_HEADS, MLSTM_QK_DIM, MLSTM_V_DIM + ONES_LANES), F32),
            pltpu.VMEM((MLSTM_HEADS, 1), F32),
            pltpu.VMEM((MLSTM_HEADS, tile_chunks, MLSTM_QK_DIM, CHUNK), BF16),
            pltpu.VMEM((tile_chunks, MLSTM_HEADS, CHUNK, CHUNK + MLSTM_QK_DIM), BF16),
            pltpu.VMEM((tile_chunks, MLSTM_HEADS, MLSTM_QK_DIM, CHUNK), BF16),
            pltpu.VMEM((tile_chunks, MLSTM_HEADS, CHUNK, CHUNK), F32),
        ],
        compiler_params=_cparams(("parallel", "arbitrary")),
        name="mlstm",
    )(b_if.reshape(-1, 1), s4d, s4d, s4d, gates_t)
    return out.reshape(MLSTM_HEADS, batch * seq, SLAB)


MERGE_TM = 512
OUT_TM = 512


def _weight_spec():
    return pl.BlockSpec((D_MODEL, D_MODEL), lambda i: (0, 0), pipeline_mode=pl.Buffered(1))


def _branch_merge_kernel(attn_ref, az_ref, cell_ref, mo_ref, mz_ref, ga_ref, gm_ref,
                         wa_ref, wm_ref, ng_ref, merged_ref):
    n = GROUP_SLABS
    gated = []
    for c in range(n):
        attn = jnp.concatenate([attn_ref[2 * c].astype(F32), attn_ref[2 * c + 1].astype(F32)], axis=1)
        gated.append(attn * az_ref[c].astype(F32))
    gated = jnp.concatenate(gated, axis=1).astype(BF16)
    y_a, mem = [], []
    for h in range(MLSTM_HEADS):
        cols = slice(h * SLAB, (h + 1) * SLAB)
        y_a.append(jnp.dot(gated, wa_ref[:, cols], preferred_element_type=F32))
        cell = mo_ref[h].astype(F32) * cell_ref[h].astype(F32)
        cell = cell * lax.rsqrt(jnp.mean(cell * cell, axis=-1, keepdims=True) + NORM_EPS)
        cell = cell * ng_ref[:, cols]
        mem.append(cell * mz_ref[h].astype(F32))
    mem = jnp.concatenate(mem, axis=1).astype(BF16)
    for c in range(n):
        cols = slice(c * SLAB, (c + 1) * SLAB)
        y_m = jnp.dot(mem, wm_ref[:, cols], preferred_element_type=F32)
        merged = ga_ref[c].astype(F32) * y_a[c] + gm_ref[c].astype(F32) * y_m
        merged_ref[:, cols] = merged.astype(merged_ref.dtype)


def _branch_merge(attn, slabs, cell, w_a, w_m, norm_g):
    t = attn.shape[1]
    assert MLSTM_V_DIM == SLAB and MLSTM_HEADS == GROUP_SLABS

    def slab_spec(group):
        return pl.BlockSpec((GROUP_SLABS, MERGE_TM, SLAB), lambda i: (group, i, 0))

    return pl.pallas_call(
        _branch_merge_kernel,
        grid=(t // MERGE_TM,),
        in_specs=[
            pl.BlockSpec((ATTN_HEADS, MERGE_TM, ATTN_HEAD_DIM), lambda i: (0, i, 0)),
            slab_spec(S_AZ), slab_spec(0), slab_spec(S_MO), slab_spec(S_MZ),
            slab_spec(S_GA), slab_spec(S_GM),
            _weight_spec(), _weight_spec(),
            pl.BlockSpec((1, D_MODEL), lambda i: (0, 0)),
        ],
        out_specs=pl.BlockSpec((MERGE_TM, D_MODEL), lambda i: (i, 0)),
        out_shape=jax.ShapeDtypeStruct((t, D_MODEL), BF16),
        compiler_params=_cparams(("parallel",)),
        name="branch_merge",
    )(attn, slabs, cell, slabs, slabs, slabs, slabs, w_a, w_m, norm_g)


def _out_proj_kernel(merged_ref, x_ref, wo_ref, fg_ref, out_ref):
    y = x_ref[...] + jnp.dot(merged_ref[...], wo_ref[...], preferred_element_type=F32)
    y = y * lax.rsqrt(jnp.mean(y * y, axis=-1, keepdims=True) + NORM_EPS)
    out_ref[...] = y * fg_ref[...]


def _out_proj(merged, x2, w_o, final_g):
    t = x2.shape[0]
    return pl.pallas_call(
        _out_proj_kernel,
        grid=(t // OUT_TM,),
        in_specs=[
            pl.BlockSpec((OUT_TM, D_MODEL), lambda i: (i, 0)),
            pl.BlockSpec((OUT_TM, D_MODEL), lambda i: (i, 0)),
            _weight_spec(),
            pl.BlockSpec((1, D_MODEL), lambda i: (0, 0)),
        ],
        out_specs=pl.BlockSpec((OUT_TM, D_MODEL), lambda i: (i, 0)),
        out_shape=jax.ShapeDtypeStruct((t, D_MODEL), F32),
        compiler_params=_cparams(("parallel",)),
        name="out_proj",
    )(merged, x2, w_o, final_g)


def kernel(x, norm_g, w_in, b_if, conv_w, conv_b, mlstm_norm_g, w_attn_branch, w_mlstm_branch,
           w_out, final_norm_g):
    batch, seq, d = x.shape
    assert d == D_MODEL and seq % (16 * BAND) == 0 and (batch * seq) % IN_TM == 0
    t = batch * seq
    x2 = x.reshape(t, d)

    assert w_in.shape[1] == QKV_COLS + N_SLAB_GROUPS * D_MODEL + GATE_COLS
    qkv, slabs, gates_t, wa_bf, wm_bf, wo_bf = _in_proj(
        x2, norm_g.reshape(1, d), w_in.T, conv_w, conv_b.reshape(1, -1),
        (w_attn_branch, w_mlstm_branch, w_out), seq)
    attn = _attention(qkv, batch, seq)
    cell = _mlstm(slabs, gates_t, b_if, batch, seq)
    merged = _branch_merge(attn, slabs, cell, wa_bf, wm_bf, mlstm_norm_g.reshape(1, d))
    out = _out_proj(merged, x2, wo_bf, final_norm_g.reshape(1, d))
    return out.reshape(batch, seq, d)
```

```python
import functools

import jax
import jax.numpy as jnp
from jax import lax
from jax.experimental import pallas as pl
from jax.experimental.pallas import tpu as pltpu

F32 = jnp.float32
BF16 = jnp.bfloat16

D_MODEL = 2048
ATTN_HEADS = 16
ATTN_HEAD_DIM = 128
MLSTM_HEADS = 8
MLSTM_QK_DIM = 128
MLSTM_V_DIM = 256
CONV_WIDTH = 4
NORM_EPS = 1e-6
BAND = 128
ATTN_PARTS = 2
SLAB = 256
GROUP_SLABS = D_MODEL // SLAB
QKV_COLS = 3 * D_MODEL
S_AZ, S_MQK, S_MV, S_MO, S_MZ, S_GA, S_GM = range(7)
N_SLAB_GROUPS = 7
GATE_LANES = 128
MASKED = -1e30
CHUNK = 128
VMEM_LIMIT = 56 * 1024 * 1024


def _cparams(sem):
    return pltpu.CompilerParams(dimension_semantics=sem, vmem_limit_bytes=VMEM_LIMIT)


LOG2E = 1.4426950408889634


def _sigmoid(x):
    return 0.5 + 0.5 * jnp.tanh(0.5 * x)


def _silu(x):
    half = 0.5 * x
    return half + half * jnp.tanh(half)


def _log_sigmoid(x):
    return jnp.minimum(x, 0.0) - jnp.log(1.0 + jnp.exp(-jnp.abs(x)))


IN_TM = 1024
IN_TN = 1024
QKV_STEPS = QKV_COLS // IN_TN

GATE_COLS = 2 * MLSTM_HEADS
GATE_COL0 = 4 * D_MODEL + 2 * MLSTM_HEADS * MLSTM_QK_DIM + MLSTM_HEADS * MLSTM_V_DIM
GATE_STEP = GATE_COL0 // IN_TN
W_RING = 3
W_AHEAD = W_RING - 1


CONV_STEP0 = QKV_STEPS + S_MQK * (D_MODEL // IN_TN)
CONV_STEPS = D_MODEL // IN_TN
CONV_HALO = 8


def _in_proj_kernel(x_hbm, g_ref, w_hbm, wg_ref, cw_ref, cb_ref, qkv_ref, p_ref, gt_ref,
                    hn_ref, halo_ref, acc_ref, stage_ref, wbuf_ref, wsem, xbuf_ref, xsem, *, tiles_per_seq):
    i = pl.program_id(0)
    n = pl.program_id(1)
    n_tiles = pl.num_programs(0)
    n_steps = pl.num_programs(1)
    flat = i * n_steps + n
    slot = flat % W_RING

    def w_copy(step):
        col_step = step % n_steps
        row0 = col_step * IN_TN + jnp.where(col_step >= GATE_STEP, GATE_COLS, 0)
        return pltpu.make_async_copy(w_hbm.at[pl.ds(pl.multiple_of(row0, GATE_COLS), IN_TN), :],
                                     wbuf_ref.at[step % W_RING], wsem.at[step % W_RING])

    def x_copy(tile):
        return pltpu.make_async_copy(x_hbm.at[pl.ds(pl.multiple_of(tile * IN_TM, IN_TM), IN_TM), :],
                                     xbuf_ref, xsem.at[0])

    @pl.when(flat == 0)
    def _():
        x_copy(0).start()
        for s in range(W_AHEAD):
            w_copy(s).start()
        halo_ref[...] = jnp.zeros_like(halo_ref)

    @pl.when(flat + W_AHEAD < n_tiles * n_steps)
    def _():
        w_copy(flat + W_AHEAD).start()

    @pl.when((n == 1) & (i + 1 < n_tiles))
    def _():
        x_copy(i + 1).start()

    w_copy(flat).wait()

    @pl.when(n == 0)
    def _():
        x_copy(i).wait()
        x = xbuf_ref[...]
        ms = jnp.mean(x * x, axis=-1, keepdims=True)
        hn = (x * lax.rsqrt(ms + NORM_EPS) * g_ref[...]).astype(BF16)
        hn_ref[...] = hn
        wg = jnp.concatenate([wg_ref[...], jnp.zeros((GATE_LANES - GATE_COLS, D_MODEL), F32)], axis=0)
        gates = lax.dot_general(hn, wg.astype(BF16), (((1,), (1,)), ((), ())),
                                preferred_element_type=F32)
        gates_t = gates.T
        for c in range(IN_TM // CHUNK):
            gt_ref[c] = gates_t[:2 * MLSTM_HEADS, c * CHUNK:(c + 1) * CHUNK]

    def slab_acc(j):
        w = wbuf_ref[slot, j * SLAB:(j + 1) * SLAB, :].astype(BF16)
        return lax.dot_general(hn_ref[...], w, (((1,), (1,)), ((), ())), preferred_element_type=F32)

    @pl.when(n < QKV_STEPS)
    def _():
        scale = jnp.where(n < D_MODEL // IN_TN, ATTN_HEAD_DIM ** -0.5 * LOG2E, 1.0).astype(F32)
        quarter = IN_TM // 4
        for j in range(IN_TN // SLAB):
            acc = slab_acc(j) * scale
            for hh in range(SLAB // ATTN_HEAD_DIM):
                slot = (2 * j + hh) % 2
                stage_ref[slot] = acc[:, hh * ATTN_HEAD_DIM:(hh + 1) * ATTN_HEAD_DIM]
                for p in range(4):
                    qkv_ref[2 * j + hh, p * quarter:(p + 1) * quarter, :] = (
                        stage_ref[slot, pl.ds(p, quarter, stride=4), :])

    is_conv = (n >= CONV_STEP0) & (n < CONV_STEP0 + CONV_STEPS)

    group = (n - QKV_STEPS) // (D_MODEL // IN_TN)
    is_sigmoid = (group == S_MO) | (group == S_GA) | (group == S_GM)

    @pl.when(group == S_MV)
    def _():
        for j in range(IN_TN // SLAB):
            p_ref[j] = slab_acc(j).astype(BF16)

    @pl.when((n >= QKV_STEPS) & ((group == S_AZ) | (group == S_MZ)))
    def _():
        for j in range(IN_TN // SLAB):
            p_ref[j] = _silu(slab_acc(j)).astype(BF16)

    @pl.when(is_sigmoid)
    def _():
        for j in range(IN_TN // SLAB):
            p_ref[j] = _sigmoid(slab_acc(j)).astype(BF16)

    @pl.when(is_conv)
    def _():
        step = n - CONV_STEP0
        seq_start = (i % tiles_per_seq) == 0
        post = jnp.where(step == 1, MLSTM_QK_DIM ** -0.5, 1.0).astype(F32)

        def conv_slab(j):
            cols = slice(j * SLAB, (j + 1) * SLAB)
            slot = (n + j) % 2
            acc_ref[slot, 0:CONV_HALO, :] = jnp.where(seq_start, 0.0, halo_ref[step, :, cols])
            halo_ref[step, :, cols] = acc_ref[slot, IN_TM:IN_TM + CONV_HALO, :]
            y = cb_ref[:, cols]
            for back in range(CONV_WIDTH):
                tap = acc_ref[slot, CONV_HALO - back:CONV_HALO - back + IN_TM, :]
                y = y + cw_ref[CONV_WIDTH - 1 - back:CONV_WIDTH - back, cols] * tap
            p_ref[j] = (_silu(y) * post).astype(BF16)

        n_slabs = IN_TN // SLAB
        for j in range(n_slabs):
            acc_ref[(n + j) % 2, CONV_HALO:, :] = slab_acc(j)
            if j > 0:
                conv_slab(j - 1)
        conv_slab(n_slabs - 1)


def _in_proj(x2, norm_g, w_t, conv_w, conv_b, seq):
    t = x2.shape[0]
    n_cols = w_t.shape[0] - GATE_COLS
    grid = (t // IN_TM, n_cols // IN_TN)
    assert seq % IN_TM == 0 and CONV_STEPS * IN_TN == conv_w.shape[1]
    assert GATE_COL0 == (QKV_STEPS + S_MO * (D_MODEL // IN_TN)) * IN_TN and n_cols % IN_TN == 0

    def conv_step(i, n):
        return (0, jnp.clip(n - CONV_STEP0, 0, CONV_STEPS - 1))

    return pl.pallas_call(
        functools.partial(_in_proj_kernel, tiles_per_seq=seq // IN_TM),
        grid=grid,
        in_specs=[
            pl.BlockSpec(memory_space=pl.ANY),
            pl.BlockSpec((1, D_MODEL), lambda i, n: (0, 0)),
            pl.BlockSpec(memory_space=pl.ANY),
            pl.BlockSpec((GATE_COLS, D_MODEL), lambda i, n: (GATE_COL0 // GATE_COLS, 0)),
            pl.BlockSpec((CONV_WIDTH, IN_TN), conv_step),
            pl.BlockSpec((1, IN_TN), conv_step),
        ],
        out_specs=[
            pl.BlockSpec((IN_TN // ATTN_HEAD_DIM, IN_TM, ATTN_HEAD_DIM),
                         lambda i, n: (jnp.minimum(n, QKV_STEPS - 1), i, 0)),
            pl.BlockSpec((IN_TN // SLAB, IN_TM, SLAB),
                         lambda i, n: (jnp.maximum(n - QKV_STEPS, 0), i, 0)),
            pl.BlockSpec((IN_TM // CHUNK, 2 * MLSTM_HEADS, CHUNK), lambda i, n: (i, 0, 0)),
        ],
        out_shape=[
            jax.ShapeDtypeStruct((QKV_COLS // ATTN_HEAD_DIM, t, ATTN_HEAD_DIM), F32),
            jax.ShapeDtypeStruct(((n_cols - QKV_COLS) // SLAB, t, SLAB), BF16),
            jax.ShapeDtypeStruct((t // CHUNK, 2 * MLSTM_HEADS, CHUNK), F32),
        ],
        scratch_shapes=[pltpu.VMEM((IN_TM, D_MODEL), BF16),
                        pltpu.VMEM((CONV_STEPS, CONV_HALO, IN_TN), F32),
                        pltpu.VMEM((2, CONV_HALO + IN_TM, SLAB), F32),
                        pltpu.VMEM((2, IN_TM, ATTN_HEAD_DIM), F32),
                        pltpu.VMEM((W_RING, IN_TN, D_MODEL), F32),
                        pltpu.SemaphoreType.DMA((W_RING,)),
                        pltpu.VMEM((IN_TM, D_MODEL), F32),
                        pltpu.SemaphoreType.DMA((1,))],
        compiler_params=_cparams(("arbitrary", "arbitrary")),
        name="in_proj",
    )(x2, norm_g, w_t, w_t, conv_w, conv_b)


def _band_bias(slope, dilation, first):
    qi = lax.broadcasted_iota(jnp.int32, (BAND, 2 * BAND), 0)
    ki = lax.broadcasted_iota(jnp.int32, (BAND, 2 * BAND), 1)
    dist = BAND + qi - ki
    valid = (dist >= 0) & (dist <= BAND)
    if first:
        valid = valid & (ki >= BAND)
    return jnp.where(valid, -slope * (dist * dilation).astype(F32), MASKED)


def _dense_bias(slope, first):
    qi = lax.broadcasted_iota(jnp.int32, (BAND, 2 * BAND), 0)
    ki = lax.broadcasted_iota(jnp.int32, (BAND, 2 * BAND), 1)
    pq, jq = qi // 32, qi % 32
    half, pk, jk = ki // BAND, (ki % BAND) // 32, ki % 32
    dist = 4 * (jq - jk + 32 * (1 - half)) + pq - pk
    valid = (dist >= 0) & (dist <= BAND)
    if first:
        valid = valid & (half == 1)
    return jnp.where(valid, -slope * dist.astype(F32), MASKED)


def _wide_bias(slope):
    qi = lax.broadcasted_iota(jnp.int32, (2 * BAND, 2 * BAND), 0)
    ki = lax.broadcasted_iota(jnp.int32, (2 * BAND, 2 * BAND), 1)
    dist = qi - ki
    valid = (dist >= 0) & (dist <= BAND)
    return jnp.where(valid, -slope * (dist * 16).astype(F32), MASKED)


def _softmax_block(q, k, v, bias):
    s = lax.dot_general(q, k, (((1,), (1,)), ((), ())), preferred_element_type=F32) + bias
    m = jnp.max(s, axis=-1, keepdims=True)
    p = jnp.exp2(s - m).astype(BF16)
    v_ext = jnp.concatenate([v, jnp.ones_like(v)], axis=1)
    r = jnp.dot(p, v_ext, preferred_element_type=F32)
    hd = ATTN_HEAD_DIM
    return r[:, :hd], r[:, hd:], jnp.broadcast_to(m, (q.shape[0], hd))


def _attn_kernel(q_ref, k_ref, v_ref, o_ref,
                 x4_ref, x16_ref, num_ref, den_ref, mx_ref, bias_ref, bias16_ref, unit_ref, unit16_ref,
                 onat_ref):
    seq = q_ref.shape[0]
    n_u4, n_u16 = seq // 4, seq // 16
    slope = jnp.exp2(jnp.full((1, 1), -8.0 / ATTN_HEADS, F32) * (pl.program_id(1) + 1)) * LOG2E

    @pl.when((pl.program_id(0) == 0) & (pl.program_id(1) == 0))
    def _():
        one = jnp.ones((1, 1), F32)
        unit_ref[0, 0] = _dense_bias(one, False)
        unit_ref[0, 1] = _dense_bias(one, True)
        unit_ref[1, 0] = _band_bias(one, 4, False)
        unit_ref[1, 1] = _band_bias(one, 4, True)
        unit16_ref[...] = _wide_bias(one)

    bias_ref[...] = unit_ref[...] * slope
    bias16_ref[...] = unit16_ref[...] * slope

    quarter = IN_TM // 4
    sixteenth = IN_TM // 16
    for ti, src in enumerate((q_ref, k_ref, v_ref)):
        def regroup(tile, carry, ti=ti, src=src):
            base = pl.multiple_of(tile * IN_TM, IN_TM)
            for p4 in range(4):
                for piece in range(quarter // BAND):
                    rows = src[pl.ds(base + p4 * quarter + piece * BAND, BAND), :]
                    dst = pl.ds(pl.multiple_of(tile * quarter + piece * BAND, BAND), BAND)
                    x4_ref[ti, p4, dst, :] = rows.astype(BF16)
            for p16 in range(16):
                rows = src[pl.ds(base + (p16 % 4) * quarter + p16 // 4, sixteenth, stride=4), :]
                x16_ref[ti, p16, pl.ds(pl.multiple_of(tile * sixteenth, sixteenth), sixteenth), :] = rows.astype(BF16)
            return carry

        lax.fori_loop(0, seq // IN_TM, regroup, 0)

    def store(pattern, p, rows, result):
        num, den, mx = result
        num_ref[pattern, p, rows, :] = num
        den_ref[pattern, p, rows, :] = den
        mx_ref[pattern, p, rows, :] = mx

    def dense_block(n):
        cur, prev = slice(n * 32, (n + 1) * 32), slice(max(n - 1, 0) * 32, max(n - 1, 0) * 32 + 32)

        def rows(ti, sl):
            return [x4_ref[ti, p, sl, :] for p in range(4)]

        q = jnp.concatenate(rows(0, cur), axis=0)
        k = jnp.concatenate(rows(1, prev) + rows(1, cur), axis=0)
        v = jnp.concatenate(rows(2, prev) + rows(2, cur), axis=0)
        num, den, mx = _softmax_block(q, k, v, bias_ref[0, int(n == 0)])
        for p in range(4):
            part = slice(p * 32, (p + 1) * 32)
            store(0, p, cur, (num[part], den[part], mx[part]))

    def band_block(p, n):
        cur = slice(n * BAND, (n + 1) * BAND)
        prev = slice(max(n - 1, 0) * BAND, max(n - 1, 0) * BAND + BAND)
        q = x4_ref[0, p, cur, :]
        k = jnp.concatenate([x4_ref[1, p, prev, :], x4_ref[1, p, cur, :]], axis=0)
        v = jnp.concatenate([x4_ref[2, p, prev, :], x4_ref[2, p, cur, :]], axis=0)
        store(1, p, cur, _softmax_block(q, k, v, bias_ref[1, int(n == 0)]))

    wide_rows = n_u16 // ATTN_PARTS

    def wide_block(p16, part):
        rows = slice(part * wide_rows, (part + 1) * wide_rows)
        result = _softmax_block(x16_ref[0, p16, rows, :], x16_ref[1, p16], x16_ref[2, p16], bias16_ref[rows, :])
        store(2, p16 % 4, pl.ds(p16 // 4 + 4 * wide_rows * part, wide_rows, stride=4), result)

    def merge_block(p, n):
        rows = slice(n * BAND, (n + 1) * BAND)
        m0, m1, m2 = mx_ref[0, p, rows, :], mx_ref[1, p, rows, :], mx_ref[2, p, rows, :]
        m_all = jnp.maximum(jnp.maximum(m0, m1), m2)
        w0, w1, w2 = jnp.exp2(m0 - m_all), jnp.exp2(m1 - m_all), jnp.exp2(m2 - m_all)
        num = w0 * num_ref[0, p, rows, :] + w1 * num_ref[1, p, rows, :] + w2 * num_ref[2, p, rows, :]
        den = w0 * den_ref[0, p, rows, :] + w1 * den_ref[1, p, rows, :] + w2 * den_ref[2, p, rows, :]
        onat_ref[pl.ds(p + 4 * BAND * n, BAND, stride=4), :] = num / den

    dense_per_part = n_u4 // 32 // ATTN_PARTS
    band_per_part = n_u4 // BAND // ATTN_PARTS
    for part in range(ATTN_PARTS):
        for n in range(part * dense_per_part, (part + 1) * dense_per_part):
            dense_block(n)
        for n in range(part * band_per_part, (part + 1) * band_per_part):
            for p in range(4):
                band_block(p, n)
                if part > 0:
                    merge_block(p, n - band_per_part)
        for p16 in range(16):
            wide_block(p16, part)
    for n in range((ATTN_PARTS - 1) * band_per_part, ATTN_PARTS * band_per_part):
        for p in range(4):
            merge_block(p, n)
    o_ref[...] = onat_ref[...].astype(o_ref.dtype)


def _attention(qkv, batch, seq):
    assert seq // 16 == 2 * BAND
    hd = ATTN_HEAD_DIM
    qkv4 = qkv.reshape(3 * ATTN_HEADS, batch, seq, hd)

    def spec(which):
        return pl.BlockSpec((None, None, seq, hd), lambda b, h: (which * ATTN_HEADS + h, b, 0, 0))

    out = pl.pallas_call(
        _attn_kernel,
        grid=(batch, ATTN_HEADS),
        in_specs=[spec(0), spec(1), spec(2)],
        out_specs=pl.BlockSpec((None, None, seq, hd), lambda b, h: (h, b, 0, 0)),
        out_shape=jax.ShapeDtypeStruct((ATTN_HEADS, batch, seq, hd), BF16),
        scratch_shapes=[
            pltpu.VMEM((3, 4, seq // 4, hd), BF16),
            pltpu.VMEM((3, 16, seq // 16, hd), BF16),
            pltpu.VMEM((3, 4, seq // 4, hd), F32),
            pltpu.VMEM((3, 4, seq // 4, hd), F32),
            pltpu.VMEM((3, 4, seq // 4, hd), F32),
            pltpu.VMEM((2, 2, BAND, 2 * BAND), F32),
            pltpu.VMEM((2 * BAND, 2 * BAND), F32),
            pltpu.VMEM((2, 2, BAND, 2 * BAND), F32),
            pltpu.VMEM((2 * BAND, 2 * BAND), F32),
            pltpu.VMEM((seq, hd), F32),
        ],
        compiler_params=_cparams(("arbitrary", "arbitrary")),
        name="dilated_attention",
    )(qkv4, qkv4, qkv4)
    return out.reshape(ATTN_HEADS, batch * seq, hd)


MLSTM_TS = 512
ONES_LANES = 128


def _mlstm_kernel(bias_ref, q_ref, k_ref, v_ref, gt_ref, out_ref, c_ref, m_ref, kt_ref,
                  lhs_ref, kw_ref, floor_ref):
    s = pl.program_id(1)
    dk, dv = MLSTM_QK_DIM, MLSTM_V_DIM
    L = CHUNK

    @pl.when(s == 0)
    def _():
        c_ref[...] = jnp.zeros_like(c_ref)
        m_ref[...] = jnp.zeros_like(m_ref)

    ti = lax.broadcasted_iota(jnp.int32, (CHUNK, CHUNK), 0)
    si = lax.broadcasted_iota(jnp.int32, (CHUNK, CHUNK), 1)
    causal = ti >= si
    csum = jnp.where(ti <= si, 1.0, 0.0).astype(F32)

    lane = lax.broadcasted_iota(jnp.int32, (MLSTM_HEADS, L), 1)

    for h in range(MLSTM_HEADS):
        for c in range(MLSTM_TS // L):
            k_hc = k_ref[h // 2, c * L:(c + 1) * L, (h % 2) * dk:(h % 2 + 1) * dk]
            kt_ref[h, c] = k_hc.astype(F32).T.astype(BF16)

    n_chunks = MLSTM_TS // L
    heads = range(MLSTM_HEADS)

    def q_of(h, c):
        return q_ref[h // 2, c * L:(c + 1) * L, (h % 2) * dk:(h % 2 + 1) * dk]

    def v_ext_of(h, c):
        return jnp.concatenate([v_ref[h, c * L:(c + 1) * L, :], jnp.ones((L, ONES_LANES), BF16)], axis=1)

    w_olds = []
    for c in range(n_chunks):
        gates = gt_ref[c] + bias_ref[...]
        ig_all = gates[:MLSTM_HEADS]
        lf_all = _log_sigmoid(gates[MLSTM_HEADS:])
        b_all = jnp.dot(lf_all, csum, preferred_element_type=F32,
                        precision=lax.Precision.HIGHEST)
        c_all = ig_all - b_all
        cm_all = c_all
        shift = 1
        while shift < L:
            cm_all = jnp.maximum(cm_all, jnp.where(lane >= shift, pltpu.roll(cm_all, shift, axis=1), MASKED))
            shift *= 2
        m_prev = m_ref[...]
        b_last = b_all[:, L - 1:L]
        top = jnp.maximum(m_prev, cm_all)
        m_new = b_last + top[:, L - 1:L]
        w_old = jnp.exp(b_last + m_prev - m_new)
        w_key = jnp.exp(b_last + c_all - m_new)
        cols = jnp.concatenate([
            -top,
            jnp.exp(m_prev - top),
            jnp.exp(-(b_all + top)),
        ], axis=0).T
        m_ref[...] = m_new
        w_olds.append(w_old)

        def column(group, h):
            j = group * MLSTM_HEADS + h
            return jnp.broadcast_to(cols[:, j:j + 1], (L, L))

        scores = [jnp.dot(q_of(h, c), kt_ref[h, c], preferred_element_type=F32) for h in heads]
        for h in heads:
            w_intra = jnp.exp(jnp.where(causal, column(0, h) + c_all[h:h + 1], MASKED))
            lhs_ref[c, h] = jnp.concatenate([(scores[h] * w_intra).astype(BF16),
                                             (q_of(h, c).astype(F32) * column(1, h)).astype(BF16)], axis=1)
            kw_ref[c, h] = (kt_ref[h, c].astype(F32) * w_key[h:h + 1]).astype(BF16)
            floor_ref[c, h] = column(2, h)

    for c in range(n_chunks):
        incs = [jnp.dot(kw_ref[c, h], v_ext_of(h, c), preferred_element_type=F32) for h in heads]
        nds = [jnp.dot(lhs_ref[c, h], jnp.concatenate([v_ext_of(h, c), c_ref[h].astype(BF16)], axis=0),
                       preferred_element_type=F32) for h in heads]
        for h in heads:
            nd = nds[h]
            inv = 1.0 / jnp.maximum(jnp.abs(nd[:, dv:]), floor_ref[c, h])
            for half in range(dv // L):
                lanes = slice(half * L, (half + 1) * L)
                out_ref[h, c * L:(c + 1) * L, lanes] = (nd[:, lanes] * inv).astype(out_ref.dtype)
            c_ref[h] = w_olds[c][h:h + 1] * c_ref[h] + incs[h]


def _mlstm(slabs, gates_t, b_if, batch, seq):
    n_slabs = slabs.shape[0]
    s4d = slabs.reshape(n_slabs, batch, seq, SLAB)
    half = MLSTM_HEADS // 2
    tiles = seq // MLSTM_TS
    tile_chunks = MLSTM_TS // CHUNK

    def tile_spec(first_slab, n):
        return pl.BlockSpec((n, None, MLSTM_TS, SLAB), lambda b, s: (first_slab // n, b, s, 0))

    out = pl.pallas_call(
        _mlstm_kernel,
        grid=(batch, tiles),
        in_specs=[
            pl.BlockSpec((2 * MLSTM_HEADS, 1), lambda b, s: (0, 0)),
            tile_spec(S_MQK * GROUP_SLABS, half), tile_spec(S_MQK * GROUP_SLABS + half, half),
            tile_spec(S_MV * GROUP_SLABS, MLSTM_HEADS),
            pl.BlockSpec((tile_chunks, 2 * MLSTM_HEADS, CHUNK), lambda b, s: (b * tiles + s, 0, 0)),
        ],
        out_specs=pl.BlockSpec((MLSTM_HEADS, None, MLSTM_TS, SLAB), lambda b, s: (0, b, s, 0)),
        out_shape=jax.ShapeDtypeStruct((MLSTM_HEADS, batch, seq, SLAB), BF16),
        scratch_shapes=[
            pltpu.VMEM((MLSTM_HEADS, MLSTM_QK_DIM, MLSTM_V_DIM + ONES_LANES), F32),
            pltpu.VMEM((MLSTM_HEADS, 1), F32),
            pltpu.VMEM((MLSTM_HEADS, tile_chunks, MLSTM_QK_DIM, CHUNK), BF16),
            pltpu.VMEM((tile_chunks, MLSTM_HEADS, CHUNK, CHUNK + MLSTM_QK_DIM), BF16),
            pltpu.VMEM((tile_chunks, MLSTM_HEADS, MLSTM_QK_DIM, CHUNK), BF16),
            pltpu.VMEM((tile_chunks, MLSTM_HEADS, CHUNK, CHUNK), F32),
        ],
        compiler_params=_cparams(("parallel", "arbitrary")),
        name="mlstm",
    )(b_if.reshape(-1, 1), s4d, s4d, s4d, gates_t)
    return out.reshape(MLSTM_HEADS, batch * seq, SLAB)


MERGE_TM = 512
OUT_TM = 512


def _weight_spec():
    return pl.BlockSpec((D_MODEL, D_MODEL), lambda i: (0, 0), pipeline_mode=pl.Buffered(1))


def _branch_merge_kernel(attn_ref, az_ref, cell_ref, mo_ref, mz_ref, ga_ref, gm_ref,
                         wa_ref, wm_ref, ng_ref, merged_ref):
    n = GROUP_SLABS
    gated = []
    for c in range(n):
        attn = jnp.concatenate([attn_ref[2 * c].astype(F32), attn_ref[2 * c + 1].astype(F32)], axis=1)
        gated.append(attn * az_ref[c].astype(F32))
    gated = jnp.concatenate(gated, axis=1).astype(BF16)
    y_a, mem = [], []
    for h in range(MLSTM_HEADS):
        cols = slice(h * SLAB, (h + 1) * SLAB)
        y_a.append(jnp.dot(gated, wa_ref[:, cols], preferred_element_type=F32))
        cell = mo_ref[h].astype(F32) * cell_ref[h].astype(F32)
        cell = cell * lax.rsqrt(jnp.mean(cell * cell, axis=-1, keepdims=True) + NORM_EPS)
        cell = cell * ng_ref[:, cols]
        mem.append(cell * mz_ref[h].astype(F32))
    mem = jnp.concatenate(mem, axis=1).astype(BF16)
    for c in range(n):
        cols = slice(c * SLAB, (c + 1) * SLAB)
        y_m = jnp.dot(mem, wm_ref[:, cols], preferred_element_type=F32)
        merged = ga_ref[c].astype(F32) * y_a[c] + gm_ref[c].astype(F32) * y_m
        merged_ref[:, cols] = merged.astype(merged_ref.dtype)


def _branch_merge(attn, slabs, cell, w_a, w_m, norm_g):
    t = attn.shape[1]
    assert MLSTM_V_DIM == SLAB and MLSTM_HEADS == GROUP_SLABS

    def slab_spec(group):
        return pl.BlockSpec((GROUP_SLABS, MERGE_TM, SLAB), lambda i: (group, i, 0))

    return pl.pallas_call(
        _branch_merge_kernel,
        grid=(t // MERGE_TM,),
        in_specs=[
            pl.BlockSpec((ATTN_HEADS, MERGE_TM, ATTN_HEAD_DIM), lambda i: (0, i, 0)),
            slab_spec(S_AZ), slab_spec(0), slab_spec(S_MO), slab_spec(S_MZ),
            slab_spec(S_GA), slab_spec(S_GM),
            _weight_spec(), _weight_spec(),
            pl.BlockSpec((1, D_MODEL), lambda i: (0, 0)),
        ],
        out_specs=pl.BlockSpec((MERGE_TM, D_MODEL), lambda i: (i, 0)),
        out_shape=jax.ShapeDtypeStruct((t, D_MODEL), BF16),
        compiler_params=_cparams(("parallel",)),
        name="branch_merge",
    )(attn, slabs, cell, slabs, slabs, slabs, slabs, w_a, w_m, norm_g)


def _out_proj_kernel(merged_ref, x_ref, wo_ref, fg_ref, out_ref):
    y = x_ref[...] + jnp.dot(merged_ref[...], wo_ref[...], preferred_element_type=F32)
    y = y * lax.rsqrt(jnp.mean(y * y, axis=-1, keepdims=True) + NORM_EPS)
    out_ref[...] = y * fg_ref[...]


def _out_proj(merged, x2, w_o, final_g):
    t = x2.shape[0]
    return pl.pallas_call(
        _out_proj_kernel,
        grid=(t // OUT_TM,),
        in_specs=[
            pl.BlockSpec((OUT_TM, D_MODEL), lambda i: (i, 0)),
            pl.BlockSpec((OUT_TM, D_MODEL), lambda i: (i, 0)),
            _weight_spec(),
            pl.BlockSpec((1, D_MODEL), lambda i: (0, 0)),
        ],
        out_specs=pl.BlockSpec((OUT_TM, D_MODEL), lambda i: (i, 0)),
        out_shape=jax.ShapeDtypeStruct((t, D_MODEL), F32),
        compiler_params=_cparams(("parallel",)),
        name="out_proj",
    )(merged, x2, w_o, final_g)


def kernel(x, norm_g, w_in, b_if, conv_w, conv_b, mlstm_norm_g, w_attn_branch, w_mlstm_branch,
           w_out, final_norm_g):
    batch, seq, d = x.shape
    assert d == D_MODEL and seq % (16 * BAND) == 0 and (batch * seq) % IN_TM == 0
    t = batch * seq
    x2 = x.reshape(t, d)

    assert w_in.shape[1] == QKV_COLS + N_SLAB_GROUPS * D_MODEL + GATE_COLS
    qkv, slabs, gates_t = _in_proj(x2, norm_g.reshape(1, d), w_in.T, conv_w, conv_b.reshape(1, -1), seq)
    attn = _attention(qkv, batch, seq)
    cell = _mlstm(slabs, gates_t, b_if, batch, seq)
    merged = _branch_merge(attn, slabs, cell, w_attn_branch.astype(BF16), w_mlstm_branch.astype(BF16),
                           mlstm_norm_g.reshape(1, d))
    out = _out_proj(merged, x2, w_out.astype(BF16), final_norm_g.reshape(1, d))
    return out.reshape(batch, seq, d)
```

```python
import functools

import jax
import jax.numpy as jnp
from jax import lax
from jax.experimental import pallas as pl
from jax.experimental.pallas import tpu as pltpu

F32 = jnp.float32
BF16 = jnp.bfloat16

D_MODEL = 2048
ATTN_HEADS = 16
ATTN_HEAD_DIM = 128
MLSTM_HEADS = 8
MLSTM_QK_DIM = 128
MLSTM_V_DIM = 256
CONV_WIDTH = 4
NORM_EPS = 1e-6
BAND = 128
ATTN_PARTS = 2
SLAB = 256
GROUP_SLABS = D_MODEL // SLAB
QKV_COLS = 3 * D_MODEL
S_AZ, S_MQK, S_MV, S_MO, S_MZ, S_GA, S_GM = range(7)
N_SLAB_GROUPS = 7
GATE_LANES = 128
MASKED = -1e30
CHUNK = 128
VMEM_LIMIT = 56 * 1024 * 1024


def _cparams(sem):
    return pltpu.CompilerParams(dimension_semantics=sem, vmem_limit_bytes=VMEM_LIMIT)


LOG2E = 1.4426950408889634


def _sigmoid(x):
    return 0.5 + 0.5 * jnp.tanh(0.5 * x)


def _silu(x):
    half = 0.5 * x
    return half + half * jnp.tanh(half)


def _log_sigmoid(x):
    return jnp.minimum(x, 0.0) - jnp.log(1.0 + jnp.exp(-jnp.abs(x)))


IN_TM = 1024
IN_TN = 1024
QKV_STEPS = QKV_COLS // IN_TN

GATE_COLS = 2 * MLSTM_HEADS
GATE_COL0 = 4 * D_MODEL + 2 * MLSTM_HEADS * MLSTM_QK_DIM + MLSTM_HEADS * MLSTM_V_DIM
GATE_STEP = GATE_COL0 // IN_TN
W_RING = 3
W_AHEAD = W_RING - 1


CONV_STEP0 = QKV_STEPS + S_MQK * (D_MODEL // IN_TN)
CONV_STEPS = D_MODEL // IN_TN
CONV_HALO = 8


def _in_proj_kernel(x_hbm, g_ref, w_hbm, wg_ref, cw_ref, cb_ref, qkv_ref, p_ref, gt_ref,
                    hn_ref, halo_ref, acc_ref, stage_ref, wbuf_ref, wsem, xbuf_ref, xsem, *, tiles_per_seq):
    i = pl.program_id(0)
    n = pl.program_id(1)
    n_tiles = pl.num_programs(0)
    n_steps = pl.num_programs(1)
    flat = i * n_steps + n
    slot = flat % W_RING

    def w_copy(step):
        col_step = step % n_steps
        row0 = col_step * IN_TN + jnp.where(col_step >= GATE_STEP, GATE_COLS, 0)
        return pltpu.make_async_copy(w_hbm.at[pl.ds(pl.multiple_of(row0, GATE_COLS), IN_TN), :],
                                     wbuf_ref.at[step % W_RING], wsem.at[step % W_RING])

    def x_copy(tile):
        return pltpu.make_async_copy(x_hbm.at[pl.ds(pl.multiple_of(tile * IN_TM, IN_TM), IN_TM), :],
                                     xbuf_ref, xsem.at[0])

    @pl.when(flat == 0)
    def _():
        x_copy(0).start()
        for s in range(W_AHEAD):
            w_copy(s).start()
        halo_ref[...] = jnp.zeros_like(halo_ref)

    @pl.when(flat + W_AHEAD < n_tiles * n_steps)
    def _():
        w_copy(flat + W_AHEAD).start()

    @pl.when((n == 1) & (i + 1 < n_tiles))
    def _():
        x_copy(i + 1).start()

    w_copy(flat).wait()

    @pl.when(n == 0)
    def _():
        x_copy(i).wait()
        x = xbuf_ref[...]
        ms = jnp.mean(x * x, axis=-1, keepdims=True)
        hn = (x * lax.rsqrt(ms + NORM_EPS) * g_ref[...]).astype(BF16)
        hn_ref[...] = hn
        wg = jnp.concatenate([wg_ref[...], jnp.zeros((GATE_LANES - GATE_COLS, D_MODEL), F32)], axis=0)
        gates = lax.dot_general(hn, wg.astype(BF16), (((1,), (1,)), ((), ())),
                                preferred_element_type=F32)
        gates_t = gates.T
        for c in range(IN_TM // CHUNK):
            gt_ref[c] = gates_t[:2 * MLSTM_HEADS, c * CHUNK:(c + 1) * CHUNK]

    def slab_acc(j):
        w = wbuf_ref[slot, j * SLAB:(j + 1) * SLAB, :].astype(BF16)
        return lax.dot_general(hn_ref[...], w, (((1,), (1,)), ((), ())), preferred_element_type=F32)

    @pl.when(n < QKV_STEPS)
    def _():
        scale = jnp.where(n < D_MODEL // IN_TN, ATTN_HEAD_DIM ** -0.5 * LOG2E, 1.0).astype(F32)
        quarter = IN_TM // 4
        for j in range(IN_TN // SLAB):
            acc = slab_acc(j) * scale
            for hh in range(SLAB // ATTN_HEAD_DIM):
                slot = (2 * j + hh) % 2
                stage_ref[slot] = acc[:, hh * ATTN_HEAD_DIM:(hh + 1) * ATTN_HEAD_DIM]
                for p in range(4):
                    qkv_ref[2 * j + hh, p * quarter:(p + 1) * quarter, :] = (
                        stage_ref[slot, pl.ds(p, quarter, stride=4), :])

    is_conv = (n >= CONV_STEP0) & (n < CONV_STEP0 + CONV_STEPS)

    group = (n - QKV_STEPS) // (D_MODEL // IN_TN)
    is_sigmoid = (group == S_MO) | (group == S_GA) | (group == S_GM)

    @pl.when(group == S_MV)
    def _():
        for j in range(IN_TN // SLAB):
            p_ref[j] = slab_acc(j).astype(BF16)

    @pl.when((n >= QKV_STEPS) & ((group == S_AZ) | (group == S_MZ)))
    def _():
        for j in range(IN_TN // SLAB):
            p_ref[j] = _silu(slab_acc(j)).astype(BF16)

    @pl.when(is_sigmoid)
    def _():
        for j in range(IN_TN // SLAB):
            p_ref[j] = _sigmoid(slab_acc(j)).astype(BF16)

    @pl.when(is_conv)
    def _():
        step = n - CONV_STEP0
        seq_start = (i % tiles_per_seq) == 0
        post = jnp.where(step == 1, MLSTM_QK_DIM ** -0.5, 1.0).astype(F32)

        def conv_slab(j):
            cols = slice(j * SLAB, (j + 1) * SLAB)
            slot = (n + j) % 2
            acc_ref[slot, 0:CONV_HALO, :] = jnp.where(seq_start, 0.0, halo_ref[step, :, cols])
            halo_ref[step, :, cols] = acc_ref[slot, IN_TM:IN_TM + CONV_HALO, :]
            y = cb_ref[:, cols]
            for back in range(CONV_WIDTH):
                tap = acc_ref[slot, CONV_HALO - back:CONV_HALO - back + IN_TM, :]
                y = y + cw_ref[CONV_WIDTH - 1 - back:CONV_WIDTH - back, cols] * tap
            p_ref[j] = (_silu(y) * post).astype(BF16)

        n_slabs = IN_TN // SLAB
        for j in range(n_slabs):
            acc_ref[(n + j) % 2, CONV_HALO:, :] = slab_acc(j)
            if j > 0:
                conv_slab(j - 1)
        conv_slab(n_slabs - 1)


def _in_proj(x2, norm_g, w_t, conv_w, conv_b, seq):
    t = x2.shape[0]
    n_cols = w_t.shape[0] - GATE_COLS
    grid = (t // IN_TM, n_cols // IN_TN)
    assert seq % IN_TM == 0 and CONV_STEPS * IN_TN == conv_w.shape[1]
    assert GATE_COL0 == (QKV_STEPS + S_MO * (D_MODEL // IN_TN)) * IN_TN and n_cols % IN_TN == 0

    def conv_step(i, n):
        return (0, jnp.clip(n - CONV_STEP0, 0, CONV_STEPS - 1))

    return pl.pallas_call(
        functools.partial(_in_proj_kernel, tiles_per_seq=seq // IN_TM),
        grid=grid,
        in_specs=[
            pl.BlockSpec(memory_space=pl.ANY),
            pl.BlockSpec((1, D_MODEL), lambda i, n: (0, 0)),
            pl.BlockSpec(memory_space=pl.ANY),
            pl.BlockSpec((GATE_COLS, D_MODEL), lambda i, n: (GATE_COL0 // GATE_COLS, 0)),
            pl.BlockSpec((CONV_WIDTH, IN_TN), conv_step),
            pl.BlockSpec((1, IN_TN), conv_step),
        ],
        out_specs=[
            pl.BlockSpec((IN_TN // ATTN_HEAD_DIM, IN_TM, ATTN_HEAD_DIM),
                         lambda i, n: (jnp.minimum(n, QKV_STEPS - 1), i, 0)),
            pl.BlockSpec((IN_TN // SLAB, IN_TM, SLAB),
                         lambda i, n: (jnp.maximum(n - QKV_STEPS, 0), i, 0)),
            pl.BlockSpec((IN_TM // CHUNK, 2 * MLSTM_HEADS, CHUNK), lambda i, n: (i, 0, 0)),
        ],
        out_shape=[
            jax.ShapeDtypeStruct((QKV_COLS // ATTN_HEAD_DIM, t, ATTN_HEAD_DIM), F32),
            jax.ShapeDtypeStruct(((n_cols - QKV_COLS) // SLAB, t, SLAB), BF16),
            jax.ShapeDtypeStruct((t // CHUNK, 2 * MLSTM_HEADS, CHUNK), F32),
        ],
        scratch_shapes=[pltpu.VMEM((IN_TM, D_MODEL), BF16),
                        pltpu.VMEM((CONV_STEPS, CONV_HALO, IN_TN), F32),
                        pltpu.VMEM((2, CONV_HALO + IN_TM, SLAB), F32),
                        pltpu.VMEM((2, IN_TM, ATTN_HEAD_DIM), F32),
                        pltpu.VMEM((W_RING, IN_TN, D_MODEL), F32),
                        pltpu.SemaphoreType.DMA((W_RING,)),
                        pltpu.VMEM((IN_TM, D_MODEL), F32),
                        pltpu.SemaphoreType.DMA((1,))],
        compiler_params=_cparams(("arbitrary", "arbitrary")),
        name="in_proj",
    )(x2, norm_g, w_t, w_t, conv_w, conv_b)


def _band_bias(slope, dilation, first):
    qi = lax.broadcasted_iota(jnp.int32, (BAND, 2 * BAND), 0)
    ki = lax.broadcasted_iota(jnp.int32, (BAND, 2 * BAND), 1)
    dist = BAND + qi - ki
    valid = (dist >= 0) & (dist <= BAND)
    if first:
        valid = valid & (ki >= BAND)
    return jnp.where(valid, -slope * (dist * dilation).astype(F32), MASKED)


def _dense_bias(slope, first):
    qi = lax.broadcasted_iota(jnp.int32, (BAND, 2 * BAND), 0)
    ki = lax.broadcasted_iota(jnp.int32, (BAND, 2 * BAND), 1)
    pq, jq = qi // 32, qi % 32
    half, pk, jk = ki // BAND, (ki % BAND) // 32, ki % 32
    dist = 4 * (jq - jk + 32 * (1 - half)) + pq - pk
    valid = (dist >= 0) & (dist <= BAND)
    if first:
        valid = valid & (half == 1)
    return jnp.where(valid, -slope * dist.astype(F32), MASKED)


def _wide_bias(slope):
    qi = lax.broadcasted_iota(jnp.int32, (2 * BAND, 2 * BAND), 0)
    ki = lax.broadcasted_iota(jnp.int32, (2 * BAND, 2 * BAND), 1)
    dist = qi - ki
    valid = (dist >= 0) & (dist <= BAND)
    return jnp.where(valid, -slope * (dist * 16).astype(F32), MASKED)


def _softmax_block(q, k, v, bias):
    s = lax.dot_general(q, k, (((1,), (1,)), ((), ())), preferred_element_type=F32) + bias
    m = jnp.max(s, axis=-1, keepdims=True)
    p = jnp.exp2(s - m).astype(BF16)
    v_ext = jnp.concatenate([v, jnp.ones_like(v)], axis=1)
    r = jnp.dot(p, v_ext, preferred_element_type=F32)
    hd = ATTN_HEAD_DIM
    return r[:, :hd], r[:, hd:], jnp.broadcast_to(m, (q.shape[0], hd))


def _attn_kernel(q_ref, k_ref, v_ref, o_ref,
                 x4_ref, x16_ref, num_ref, den_ref, mx_ref, bias_ref, bias16_ref, unit_ref, unit16_ref,
                 onat_ref):
    seq = q_ref.shape[0]
    n_u4, n_u16 = seq // 4, seq // 16
    slope = jnp.exp2(jnp.full((1, 1), -8.0 / ATTN_HEADS, F32) * (pl.program_id(1) + 1)) * LOG2E

    @pl.when((pl.program_id(0) == 0) & (pl.program_id(1) == 0))
    def _():
        one = jnp.ones((1, 1), F32)
        unit_ref[0, 0] = _dense_bias(one, False)
        unit_ref[0, 1] = _dense_bias(one, True)
        unit_ref[1, 0] = _band_bias(one, 4, False)
        unit_ref[1, 1] = _band_bias(one, 4, True)
        unit16_ref[...] = _wide_bias(one)

    bias_ref[...] = unit_ref[...] * slope
    bias16_ref[...] = unit16_ref[...] * slope

    quarter = IN_TM // 4
    sixteenth = IN_TM // 16
    for ti, src in enumerate((q_ref, k_ref, v_ref)):
        def regroup(tile, carry, ti=ti, src=src):
            base = pl.multiple_of(tile * IN_TM, IN_TM)
            for p4 in range(4):
                for piece in range(quarter // BAND):
                    rows = src[pl.ds(base + p4 * quarter + piece * BAND, BAND), :]
                    dst = pl.ds(pl.multiple_of(tile * quarter + piece * BAND, BAND), BAND)
                    x4_ref[ti, p4, dst, :] = rows.astype(BF16)
            for p16 in range(16):
                rows = src[pl.ds(base + (p16 % 4) * quarter + p16 // 4, sixteenth, stride=4), :]
                x16_ref[ti, p16, pl.ds(pl.multiple_of(tile * sixteenth, sixteenth), sixteenth), :] = rows.astype(BF16)
            return carry

        lax.fori_loop(0, seq // IN_TM, regroup, 0)

    def store(pattern, p, rows, result):
        num, den, mx = result
        num_ref[pattern, p, rows, :] = num
        den_ref[pattern, p, rows, :] = den
        mx_ref[pattern, p, rows, :] = mx

    def dense_block(n):
        cur, prev = slice(n * 32, (n + 1) * 32), slice(max(n - 1, 0) * 32, max(n - 1, 0) * 32 + 32)

        def rows(ti, sl):
            return [x4_ref[ti, p, sl, :] for p in range(4)]

        q = jnp.concatenate(rows(0, cur), axis=0)
        k = jnp.concatenate(rows(1, prev) + rows(1, cur), axis=0)
        v = jnp.concatenate(rows(2, prev) + rows(2, cur), axis=0)
        num, den, mx = _softmax_block(q, k, v, bias_ref[0, int(n == 0)])
        for p in range(4):
            part = slice(p * 32, (p + 1) * 32)
            store(0, p, cur, (num[part], den[part], mx[part]))

    def band_block(p, n):
        cur = slice(n * BAND, (n + 1) * BAND)
        prev = slice(max(n - 1, 0) * BAND, max(n - 1, 0) * BAND + BAND)
        q = x4_ref[0, p, cur, :]
        k = jnp.concatenate([x4_ref[1, p, prev, :], x4_ref[1, p, cur, :]], axis=0)
        v = jnp.concatenate([x4_ref[2, p, prev, :], x4_ref[2, p, cur, :]], axis=0)
        store(1, p, cur, _softmax_block(q, k, v, bias_ref[1, int(n == 0)]))

    wide_rows = n_u16 // ATTN_PARTS

    def wide_block(p16, part):
        rows = slice(part * wide_rows, (part + 1) * wide_rows)
        result = _softmax_block(x16_ref[0, p16, rows, :], x16_ref[1, p16], x16_ref[2, p16], bias16_ref[rows, :])
        store(2, p16 % 4, pl.ds(p16 // 4 + 4 * wide_rows * part, wide_rows, stride=4), result)

    def merge_block(p, n):
        rows = slice(n * BAND, (n + 1) * BAND)
        m0, m1, m2 = mx_ref[0, p, rows, :], mx_ref[1, p, rows, :], mx_ref[2, p, rows, :]
        m_all = jnp.maximum(jnp.maximum(m0, m1), m2)
        w0, w1, w2 = jnp.exp2(m0 - m_all), jnp.exp2(m1 - m_all), jnp.exp2(m2 - m_all)
        num = w0 * num_ref[0, p, rows, :] + w1 * num_ref[1, p, rows, :] + w2 * num_ref[2, p, rows, :]
        den = w0 * den_ref[0, p, rows, :] + w1 * den_ref[1, p, rows, :] + w2 * den_ref[2, p, rows, :]
        onat_ref[pl.ds(p + 4 * BAND * n, BAND, stride=4), :] = num / den

    dense_per_part = n_u4 // 32 // ATTN_PARTS
    band_per_part = n_u4 // BAND // ATTN_PARTS
    for part in range(ATTN_PARTS):
        for n in range(part * dense_per_part, (part + 1) * dense_per_part):
            dense_block(n)
        for n in range(part * band_per_part, (part + 1) * band_per_part):
            for p in range(4):
                band_block(p, n)
                if part > 0:
                    merge_block(p, n - band_per_part)
        for p16 in range(16):
            wide_block(p16, part)
    for n in range((ATTN_PARTS - 1) * band_per_part, ATTN_PARTS * band_per_part):
        for p in range(4):
            merge_block(p, n)
    o_ref[...] = onat_ref[...].astype(o_ref.dtype)


def _attention(qkv, batch, seq):
    assert seq // 16 == 2 * BAND
    hd = ATTN_HEAD_DIM
    qkv4 = qkv.reshape(3 * ATTN_HEADS, batch, seq, hd)

    def spec(which):
        return pl.BlockSpec((None, None, seq, hd), lambda b, h: (which * ATTN_HEADS + h, b, 0, 0))

    out = pl.pallas_call(
        _attn_kernel,
        grid=(batch, ATTN_HEADS),
        in_specs=[spec(0), spec(1), spec(2)],
        out_specs=pl.BlockSpec((None, None, seq, hd), lambda b, h: (h, b, 0, 0)),
        out_shape=jax.ShapeDtypeStruct((ATTN_HEADS, batch, seq, hd), BF16),
        scratch_shapes=[
            pltpu.VMEM((3, 4, seq // 4, hd), BF16),
            pltpu.VMEM((3, 16, seq // 16, hd), BF16),
            pltpu.VMEM((3, 4, seq // 4, hd), F32),
            pltpu.VMEM((3, 4, seq // 4, hd), F32),
            pltpu.VMEM((3, 4, seq // 4, hd), F32),
            pltpu.VMEM((2, 2, BAND, 2 * BAND), F32),
            pltpu.VMEM((2 * BAND, 2 * BAND), F32),
            pltpu.VMEM((2, 2, BAND, 2 * BAND), F32),
            pltpu.VMEM((2 * BAND, 2 * BAND), F32),
            pltpu.VMEM((seq, hd), F32),
        ],
        compiler_params=_cparams(("arbitrary", "arbitrary")),
        name="dilated_attention",
    )(qkv4, qkv4, qkv4)
    return out.reshape(ATTN_HEADS, batch * seq, hd)


MLSTM_TS = 512
ONES_LANES = 128


def _mlstm_kernel(bias_ref, q_ref, k_ref, v_ref, gt_ref, out_ref, c_ref, m_ref, kt_ref,
                  lhs_ref, kw_ref, floor_ref):
    s = pl.program_id(1)
    dk, dv = MLSTM_QK_DIM, MLSTM_V_DIM
    L = CHUNK

    @pl.when(s == 0)
    def _():
        c_ref[...] = jnp.zeros_like(c_ref)
        m_ref[...] = jnp.zeros_like(m_ref)

    ti = lax.broadcasted_iota(jnp.int32, (CHUNK, CHUNK), 0)
    si = lax.broadcasted_iota(jnp.int32, (CHUNK, CHUNK), 1)
    causal = ti >= si
    csum = jnp.where(ti <= si, 1.0, 0.0).astype(F32)

    lane = lax.broadcasted_iota(jnp.int32, (MLSTM_HEADS, L), 1)

    for h in range(MLSTM_HEADS):
        for c in range(MLSTM_TS // L):
            k_hc = k_ref[h // 2, c * L:(c + 1) * L, (h % 2) * dk:(h % 2 + 1) * dk]
            kt_ref[h, c] = k_hc.astype(F32).T.astype(BF16)

    n_chunks = MLSTM_TS // L
    heads = range(MLSTM_HEADS)

    def q_of(h, c):
        return q_ref[h // 2, c * L:(c + 1) * L, (h % 2) * dk:(h % 2 + 1) * dk]

    def v_ext_of(h, c):
        return jnp.concatenate([v_ref[h, c * L:(c + 1) * L, :], jnp.ones((L, ONES_LANES), BF16)], axis=1)

    w_olds = []
    for c in range(n_chunks):
        gates = gt_ref[c] + bias_ref[...]
        ig_all = gates[:MLSTM_HEADS]
        lf_all = _log_sigmoid(gates[MLSTM_HEADS:])
        b_all = jnp.dot(lf_all, csum, preferred_element_type=F32,
                        precision=lax.Precision.HIGHEST)
        c_all = ig_all - b_all
        cm_all = c_all
        shift = 1
        while shift < L:
            cm_all = jnp.maximum(cm_all, jnp.where(lane >= shift, pltpu.roll(cm_all, shift, axis=1), MASKED))
            shift *= 2
        m_prev = m_ref[...]
        b_last = b_all[:, L - 1:L]
        top = jnp.maximum(m_prev, cm_all)
        m_new = b_last + top[:, L - 1:L]
        w_old = jnp.exp(b_last + m_prev - m_new)
        w_key = jnp.exp(b_last + c_all - m_new)
        cols = jnp.concatenate([
            -top,
            jnp.exp(m_prev - top),
            jnp.exp(-(b_all + top)),
        ], axis=0).T
        m_ref[...] = m_new
        w_olds.append(w_old)

        def column(group, h):
            j = group * MLSTM_HEADS + h
            return jnp.broadcast_to(cols[:, j:j + 1], (L, L))

        scores = [jnp.dot(q_of(h, c), kt_ref[h, c], preferred_element_type=F32) for h in heads]
        for h in heads:
            w_intra = jnp.exp(jnp.where(causal, column(0, h) + c_all[h:h + 1], MASKED))
            lhs_ref[c, h] = jnp.concatenate([(scores[h] * w_intra).astype(BF16),
                                             (q_of(h, c).astype(F32) * column(1, h)).astype(BF16)], axis=1)
            kw_ref[c, h] = (kt_ref[h, c].astype(F32) * w_key[h:h + 1]).astype(BF16)
            floor_ref[c, h] = column(2, h)

    for c in range(n_chunks):
        incs = [jnp.dot(kw_ref[c, h], v_ext_of(h, c), preferred_element_type=F32) for h in heads]
        nds = [jnp.dot(lhs_ref[c, h], jnp.concatenate([v_ext_of(h, c), c_ref[h].astype(BF16)], axis=0),
                       preferred_element_type=F32) for h in heads]
        for h in heads:
            nd = nds[h]
            inv = 1.0 / jnp.maximum(jnp.abs(nd[:, dv:]), floor_ref[c, h])
            for half in range(dv // L):
                lanes = slice(half * L, (half + 1) * L)
                out_ref[h, c * L:(c + 1) * L, lanes] = (nd[:, lanes] * inv).astype(out_ref.dtype)
            c_ref[h] = w_olds[c][h:h + 1] * c_ref[h] + incs[h]


def _mlstm(slabs, gates_t, b_if, batch, seq):
    n_slabs = slabs.shape[0]
    s4d = slabs.reshape(n_slabs, batch, seq, SLAB)
    half = MLSTM_HEADS // 2
    tiles = seq // MLSTM_TS
    tile_chunks = MLSTM_TS // CHUNK

    def tile_spec(first_slab, n):
        return pl.BlockSpec((n, None, MLSTM_TS, SLAB), lambda b, s: (first_slab // n, b, s, 0))

    out = pl.pallas_call(
        _mlstm_kernel,
        grid=(batch, tiles),
        in_specs=[
            pl.BlockSpec((2 * MLSTM_HEADS, 1), lambda b, s: (0, 0)),
            tile_spec(S_MQK * GROUP_SLABS, half), tile_spec(S_MQK * GROUP_SLABS + half, half),
            tile_spec(S_MV * GROUP_SLABS, MLSTM_HEADS),
            pl.BlockSpec((tile_chunks, 2 * MLSTM_HEADS, CHUNK), lambda b, s: (b * tiles + s, 0, 0)),
        ],
        out_specs=pl.BlockSpec((MLSTM_HEADS, None, MLSTM_TS, SLAB), lambda b, s: (0, b, s, 0)),
        out_shape=jax.ShapeDtypeStruct((MLSTM_HEADS, batch, seq, SLAB), BF16),
        scratch_shapes=[
            pltpu.VMEM((MLSTM_HEADS, MLSTM_QK_DIM, MLSTM_V_DIM + ONES_LANES), F32),
            pltpu.VMEM((MLSTM_HEADS, 1), F32),
            pltpu.VMEM((MLSTM_HEADS, tile_chunks, MLSTM_QK_DIM, CHUNK), BF16),
            pltpu.VMEM((tile_chunks, MLSTM_HEADS, CHUNK, CHUNK + MLSTM_QK_DIM), BF16),
            pltpu.VMEM((tile_chunks, MLSTM_HEADS, MLSTM_QK_DIM, CHUNK), BF16),
            pltpu.VMEM((tile_chunks, MLSTM_HEADS, CHUNK, CHUNK), F32),
        ],
        compiler_params=_cparams(("parallel", "arbitrary")),
        name="mlstm",
    )(b_if.reshape(-1, 1), s4d, s4d, s4d, gates_t)
    return out.reshape(MLSTM_HEADS, batch * seq, SLAB)


MERGE_TM = 512
OUT_TM = 512


def _weight_spec():
    return pl.BlockSpec((D_MODEL, D_MODEL), lambda i: (0, 0), pipeline_mode=pl.Buffered(1))


def _branch_merge_kernel(attn_ref, az_ref, cell_ref, mo_ref, mz_ref, ga_ref, gm_ref,
                         wa_ref, wm_ref, ng_ref, merged_ref):
    n = GROUP_SLABS
    gated = []
    for c in range(n):
        attn = jnp.concatenate([attn_ref[2 * c].astype(F32), attn_ref[2 * c + 1].astype(F32)], axis=1)
        gated.append(attn * az_ref[c].astype(F32))
    gated = jnp.concatenate(gated, axis=1).astype(BF16)
    y_a, mem = [], []
    for h in range(MLSTM_HEADS):
        cols = slice(h * SLAB, (h + 1) * SLAB)
        y_a.append(jnp.dot(gated, wa_ref[:, cols], preferred_element_type=F32))
        cell = mo_ref[h].astype(F32) * cell_ref[h].astype(F32)
        cell = cell * lax.rsqrt(jnp.mean(cell * cell, axis=-1, keepdims=True) + NORM_EPS)
        cell = cell * ng_ref[:, cols]
        mem.append(cell * mz_ref[h].astype(F32))
    mem = jnp.concatenate(mem, axis=1).astype(BF16)
    for c in range(n):
        cols = slice(c * SLAB, (c + 1) * SLAB)
        y_m = jnp.dot(mem, wm_ref[:, cols], preferred_element_type=F32)
        merged = ga_ref[c].astype(F32) * y_a[c] + gm_ref[c].astype(F32) * y_m
        merged_ref[:, cols] = merged.astype(merged_ref.dtype)


def _branch_merge(attn, slabs, cell, w_a, w_m, norm_g):
    t = attn.shape[1]
    assert MLSTM_V_DIM == SLAB and MLSTM_HEADS == GROUP_SLABS

    def slab_spec(group):
        return pl.BlockSpec((GROUP_SLABS, MERGE_TM, SLAB), lambda i: (group, i, 0))

    return pl.pallas_call(
        _branch_merge_kernel,
        grid=(t // MERGE_TM,),
        in_specs=[
            pl.BlockSpec((ATTN_HEADS, MERGE_TM, ATTN_HEAD_DIM), lambda i: (0, i, 0)),
            slab_spec(S_AZ), slab_spec(0), slab_spec(S_MO), slab_spec(S_MZ),
            slab_spec(S_GA), slab_spec(S_GM),
            _weight_spec(), _weight_spec(),
            pl.BlockSpec((1, D_MODEL), lambda i: (0, 0)),
        ],
        out_specs=pl.BlockSpec((MERGE_TM, D_MODEL), lambda i: (i, 0)),
        out_shape=jax.ShapeDtypeStruct((t, D_MODEL), BF16),
        compiler_params=_cparams(("parallel",)),
        name="branch_merge",
    )(attn, slabs, cell, slabs, slabs, slabs, slabs, w_a, w_m, norm_g)


OUT_RING = 3


def _out_proj_kernel(merged_hbm, x_hbm, wo_ref, fg_ref, out_ref, mbuf_ref, xbuf_ref, sem):
    i = pl.program_id(0)
    n_steps = pl.num_programs(0)

    def copies(s):
        rows = pl.ds(pl.multiple_of(s * OUT_TM, OUT_TM), OUT_TM)
        slot = s % OUT_RING
        return (pltpu.make_async_copy(merged_hbm.at[rows, :], mbuf_ref.at[slot], sem.at[0, slot]),
                pltpu.make_async_copy(x_hbm.at[rows, :], xbuf_ref.at[slot], sem.at[1, slot]))

    @pl.when(i == 0)
    def _():
        for s in range(OUT_RING - 1):
            for c in copies(s):
                c.start()

    @pl.when(i + OUT_RING - 1 < n_steps)
    def _():
        for c in copies(i + OUT_RING - 1):
            c.start()

    for c in copies(i):
        c.wait()
    slot = i % OUT_RING
    y = xbuf_ref[slot] + jnp.dot(mbuf_ref[slot], wo_ref[...], preferred_element_type=F32)
    y = y * lax.rsqrt(jnp.mean(y * y, axis=-1, keepdims=True) + NORM_EPS)
    out_ref[...] = y * fg_ref[...]


def _out_proj(merged, x2, w_o, final_g):
    t = x2.shape[0]
    return pl.pallas_call(
        _out_proj_kernel,
        grid=(t // OUT_TM,),
        in_specs=[
            pl.BlockSpec(memory_space=pl.ANY),
            pl.BlockSpec(memory_space=pl.ANY),
            _weight_spec(),
            pl.BlockSpec((1, D_MODEL), lambda i: (0, 0)),
        ],
        out_specs=pl.BlockSpec((OUT_TM, D_MODEL), lambda i: (i, 0)),
        out_shape=jax.ShapeDtypeStruct((t, D_MODEL), F32),
        scratch_shapes=[pltpu.VMEM((OUT_RING, OUT_TM, D_MODEL), BF16),
                        pltpu.VMEM((OUT_RING, OUT_TM, D_MODEL), F32),
                        pltpu.SemaphoreType.DMA((2, OUT_RING))],
        compiler_params=_cparams(("arbitrary",)),
        name="out_proj",
    )(merged, x2, w_o, final_g)


def kernel(x, norm_g, w_in, b_if, conv_w, conv_b, mlstm_norm_g, w_attn_branch, w_mlstm_branch,
           w_out, final_norm_g):
    batch, seq, d = x.shape
    assert d == D_MODEL and seq % (16 * BAND) == 0 and (batch * seq) % IN_TM == 0
    t = batch * seq
    x2 = x.reshape(t, d)

    assert w_in.shape[1] == QKV_COLS + N_SLAB_GROUPS * D_MODEL + GATE_COLS
    qkv, slabs, gates_t = _in_proj(x2, norm_g.reshape(1, d), w_in.T, conv_w, conv_b.reshape(1, -1), seq)
    attn = _attention(qkv, batch, seq)
    cell = _mlstm(slabs, gates_t, b_if, batch, seq)
    merged = _branch_merge(attn, slabs, cell, w_attn_branch.astype(BF16), w_mlstm_branch.astype(BF16),
                           mlstm_norm_g.reshape(1, d))
    out = _out_proj(merged, x2, w_out.astype(BF16), final_norm_g.reshape(1, d))
    return out.reshape(batch, seq, d)
```
